```python
import jax, jax.numpy as jnp
from jax import lax
import numpy as np

D_MODEL = 1024
BATCH = 16
SEQ = 2048
DEPTH = 1
DEC_BATCH = 128
DEC_SEQ = 1
PAST_LEN = 16384
PAGE_SIZE = 128

HEAD_DIM = 64
N_HEADS_A = 16
N_KV_A = 2
GROUP_A = N_HEADS_A // N_KV_A
WINDOW_A = 128
DILATED_PAIRS = ((128, 1), (512, 4), (2048, 16))
N_GROUPS_B = 3
HEADS_PER_GROUP_B = 4
SPAN = 128
BLOCK = 128
D_FF = ((8 * D_MODEL + 2) // 3 + 255) // 256 * 256
ROPE_THETA = 10000.0
NORM_EPS = 1e-6
NEG_INF = -1e30
Q_A_WIDTH = N_HEADS_A * HEAD_DIM
KV_A_WIDTH = N_KV_A * HEAD_DIM
B_WIDTH = N_GROUPS_B * HEADS_PER_GROUP_B * HEAD_DIM
OUT_B_WIDTH = HEADS_PER_GROUP_B * HEAD_DIM
N_IN = Q_A_WIDTH + 2 * KV_A_WIDTH + 3 * B_WIDTH + 2 * D_MODEL

kernel_name = "gated_swa_sink_dilated_hybrid_step"


def _rmsnorm(x, g):
    xf = x.astype(jnp.float32)
    y = xf * lax.rsqrt(jnp.mean(xf * xf, axis=-1, keepdims=True) + NORM_EPS)
    return (y * g.astype(jnp.float32)).astype(x.dtype)


def _rope(x, pos):
    half = HEAD_DIM // 2
    inv = 1.0 / (ROPE_THETA ** (jnp.arange(half, dtype=jnp.float32) / half))
    ang = pos.astype(jnp.float32)[:, None] * inv[None, :]
    ang = jnp.concatenate([ang, ang], axis=-1)
    shape = (1, pos.shape[0]) + (1,) * (x.ndim - 3) + (HEAD_DIM,)
    cos = jnp.cos(ang).reshape(shape)
    sin = jnp.sin(ang).reshape(shape)
    xf = x.astype(jnp.float32)
    rot = jnp.concatenate([-xf[..., half:], xf[..., :half]], axis=-1)
    return (xf * cos + rot * sin).astype(x.dtype)


def _lse(s, sink):
    lse = jax.nn.logsumexp(s, axis=-1)
    if sink is not None:
        lse = jnp.logaddexp(lse, sink.astype(jnp.float32))
    return lse


def _banded_attention(q, k, v, sink):
    N, L, Hk, G, Dh = q.shape
    nb = -(-L // BLOCK)
    Lp = nb * BLOCK
    pe = Lp - L
    qb = jnp.pad(q, ((0, 0), (0, pe), (0, 0), (0, 0), (0, 0))).reshape(N, nb, BLOCK, Hk, G, Dh)
    kb = jnp.pad(k, ((0, 0), (BLOCK, pe), (0, 0), (0, 0))).reshape(N, nb + 1, BLOCK, Hk, Dh)
    vb = jnp.pad(v, ((0, 0), (BLOCK, pe), (0, 0), (0, 0))).reshape(N, nb + 1, BLOCK, Hk, Dh)
    k_band = jnp.concatenate([kb[:, :-1], kb[:, 1:]], axis=2)
    v_band = jnp.concatenate([vb[:, :-1], vb[:, 1:]], axis=2)
    s = jnp.einsum('nbqhgd,nbkhd->nbhgqk', qb, k_band, preferred_element_type=jnp.float32) * (Dh ** -0.5)
    qi = jnp.arange(BLOCK)
    ki = jnp.arange(2 * BLOCK)
    dist = BLOCK + qi[:, None] - ki[None, :]
    key_pos = jnp.arange(nb)[:, None] * BLOCK - BLOCK + ki[None, :]
    mask = ((dist >= 0) & (dist <= SPAN))[None] & (key_pos >= 0)[:, None, :]
    s = jnp.where(mask[None, :, None, None], s, NEG_INF)
    lse = _lse(s, None if sink is None else sink[:, :, None])
    p = jnp.exp(s - lse[..., None])
    o = jnp.einsum('nbhgqk,nbkhd->nbqhgd', p.astype(v.dtype), v_band)
    o = o.reshape(N, Lp, Hk, G, Dh)[:, :L]
    lse = jnp.transpose(lse, (0, 1, 4, 2, 3)).reshape(N, Lp, Hk, G)[:, :L]
    return o, lse


def _gathered_attention(q, k_all, v_all, n_past, dilation, sink):
    S = q.shape[1]
    idx = n_past + jnp.arange(S)[:, None] - dilation * jnp.arange(SPAN + 1)[None, :]
    valid = idx >= 0
    idx = jnp.maximum(idx, 0)
    kg = jnp.take(k_all, idx, axis=1)
    vg = jnp.take(v_all, idx, axis=1)
    s = jnp.einsum('nshgd,nsjhd->nshgj', q, kg, preferred_element_type=jnp.float32) * (q.shape[-1] ** -0.5)
    s = jnp.where(valid[None, :, None, None, :], s, NEG_INF)
    lse = _lse(s, sink)
    p = jnp.exp(s - lse[..., None])
    o = jnp.einsum('nshgj,nsjhd->nshgd', p.astype(v_all.dtype), vg)
    return o, lse


def _to_residues(x, d):
    B, S = x.shape[:2]
    rest = x.shape[2:]
    x = jnp.swapaxes(x.reshape((B, S // d, d) + rest), 1, 2)
    return x.reshape((B * d, S // d) + rest)


def _from_residues(x, B, d):
    L = x.shape[1]
    rest = x.shape[2:]
    x = jnp.swapaxes(x.reshape((B, d, L) + rest), 1, 2)
    return x.reshape((B, L * d) + rest)


def _merge_dilated(outs, lses, dtype):
    alpha = jax.nn.softmax(jnp.stack(lses, axis=0), axis=0)
    o = jnp.einsum('gbshd,gbsh->bshd', jnp.stack(outs, axis=0).astype(jnp.float32), alpha)
    return o.astype(dtype)


def _project(x, pos, norm1_g, w_in):
    B, S, _ = x.shape
    h = _rmsnorm(x, norm1_g)
    z = h @ w_in
    o1 = Q_A_WIDTH
    o2 = o1 + KV_A_WIDTH
    o3 = o2 + KV_A_WIDTH
    o4 = o3 + B_WIDTH
    o5 = o4 + B_WIDTH
    o6 = o5 + B_WIDTH
    o7 = o6 + D_MODEL
    qa = _rope(z[..., :o1].reshape(B, S, N_KV_A, GROUP_A, HEAD_DIM), pos)
    ka = _rope(z[..., o1:o2].reshape(B, S, N_KV_A, HEAD_DIM), pos)
    va = z[..., o2:o3].reshape(B, S, N_KV_A, HEAD_DIM)
    qb = _rope(z[..., o3:o4].reshape(B, S, N_GROUPS_B, HEADS_PER_GROUP_B, HEAD_DIM), pos)
    kb = _rope(z[..., o4:o5].reshape(B, S, N_GROUPS_B, HEADS_PER_GROUP_B, HEAD_DIM), pos)
    vb = z[..., o5:o6].reshape(B, S, N_GROUPS_B, HEADS_PER_GROUP_B, HEAD_DIM)
    gate_a = jax.nn.sigmoid(z[..., o6:o7])
    gate_b = jax.nn.sigmoid(z[..., o7:])
    return qa, ka, va, qb, kb, vb, gate_a, gate_b


def _finish(x, oa, ob, gate_a, gate_b, w_pa, w_pb, w_o, norm2_g, w_gu, w_down):
    B, S, _ = x.shape
    m = gate_a * (oa.reshape(B, S, Q_A_WIDTH) @ w_pa) + gate_b * (ob.reshape(B, S, OUT_B_WIDTH) @ w_pb)
    x = x + m @ w_o
    gu = _rmsnorm(x, norm2_g) @ w_gu
    return x + (jax.nn.silu(gu[..., :D_FF]) * gu[..., D_FF:]) @ w_down


def _prompt_layer(x, norm1_g, w_in, sinks, w_pa, w_pb, w_o, norm2_g, w_gu, w_down):
    B, S, _ = x.shape
    pos = jnp.arange(S, dtype=jnp.int32)
    qa, ka, va, qb, kb, vb, gate_a, gate_b = _project(x, pos, norm1_g, w_in)
    oa, _ = _banded_attention(qa, ka, va, sinks.reshape(N_KV_A, GROUP_A))
    rows_a = min(WINDOW_A, S)
    state_a = jnp.stack([ka[:, S - rows_a:], va[:, S - rows_a:]], axis=2)
    outs, lses, states_b = [], [], []
    for gi, (win, dil) in enumerate(DILATED_PAIRS):
        q = _to_residues(qb[:, :, gi], dil)[:, :, :, None, :]
        o, lse = _banded_attention(q, _to_residues(kb[:, :, gi], dil), _to_residues(vb[:, :, gi], dil), None)
        outs.append(_from_residues(o[:, :, :, 0], B, dil))
        lses.append(_from_residues(lse[..., 0], B, dil))
        rows = min(win, S)
        states_b.append(jnp.stack([kb[:, S - rows:, gi], vb[:, S - rows:, gi]], axis=2))
    ob = _merge_dilated(outs, lses, x.dtype)
    y = _finish(x, oa, ob, gate_a, gate_b, w_pa, w_pb, w_o, norm2_g, w_gu, w_down)
    return y, state_a, states_b


def _sample_layer(x, cache_a, caches_b, norm1_g, w_in, sinks, w_pa, w_pb, w_o, norm2_g, w_gu, w_down):
    B, S, _ = x.shape
    pos = PAST_LEN + jnp.arange(S, dtype=jnp.int32)
    qa, ka, va, qb, kb, vb, gate_a, gate_b = _project(x, pos, norm1_g, w_in)
    k_all = jnp.concatenate([cache_a[:, :, 0], ka], axis=1)
    v_all = jnp.concatenate([cache_a[:, :, 1], va], axis=1)
    oa, _ = _gathered_attention(qa, k_all, v_all, cache_a.shape[1], 1, sinks.reshape(N_KV_A, GROUP_A))
    rows_a = min(WINDOW_A, PAST_LEN + S)
    n_a = k_all.shape[1]
    state_a = jnp.stack([k_all[:, n_a - rows_a:], v_all[:, n_a - rows_a:]], axis=2)
    outs, lses, states_b = [], [], []
    for gi, (win, dil) in enumerate(DILATED_PAIRS):
        cb = caches_b[gi]
        kg_all = jnp.concatenate([cb[:, :, 0], kb[:, :, gi]], axis=1)
        vg_all = jnp.concatenate([cb[:, :, 1], vb[:, :, gi]], axis=1)
        o, lse = _gathered_attention(qb[:, :, gi][:, :, :, None, :], kg_all, vg_all, cb.shape[1], dil, None)
        outs.append(o[:, :, :, 0])
        lses.append(lse[..., 0])
        rows = min(win, PAST_LEN + S)
        n_g = kg_all.shape[1]
        states_b.append(jnp.stack([kg_all[:, n_g - rows:], vg_all[:, n_g - rows:]], axis=2))
    ob = _merge_dilated(outs, lses, x.dtype)
    y = _finish(x, oa, ob, gate_a, gate_b, w_pa, w_pb, w_o, norm2_g, w_gu, w_down)
    return y, state_a, states_b


def setup_inputs(seed: int = 0) -> dict:
    key = jax.random.key(seed)
    ks = jax.random.split(key, 16)
    f32 = jnp.float32
    la = min(WINDOW_A, PAST_LEN)
    lb = [min(w, PAST_LEN) for (w, _) in DILATED_PAIRS]
    nrm = jax.random.normal
    return {
        "x_prompt": nrm(ks[0], (BATCH, SEQ, D_MODEL), f32),
        "x_sample": nrm(ks[1], (DEC_BATCH, DEC_SEQ, D_MODEL), f32),
        "cache_a_kv": nrm(ks[2], (DEPTH, DEC_BATCH, la, 2, N_KV_A, HEAD_DIM), f32),
        "cache_b1_kv": nrm(ks[3], (DEPTH, DEC_BATCH, lb[0], 2, HEADS_PER_GROUP_B, HEAD_DIM), f32),
        "cache_b2_kv": nrm(ks[4], (DEPTH, DEC_BATCH, lb[1], 2, HEADS_PER_GROUP_B, HEAD_DIM), f32),
        "cache_b3_kv": nrm(ks[5], (DEPTH, DEC_BATCH, lb[2], 2, HEADS_PER_GROUP_B, HEAD_DIM), f32),
        "norm1_g": 1.0 + 0.02 * nrm(ks[6], (DEPTH, D_MODEL), f32),
        "w_in": nrm(ks[7], (DEPTH, D_MODEL, N_IN), f32) * D_MODEL ** -0.5,
        "sinks": 0.5 * nrm(ks[8], (DEPTH, N_HEADS_A), f32),
        "w_pa": nrm(ks[9], (DEPTH, Q_A_WIDTH, D_MODEL), f32) * Q_A_WIDTH ** -0.5,
        "w_pb": nrm(ks[10], (DEPTH, OUT_B_WIDTH, D_MODEL), f32) * OUT_B_WIDTH ** -0.5,
        "w_o": nrm(ks[11], (DEPTH, D_MODEL, D_MODEL), f32) * D_MODEL ** -0.5,
        "norm2_g": 1.0 + 0.02 * nrm(ks[12], (DEPTH, D_MODEL), f32),
        "w_gu": nrm(ks[13], (DEPTH, D_MODEL, 2 * D_FF), f32) * D_MODEL ** -0.5,
        "w_down": nrm(ks[14], (DEPTH, D_FF, D_MODEL), f32) * D_FF ** -0.5,
        "final_norm_g": 1.0 + 0.02 * nrm(ks[15], (D_MODEL,), f32),
    }


def reference(x_prompt, x_sample, cache_a_kv, cache_b1_kv, cache_b2_kv, cache_b3_kv, norm1_g, w_in, sinks, w_pa, w_pb, w_o, norm2_g, w_gu, w_down, final_norm_g):
    xp = x_prompt
    xs = x_sample
    pa, pb1, pb2, pb3, sa, sb1, sb2, sb3 = [], [], [], [], [], [], [], []
    for l in range(DEPTH):
        xp, st_a, st_b = _prompt_layer(xp, norm1_g[l], w_in[l], sinks[l], w_pa[l], w_pb[l], w_o[l], norm2_g[l], w_gu[l], w_down[l])
        pa.append(st_a)
        pb1.append(st_b[0])
        pb2.append(st_b[1])
        pb3.append(st_b[2])
        xs, st_a, st_b = _sample_layer(xs, cache_a_kv[l], (cache_b1_kv[l], cache_b2_kv[l], cache_b3_kv[l]), norm1_g[l], w_in[l], sinks[l], w_pa[l], w_pb[l], w_o[l], norm2_g[l], w_gu[l], w_down[l])
        sa.append(st_a)
        sb1.append(st_b[0])
        sb2.append(st_b[1])
        sb3.append(st_b[2])
    y_prompt = _rmsnorm(xp, final_norm_g)
    y_sample = _rmsnorm(xs, final_norm_g)
    new_pa = jnp.stack(pa)
    new_pb1 = jnp.stack(pb1)
    new_pb2 = jnp.stack(pb2)
    new_pb3 = jnp.stack(pb3)
    new_sa = jnp.stack(sa)
    new_sb1 = jnp.stack(sb1)
    new_sb2 = jnp.stack(sb2)
    new_sb3 = jnp.stack(sb3)
    return (y_prompt, y_sample, new_pa, new_pb1, new_pb2, new_pb3, new_sa, new_sb1, new_sb2, new_sb3)
```

```python
import functools

import jax
import jax.numpy as jnp
from jax import lax
from jax.experimental import pallas as pl
from jax.experimental.pallas import tpu as pltpu

F32 = jnp.float32
BF16 = jnp.bfloat16

D_MODEL = 1024
HEAD_DIM = 64
HALF = HEAD_DIM // 2
N_HEADS_A = 16
N_KV_A = 2
GROUP_A = N_HEADS_A // N_KV_A
WINDOW_A = 128
DILATED_PAIRS = ((128, 1), (512, 4), (2048, 16))
N_GROUPS_B = 3
HEADS_B = 4
SPAN = 128
PAST_LEN = 16384
ROPE_THETA = 10000.0
NORM_EPS = 1e-6
NEG_INF = -1e30
SCALE = HEAD_DIM ** -0.5

LANES = 128
PAIRS_A = N_HEADS_A // 2
PAIRS_B = HEADS_B // 2
Q_A = N_HEADS_A * HEAD_DIM
KV_A = N_KV_A * HEAD_DIM
GRP_B = HEADS_B * HEAD_DIM
B_W = N_GROUPS_B * GRP_B
OFF_KA = Q_A
OFF_VA = OFF_KA + KV_A
OFF_QB = OFF_VA + KV_A
OFF_KB = OFF_QB + B_W
OFF_VB = OFF_KB + B_W
OFF_GATE = OFF_VB + B_W
QBLK = 128

VMEM_LIMIT = 56 * 1024 * 1024


def _rms(x, g):
    return x * lax.rsqrt(jnp.mean(x * x, axis=-1, keepdims=True) + NORM_EPS) * g


def _rope(z, cos, sina, sinb):
    return z * cos + pltpu.roll(z, LANES - HALF, 1) * sina + pltpu.roll(z, HALF, 1) * sinb


def _sigmoid(z):
    return 1.0 / (1.0 + jnp.exp(-z))


def _dup_halves(z, lo):
    r = pltpu.roll(z, HEAD_DIM, 1)
    return jnp.where(lo, z, r), jnp.where(lo, r, z)


def _rope_tables(pos):
    inv = 1.0 / (ROPE_THETA ** (jnp.arange(HALF, dtype=F32) / HALF))
    ang = pos.astype(F32)[:, None] * inv[None, :]
    ang = jnp.concatenate([ang, ang, ang, ang], axis=-1)
    first = (jnp.arange(LANES) % HEAD_DIM) < HALF
    cos = jnp.cos(ang)
    sin = jnp.sin(ang)
    return cos, jnp.where(first, -sin, 0.0), jnp.where(first, 0.0, sin)


def _proj_prompt_kernel(x_ref, g_ref, w_ref, cos_ref, sina_ref, sinb_ref,
                        qa_ref, kad_ref, vad_ref, b1_ref, b2_ref, b3_ref, gate_ref,
                        pa_ref, pb1_ref, pb2_ref, pb3_ref, scr_ref):
    tm = x_ref.shape[1]
    i = pl.program_id(1)
    last = pl.num_programs(1) - 1
    h = _rms(x_ref[0], g_ref[...]).astype(BF16)
    cos = cos_ref[...]
    sina = sina_ref[...]
    sinb = sinb_ref[...]
    lo = lax.broadcasted_iota(jnp.int32, (tm, LANES), 1) < HEAD_DIM

    def mm(off):
        z = jnp.dot(h, w_ref[:, off:off + 2 * LANES], preferred_element_type=F32)
        return z[:, :LANES], z[:, LANES:]

    def rope(z):
        return _rope(z, cos, sina, sinb)

    for c in range(PAIRS_A // 2):
        z0, z1 = mm(c * 2 * LANES)
        qa_ref[0, 2 * c] = (rope(z0) * SCALE).astype(BF16)
        qa_ref[0, 2 * c + 1] = (rope(z1) * SCALE).astype(BF16)

    zk, zv = mm(OFF_KA)
    ka = rope(zk)
    k0, k1 = _dup_halves(ka, lo)
    v0, v1 = _dup_halves(zv, lo)
    kad_ref[0, 0] = k0.astype(BF16)
    kad_ref[0, 1] = k1.astype(BF16)
    vad_ref[0, 0] = v0.astype(BF16)
    vad_ref[0, 1] = v1.astype(BF16)

    @pl.when(i == last)
    def _():
        pa_ref[0, 0] = ka[tm - WINDOW_A:, :].T
        pa_ref[0, 1] = zv[tm - WINDOW_A:, :].T

    b_refs = (b1_ref, b2_ref, b3_ref)
    for g, (win, dil) in enumerate(DILATED_PAIRS):
        q0, q1 = mm(OFF_QB + g * GRP_B)
        k0, k1 = mm(OFF_KB + g * GRP_B)
        v0, v1 = mm(OFF_VB + g * GRP_B)
        q0 = rope(q0) * SCALE
        q1 = rope(q1) * SCALE
        k0 = rope(k0)
        k1 = rope(k1)
        parts = ((q0, q1), (k0, k1), (v0, v1))
        for part in range(3):
            for pr in range(PAIRS_B):
                val = parts[part][pr]
                if dil == 1:
                    b_refs[g][0, part, pr, 0] = val.astype(BF16)
                else:
                    slot = part * PAIRS_B + pr
                    scr_ref[slot] = val
                    for r in range(dil):
                        b_refs[g][0, part, pr, r] = scr_ref[
                            slot, pl.ds(r, tm // dil, stride=dil), :].astype(BF16)
        kt = jnp.concatenate([k0, k1], axis=-1).T
        vt = jnp.concatenate([v0, v1], axis=-1).T
        if g == 2:
            pb3_ref[0, 0] = kt
            pb3_ref[0, 1] = vt
        else:
            st_ref = pb1_ref if g == 0 else pb2_ref
            rows = st_ref.shape[3]

            @pl.when(i == last)
            def _(kt=kt, vt=vt, st_ref=st_ref, rows=rows):
                st_ref[0, 0] = kt[:, tm - rows:]
                st_ref[0, 1] = vt[:, tm - rows:]

    for c in range(2 * D_MODEL // (2 * LANES)):
        z0, z1 = mm(OFF_GATE + c * 2 * LANES)
        gate_ref[0, :, c * 2 * LANES:c * 2 * LANES + LANES] = _sigmoid(z0)
        gate_ref[0, :, c * 2 * LANES + LANES:(c + 1) * 2 * LANES] = _sigmoid(z1)


def _proj_prompt(x, g1, w_in, tables, tm):
    bsz, seq, _ = x.shape
    nt = seq // tm
    cos, sina, sinb = tables
    const2 = lambda b, i: (0, 0)
    tab_spec = pl.BlockSpec((tm, LANES), lambda b, i: (i, 0))
    out_shape = [
        jax.ShapeDtypeStruct((bsz, PAIRS_A, seq, LANES), BF16),
        jax.ShapeDtypeStruct((bsz, N_KV_A, seq, LANES), BF16),
        jax.ShapeDtypeStruct((bsz, N_KV_A, seq, LANES), BF16),
    ]
    out_specs = [
        pl.BlockSpec((1, PAIRS_A, tm, LANES), lambda b, i: (b, 0, i, 0)),
        pl.BlockSpec((1, N_KV_A, tm, LANES), lambda b, i: (b, 0, i, 0)),
        pl.BlockSpec((1, N_KV_A, tm, LANES), lambda b, i: (b, 0, i, 0)),
    ]
    for _, dil in DILATED_PAIRS:
        out_shape.append(jax.ShapeDtypeStruct((bsz, 3, PAIRS_B, dil, seq // dil, LANES), BF16))
        out_specs.append(pl.BlockSpec((1, 3, PAIRS_B, dil, tm // dil, LANES),
                                      lambda b, i: (b, 0, 0, 0, i, 0)))
    out_shape.append(jax.ShapeDtypeStruct((bsz, seq, 2 * D_MODEL), F32))
    out_specs.append(pl.BlockSpec((1, tm, 2 * D_MODEL), lambda b, i: (b, i, 0)))
    out_shape.append(jax.ShapeDtypeStruct((bsz, 2, KV_A, min(WINDOW_A, seq)), F32))
    out_specs.append(pl.BlockSpec((1, 2, KV_A, min(WINDOW_A, seq)), lambda b, i: (b, 0, 0, 0)))
    for g, (win, _) in enumerate(DILATED_PAIRS):
        rows = min(win, seq)
        out_shape.append(jax.ShapeDtypeStruct((bsz, 2, GRP_B, rows), F32))
        if g == 2:
            out_specs.append(pl.BlockSpec((1, 2, GRP_B, tm), lambda b, i: (b, 0, 0, i)))
        else:
            out_specs.append(pl.BlockSpec((1, 2, GRP_B, rows), lambda b, i: (b, 0, 0, 0)))
    return pl.pallas_call(
        _proj_prompt_kernel,
        grid=(bsz, nt),
        in_specs=[
            pl.BlockSpec((1, tm, D_MODEL), lambda b, i: (b, i, 0)),
            pl.BlockSpec((1, D_MODEL), const2),
            pl.BlockSpec(w_in.shape, const2, pipeline_mode=pl.Buffered(1)),
            tab_spec, tab_spec, tab_spec,
        ],
        out_specs=out_specs,
        out_shape=out_shape,
        scratch_shapes=[pltpu.VMEM((3 * PAIRS_B, tm, LANES), F32)],
        compiler_params=pltpu.CompilerParams(
            dimension_semantics=("arbitrary", "arbitrary"), vmem_limit_bytes=VMEM_LIMIT),
        name="proj_prompt",
    )(x, g1, w_in, cos, sina, sinb)


def _band_biases():
    qi = lax.broadcasted_iota(jnp.int32, (QBLK, 2 * QBLK), 0)
    ki = lax.broadcasted_iota(jnp.int32, (QBLK, 2 * QBLK), 1)
    band = jnp.where((ki >= qi) & (ki <= qi + SPAN), 0.0, NEG_INF).astype(F32)
    qi1 = lax.broadcasted_iota(jnp.int32, (QBLK, QBLK), 0)
    ki1 = lax.broadcasted_iota(jnp.int32, (QBLK, QBLK), 1)
    causal = jnp.where(ki1 <= qi1, 0.0, NEG_INF).astype(F32)
    return causal, band


def _head_scores(q2, k, bias, lo, hh):
    qm = jnp.where(lo if hh == 0 else jnp.logical_not(lo), q2, jnp.zeros_like(q2))
    s = lax.dot_general(qm, k, (((1,), (1,)), ((), ())), preferred_element_type=F32)
    return s + bias


def _attn_a_kernel(sink_ref, qa_ref, kd_ref, vd_ref, oa_ref):
    seq = qa_ref.shape[2]
    causal, band = _band_biases()
    lo = lax.broadcasted_iota(jnp.int32, (QBLK, LANES), 1) < HEAD_DIM

    def block(jb, first):
        q0 = 0 if first else pl.multiple_of(jb * QBLK, QBLK)
        for kv in range(N_KV_A):
            if first:
                k = kd_ref[0, kv, 0:QBLK, :]
                v = vd_ref[0, kv, 0:QBLK, :]
                bias = causal
            else:
                k = kd_ref[0, kv, pl.ds(q0 - QBLK, 2 * QBLK), :]
                v = vd_ref[0, kv, pl.ds(q0 - QBLK, 2 * QBLK), :]
                bias = band
            for pr in range(PAIRS_A // N_KV_A):
                pair = kv * (PAIRS_A // N_KV_A) + pr
                q2 = qa_ref[0, pair, pl.ds(q0, QBLK), :]
                outs = []
                for hh in range(2):
                    s = _head_scores(q2, k, bias, lo, hh)
                    sink = sink_ref[2 * pair + hh]
                    m = jnp.maximum(jnp.max(s, axis=1, keepdims=True), sink)
                    e = jnp.exp(s - m)
                    l = jnp.sum(e, axis=1, keepdims=True) + jnp.exp(sink - m)
                    pv = jnp.dot(e.astype(BF16), v, preferred_element_type=F32)
                    outs.append(pv / l)
                oa_ref[0, pair, pl.ds(q0, QBLK), :] = jnp.where(lo, outs[0], outs[1]).astype(BF16)

    block(0, True)

    def body(jb, carry):
        block(jb, False)
        return carry

    lax.fori_loop(1, seq // QBLK, body, 0)


def _attn_a(sinks, qa, kad, vad):
    bsz, _, seq, _ = qa.shape
    return pl.pallas_call(
        _attn_a_kernel,
        grid=(bsz,),
        in_specs=[
            pl.BlockSpec(memory_space=pltpu.SMEM),
            pl.BlockSpec((1, PAIRS_A, seq, LANES), lambda b: (b, 0, 0, 0)),
            pl.BlockSpec((1, N_KV_A, seq, LANES), lambda b: (b, 0, 0, 0)),
            pl.BlockSpec((1, N_KV_A, seq, LANES), lambda b: (b, 0, 0, 0)),
        ],
        out_specs=pl.BlockSpec((1, PAIRS_A, seq, LANES), lambda b: (b, 0, 0, 0)),
        out_shape=jax.ShapeDtypeStruct((bsz, PAIRS_A, seq, LANES), BF16),
        compiler_params=pltpu.CompilerParams(
            dimension_semantics=("arbitrary",), vmem_limit_bytes=VMEM_LIMIT),
        name="attn_a",
    )(sinks, qa, kad, vad)


def _attn_b_kernel(b1_ref, b2_ref, b3_ref, ob_ref, o_scr, lse_scr):
    seq = ob_ref.shape[1]
    causal, band = _band_biases()
    lo = lax.broadcasted_iota(jnp.int32, (QBLK, LANES), 1) < HEAD_DIM

    for g, (ref, (_, dil)) in enumerate(zip((b1_ref, b2_ref, b3_ref), DILATED_PAIRS)):
        nblk = seq // dil // QBLK

        def unit(r, jb, first, g=g, ref=ref, dil=dil):
            q0 = 0 if first else pl.multiple_of(jb * QBLK, QBLK)
            for pr in range(PAIRS_B):
                q2 = ref[0, 0, pr, r, pl.ds(q0, QBLK), :]
                if first:
                    k = ref[0, 1, pr, r, 0:QBLK, :]
                    v = ref[0, 2, pr, r, 0:QBLK, :]
                    bias = causal
                else:
                    k = ref[0, 1, pr, r, pl.ds(q0 - QBLK, 2 * QBLK), :]
                    v = ref[0, 2, pr, r, pl.ds(q0 - QBLK, 2 * QBLK), :]
                    bias = band
                outs, lses = [], []
                for hh in range(2):
                    s = _head_scores(q2, k, bias, lo, hh)
                    m = jnp.max(s, axis=1, keepdims=True)
                    e = jnp.exp(s - m)
                    l = jnp.sum(e, axis=1, keepdims=True)
                    pv = jnp.dot(e.astype(BF16), v, preferred_element_type=F32)
                    outs.append(pv / l)
                    lses.append(m + jnp.log(l))
                o = jnp.where(lo, outs[0], outs[1])
                lse = jnp.where(lo, lses[0], lses[1])
                row0 = q0 * dil + r
                if dil == 1:
                    idx = pl.ds(row0, QBLK)
                else:
                    idx = pl.ds(row0, QBLK, stride=dil)
                o_scr[g, pr, idx, :] = o
                lse_scr[g, pr, idx, :] = lse

        def per_seq(r, carry, unit=unit, nblk=nblk):
            unit(r, 0, True)
            if nblk > 1:
                def body(jb, c):
                    unit(r, jb, False)
                    return c
                lax.fori_loop(1, nblk, body, 0)
            return carry

        lax.fori_loop(0, dil, per_seq, 0)

    mrows = 2 * QBLK

    def merge(c, carry):
        rows = pl.ds(pl.multiple_of(c * mrows, mrows), mrows)
        for pr in range(PAIRS_B):
            ls = [lse_scr[g, pr, rows, :] for g in range(N_GROUPS_B)]
            mx = jnp.maximum(jnp.maximum(ls[0], ls[1]), ls[2])
            ws = [jnp.exp(l - mx) for l in ls]
            num = ws[0] * o_scr[0, pr, rows, :] + ws[1] * o_scr[1, pr, rows, :] + ws[2] * o_scr[2, pr, rows, :]
            den = ws[0] + ws[1] + ws[2]
            ob_ref[0, rows, pr * LANES:(pr + 1) * LANES] = (num / den).astype(BF16)
        return carry

    lax.fori_loop(0, seq // mrows, merge, 0)


def _attn_b(b1, b2, b3):
    bsz = b1.shape[0]
    seq = b1.shape[4]
    specs = [pl.BlockSpec((1,) + b.shape[1:], lambda b_: (b_, 0, 0, 0, 0, 0)) for b in (b1, b2, b3)]
    return pl.pallas_call(
        _attn_b_kernel,
        grid=(bsz,),
        in_specs=specs,
        out_specs=pl.BlockSpec((1, seq, GRP_B), lambda b_: (b_, 0, 0)),
        out_shape=jax.ShapeDtypeStruct((bsz, seq, GRP_B), BF16),
        scratch_shapes=[pltpu.VMEM((N_GROUPS_B, PAIRS_B, seq, LANES), F32),
                        pltpu.VMEM((N_GROUPS_B, PAIRS_B, seq, LANES), F32)],
        compiler_params=pltpu.CompilerParams(
            dimension_semantics=("arbitrary",), vmem_limit_bytes=VMEM_LIMIT),
        name="attn_b",
    )(b1, b2, b3)


def _finish_kernel(x_ref, oa_ref, ob_ref, gate_ref, wpa_ref, wpb_ref, wo_ref, g2_ref,
                   wgu_ref, wdn_ref, gf_ref, y_ref):
    d_ff = wdn_ref.shape[0]
    oa = jnp.concatenate([oa_ref[0, p] for p in range(PAIRS_A)], axis=-1)
    ma = jnp.dot(oa, wpa_ref[...], preferred_element_type=F32)
    mb = jnp.dot(ob_ref[0], wpb_ref[...], preferred_element_type=F32)
    m = gate_ref[0, :, :D_MODEL] * ma + gate_ref[0, :, D_MODEL:] * mb
    x1 = x_ref[0] + jnp.dot(m.astype(BF16), wo_ref[...], preferred_element_type=F32)
    xn = _rms(x1, g2_ref[...]).astype(BF16)
    ck = 2 * LANES
    acc = jnp.zeros_like(x1)
    for c in range(d_ff // ck):
        gg = jnp.dot(xn, wgu_ref[:, c * ck:(c + 1) * ck], preferred_element_type=F32)
        uu = jnp.dot(xn, wgu_ref[:, d_ff + c * ck:d_ff + (c + 1) * ck], preferred_element_type=F32)
        act = (gg * _sigmoid(gg) * uu).astype(BF16)
        acc = acc + jnp.dot(act, wdn_ref[c * ck:(c + 1) * ck, :], preferred_element_type=F32)
    y_ref[0] = _rms(x1 + acc, gf_ref[...])


def _finish(x, oa, ob, gate, w_pa, w_pb, w_o, g2, w_gu, w_down, gf, tm):
    bsz, seq, _ = x.shape
    const2 = lambda b, i: (0, 0)
    wspec = lambda w: pl.BlockSpec(w.shape, const2, pipeline_mode=pl.Buffered(1))
    return pl.pallas_call(
        _finish_kernel,
        grid=(bsz, seq // tm),
        in_specs=[
            pl.BlockSpec((1, tm, D_MODEL), lambda b, i: (b, i, 0)),
            pl.BlockSpec((1, PAIRS_A, tm, LANES), lambda b, i: (b, 0, i, 0)),
            pl.BlockSpec((1, tm, GRP_B), lambda b, i: (b, i, 0)),
            pl.BlockSpec((1, tm, 2 * D_MODEL), lambda b, i: (b, i, 0)),
            wspec(w_pa), wspec(w_pb), wspec(w_o),
            pl.BlockSpec((1, D_MODEL), const2),
            wspec(w_gu), wspec(w_down),
            pl.BlockSpec((1, D_MODEL), const2),
        ],
        out_specs=pl.BlockSpec((1, tm, D_MODEL), lambda b, i: (b, i, 0)),
        out_shape=jax.ShapeDtypeStruct((bsz, seq, D_MODEL), F32),
        compiler_params=pltpu.CompilerParams(
            dimension_semantics=("arbitrary", "arbitrary"), vmem_limit_bytes=VMEM_LIMIT),
        name="finish",
    )(x, oa, ob, gate, w_pa, w_pb, w_o, g2, w_gu, w_down, gf)


ZS_QA = 0
ZS_QB = ZS_QA + Q_A
ZS_KA = ZS_QB + B_W
ZS_VA = ZS_KA + N_KV_A * GROUP_A * HEAD_DIM
ZS_KB = ZS_VA + N_KV_A * GROUP_A * HEAD_DIM
ZS_VB = ZS_KB + B_W
ZS_W = ZS_VB + B_W
KT_KA = 0
KT_VA = KT_KA + KV_A
KT_KB = KT_VA + KV_A
KT_VB = KT_KB + B_W
KT_W = KT_VB + B_W


def _proj_sample_kernel(x_ref, g_ref, w_ref, cos_ref, sina_ref, sinb_ref, zs_ref, kvt_ref, gate_ref):
    nb = x_ref.shape[0]
    h = _rms(x_ref[...], g_ref[...]).astype(BF16)
    cos = cos_ref[...]
    sina = sina_ref[...]
    sinb = sinb_ref[...]
    lo = lax.broadcasted_iota(jnp.int32, (nb, LANES), 1) < HEAD_DIM

    def mm(off):
        z = jnp.dot(h, w_ref[:, off:off + 2 * LANES], preferred_element_type=F32)
        return z[:, :LANES], z[:, LANES:]

    def rope(z):
        return _rope(z, cos, sina, sinb)

    def put(off, val):
        zs_ref[:, off:off + LANES] = val

    for c in range(PAIRS_A // 2):
        z0, z1 = mm(c * 2 * LANES)
        put(ZS_QA + c * 2 * LANES, rope(z0) * SCALE)
        put(ZS_QA + c * 2 * LANES + LANES, rope(z1) * SCALE)
    for c in range(B_W // (2 * LANES)):
        z0, z1 = mm(OFF_QB + c * 2 * LANES)
        put(ZS_QB + c * 2 * LANES, rope(z0) * SCALE)
        put(ZS_QB + c * 2 * LANES + LANES, rope(z1) * SCALE)

    zk, zv = mm(OFF_KA)
    ka = rope(zk)
    kvt_ref[KT_KA:KT_KA + KV_A, :] = ka.T
    kvt_ref[KT_VA:KT_VA + KV_A, :] = zv.T
    reps = GROUP_A * HEAD_DIM // LANES
    for base, val in ((ZS_KA, ka), (ZS_VA, zv)):
        d0, d1 = _dup_halves(val, lo)
        for kv, dup in enumerate((d0, d1)):
            for c in range(reps):
                put(base + (kv * reps + c) * LANES, dup)

    for c in range(B_W // (2 * LANES)):
        k0, k1 = mm(OFF_KB + c * 2 * LANES)
        v0, v1 = mm(OFF_VB + c * 2 * LANES)
        k0 = rope(k0)
        k1 = rope(k1)
        for j, (kk, vv) in enumerate(((k0, v0), (k1, v1))):
            off = (2 * c + j) * LANES
            put(ZS_KB + off, kk)
            put(ZS_VB + off, vv)
            kvt_ref[KT_KB + off:KT_KB + off + LANES, :] = kk.T
            kvt_ref[KT_VB + off:KT_VB + off + LANES, :] = vv.T

    for c in range(2 * D_MODEL // (2 * LANES)):
        z0, z1 = mm(OFF_GATE + c * 2 * LANES)
        gate_ref[:, c * 2 * LANES:c * 2 * LANES + LANES] = _sigmoid(z0)
        gate_ref[:, c * 2 * LANES + LANES:(c + 1) * 2 * LANES] = _sigmoid(z1)


def _proj_sample(x, g1, w_in, tables):
    nb = x.shape[0]
    cos, sina, sinb = tables
    return pl.pallas_call(
        _proj_sample_kernel,
        out_shape=[
            jax.ShapeDtypeStruct((nb, ZS_W), F32),
            jax.ShapeDtypeStruct((KT_W, nb), F32),
            jax.ShapeDtypeStruct((nb, 2 * D_MODEL), F32),
        ],
        compiler_params=pltpu.CompilerParams(vmem_limit_bytes=VMEM_LIMIT),
        name="proj_sample",
    )(x, g1, w_in, cos, sina, sinb)


def _sample_attn_kernel(zs_ref, kvt_ref, sink_ref, ca_ref, c1_ref, c2_ref, c3_ref,
                        sa_ref, s1_ref, s2_ref, s3_ref, oa_ref, ob_ref):
    b = pl.program_id(0)
    nb = kvt_ref.shape[1]
    pick = lax.broadcasted_iota(jnp.int32, (1, nb), 1) == b

    def row(off, width):
        return zs_ref[pl.ds(b, 1), off:off + width]

    def new_col(off, width):
        blk = kvt_ref[off:off + width, :]
        return jnp.sum(jnp.where(pick, blk, 0.0), axis=1, keepdims=True)

    def shifted(tile, col):
        w = tile.shape[1]
        lane = lax.broadcasted_iota(jnp.int32, tile.shape, 1)
        return jnp.where(lane == w - 1, col, pltpu.roll(tile, w - 1, 1))

    def head_mask(nh):
        rows = max(nh, 8)
        hid = lax.broadcasted_iota(jnp.int32, (rows, nh * HEAD_DIM), 0)
        lh = lax.broadcasted_iota(jnp.int32, (rows, nh * HEAD_DIM), 1) // HEAD_DIM
        return hid == lh

    def attend(q_row, kt, vt, knew_row, vnew_row, hm, dil, sink_col):
        qm = jnp.where(hm, q_row, 0.0)
        s = jnp.dot(qm.astype(BF16), kt.astype(BF16), preferred_element_type=F32)
        if dil > 1:
            lane = lax.broadcasted_iota(jnp.int32, s.shape, 1)
            s = jnp.where(lane % dil == 0, s, NEG_INF)
        s_new = jnp.sum(qm * knew_row, axis=1, keepdims=True)
        m = jnp.maximum(jnp.max(s, axis=1, keepdims=True), s_new)
        if sink_col is not None:
            m = jnp.maximum(m, sink_col)
        e = jnp.exp(s - m)
        e_new = jnp.exp(s_new - m)
        l = jnp.sum(e, axis=1, keepdims=True) + e_new
        if sink_col is not None:
            l = l + jnp.exp(sink_col - m)
        num = lax.dot_general(e.astype(BF16), vt.astype(BF16), (((1,), (1,)), ((), ())),
                              preferred_element_type=F32)
        return num + e_new * vnew_row, m, l

    hm_a = head_mask(GROUP_A)
    wq = GROUP_A * HEAD_DIM
    for kv in range(N_KV_A):
        kt = ca_ref[0, 0, kv * HEAD_DIM:(kv + 1) * HEAD_DIM, :]
        vt = ca_ref[0, 1, kv * HEAD_DIM:(kv + 1) * HEAD_DIM, :]
        num, m, l = attend(row(ZS_QA + kv * wq, wq),
                           jnp.concatenate([kt] * GROUP_A, axis=0),
                           jnp.concatenate([vt] * GROUP_A, axis=0),
                           row(ZS_KA + kv * wq, wq), row(ZS_VA + kv * wq, wq),
                           hm_a, 1, sink_ref[kv * GROUP_A:(kv + 1) * GROUP_A, :])
        o = jnp.sum(jnp.where(hm_a, num / l, 0.0), axis=0, keepdims=True)
        oa_ref[pl.ds(b, 1), kv * wq:(kv + 1) * wq] = o
    sa_ref[0, 0] = shifted(ca_ref[0, 0], new_col(KT_KA, KV_A))
    sa_ref[0, 1] = shifted(ca_ref[0, 1], new_col(KT_VA, KV_A))

    hm_b = head_mask(HEADS_B)
    nums, ms, ls = [], [], []
    for g, (c_ref, s_ref, (_, dil)) in enumerate(zip((c1_ref, c2_ref, c3_ref), (s1_ref, s2_ref, s3_ref),
                                                      DILATED_PAIRS)):
        kt = c_ref[0, 0]
        vt = c_ref[0, 1]
        num, m, l = attend(row(ZS_QB + g * GRP_B, GRP_B), kt, vt,
                           row(ZS_KB + g * GRP_B, GRP_B), row(ZS_VB + g * GRP_B, GRP_B),
                           hm_b, dil, None)
        nums.append(num)
        ms.append(m)
        ls.append(l)
        s_ref[0, 0] = shifted(kt, new_col(KT_KB + g * GRP_B, GRP_B))
        s_ref[0, 1] = shifted(vt, new_col(KT_VB + g * GRP_B, GRP_B))
    mx = jnp.maximum(jnp.maximum(ms[0], ms[1]), ms[2])
    ws = [jnp.exp(m - mx) for m in ms]
    num = ws[0] * nums[0] + ws[1] * nums[1] + ws[2] * nums[2]
    den = ws[0] * ls[0] + ws[1] * ls[1] + ws[2] * ls[2]
    ob_ref[pl.ds(b, 1), :] = jnp.sum(jnp.where(hm_b, num / den, 0.0), axis=0, keepdims=True)


def _sample_attn(zs, kvt, sinks_col, caches):
    nb = zs.shape[0]
    full2 = lambda a: pl.BlockSpec(a.shape, lambda b: (0, 0))
    cspec = lambda a: pl.BlockSpec((1,) + a.shape[1:], lambda b: (b, 0, 0, 0))
    return pl.pallas_call(
        _sample_attn_kernel,
        grid=(nb,),
        in_specs=[full2(zs), full2(kvt), full2(sinks_col)] + [cspec(c) for c in caches],
        out_specs=[cspec(c) for c in caches] + [
            pl.BlockSpec((nb, Q_A), lambda b: (0, 0)),
            pl.BlockSpec((nb, GRP_B), lambda b: (0, 0)),
        ],
        out_shape=[jax.ShapeDtypeStruct(c.shape, F32) for c in caches] + [
            jax.ShapeDtypeStruct((nb, Q_A), F32),
            jax.ShapeDtypeStruct((nb, GRP_B), F32),
        ],
        compiler_params=pltpu.CompilerParams(
            dimension_semantics=("arbitrary",), vmem_limit_bytes=VMEM_LIMIT),
        name="sample_attn",
    )(zs, kvt, sinks_col, *caches)


def _rows_minor(c):
    _, bsz, rows, two, nh, hd = c.shape
    return jnp.transpose(c, (0, 1, 3, 4, 5, 2)).reshape(bsz, two, nh * hd, rows)


def _rows_major(c, nh):
    bsz, two, _, rows = c.shape
    return jnp.transpose(c.reshape(1, bsz, two, nh, HEAD_DIM, rows), (0, 1, 5, 2, 3, 4))


def kernel(x_prompt, x_sample, cache_a_kv, cache_b1_kv, cache_b2_kv, cache_b3_kv, norm1_g, w_in, sinks,
           w_pa, w_pb, w_o, norm2_g, w_gu, w_down, final_norm_g):
    bsz, seq, _ = x_prompt.shape
    nb, dec_seq, _ = x_sample.shape
    assert dec_seq == 1 and norm1_g.shape[0] == 1
    assert cache_a_kv.shape[2] == WINDOW_A
    assert tuple(c.shape[2] for c in (cache_b1_kv, cache_b2_kv, cache_b3_kv)) == tuple(
        w for w, _ in DILATED_PAIRS)

    w_in_b = w_in[0].astype(BF16)
    w_pa_b = w_pa[0].astype(BF16)
    w_pb_b = w_pb[0].astype(BF16)
    w_o_b = w_o[0].astype(BF16)
    w_gu_b = w_gu[0].astype(BF16)
    w_dn_b = w_down[0].astype(BF16)
    gf = final_norm_g.reshape(1, D_MODEL)

    tables_p = _rope_tables(jnp.arange(seq, dtype=jnp.int32))
    (qa, kad, vad, b1, b2, b3, gate, pa, pb1, pb2, pb3) = _proj_prompt(
        x_prompt, norm1_g, w_in_b, tables_p, 512)
    oa = _attn_a(sinks[0], qa, kad, vad)
    ob = _attn_b(b1, b2, b3)
    y_prompt = _finish(x_prompt, oa, ob, gate, w_pa_b, w_pb_b, w_o_b, norm2_g, w_gu_b, w_dn_b, gf, 256)

    xs = x_sample.reshape(nb, D_MODEL)
    tables_s = _rope_tables(jnp.full((nb,), PAST_LEN, dtype=jnp.int32))
    zs, kvt, gate_s = _proj_sample(xs, norm1_g, w_in_b, tables_s)
    caches = [_rows_minor(c) for c in (cache_a_kv, cache_b1_kv, cache_b2_kv, cache_b3_kv)]
    sa, sb1, sb2, sb3, oa_s, ob_s = _sample_attn(zs, kvt, sinks[0].reshape(N_HEADS_A, 1), caches)
    oa_s = jnp.transpose(oa_s.reshape(nb, PAIRS_A, LANES), (1, 0, 2))[None].astype(BF16)
    y_sample = _finish(xs[None], oa_s, ob_s[None].astype(BF16), gate_s[None], w_pa_b, w_pb_b, w_o_b,
                       norm2_g, w_gu_b, w_dn_b, gf, nb)

    return (y_prompt, y_sample.reshape(nb, 1, D_MODEL),
            _rows_major(pa, N_KV_A), _rows_major(pb1, HEADS_B), _rows_major(pb2, HEADS_B),
            _rows_major(pb3, HEADS_B),
            _rows_major(sa, N_KV_A), _rows_major(sb1, HEADS_B), _rows_major(sb2, HEADS_B),
            _rows_major(sb3, HEADS_B))
```

```python
import functools

import jax
import jax.numpy as jnp
from jax import lax
from jax.experimental import pallas as pl
from jax.experimental.pallas import tpu as pltpu

F32 = jnp.float32
BF16 = jnp.bfloat16

D_MODEL = 1024
HEAD_DIM = 64
HALF = HEAD_DIM // 2
N_HEADS_A = 16
N_KV_A = 2
GROUP_A = N_HEADS_A // N_KV_A
WINDOW_A = 128
DILATED_PAIRS = ((128, 1), (512, 4), (2048, 16))
N_GROUPS_B = 3
HEADS_B = 4
SPAN = 128
PAST_LEN = 16384
ROPE_THETA = 10000.0
NORM_EPS = 1e-6
NEG_INF = -1e30
SCALE = HEAD_DIM ** -0.5

LANES = 128
PAIRS_A = N_HEADS_A // 2
PAIRS_B = HEADS_B // 2
Q_A = N_HEADS_A * HEAD_DIM
KV_A = N_KV_A * HEAD_DIM
GRP_B = HEADS_B * HEAD_DIM
B_W = N_GROUPS_B * GRP_B
OFF_KA = Q_A
OFF_VA = OFF_KA + KV_A
OFF_QB = OFF_VA + KV_A
OFF_KB = OFF_QB + B_W
OFF_VB = OFF_KB + B_W
OFF_GATE = OFF_VB + B_W
QBLK = 128
B_PARTS = 4

VMEM_LIMIT = 56 * 1024 * 1024


def _rms(x, g):
    return x * lax.rsqrt(jnp.mean(x * x, axis=-1, keepdims=True) + NORM_EPS) * g


def _rope(z, cos, sina, sinb):
    return z * cos + pltpu.roll(z, LANES - HALF, 1) * sina + pltpu.roll(z, HALF, 1) * sinb


def _sigmoid(z):
    return 1.0 / (1.0 + jnp.exp(-z))


def _dup_halves(z, lo):
    r = pltpu.roll(z, HEAD_DIM, 1)
    return jnp.where(lo, z, r), jnp.where(lo, r, z)


def _rope_tables(pos):
    inv = 1.0 / (ROPE_THETA ** (jnp.arange(HALF, dtype=F32) / HALF))
    ang = pos.astype(F32)[:, None] * inv[None, :]
    ang = jnp.concatenate([ang, ang, ang, ang], axis=-1)
    first = (jnp.arange(LANES) % HEAD_DIM) < HALF
    cos = jnp.cos(ang)
    sin = jnp.sin(ang)
    return cos, jnp.where(first, -sin, 0.0), jnp.where(first, 0.0, sin)


def _proj_prompt_kernel(x_ref, g_ref, w_ref, cos_ref, sina_ref, sinb_ref,
                        qa_ref, kad_ref, vad_ref, b1_ref, b2_ref, b3_ref, gate_ref,
                        pa_ref, pb1_ref, pb2_ref, pb3_ref, scr_ref):
    tm = x_ref.shape[1]
    i = pl.program_id(1)
    last = pl.num_programs(1) - 1
    h = _rms(x_ref[0], g_ref[...]).astype(BF16)
    cos = cos_ref[...]
    sina = sina_ref[...]
    sinb = sinb_ref[...]
    lo = lax.broadcasted_iota(jnp.int32, (tm, LANES), 1) < HEAD_DIM

    def mm(off):
        z = jnp.dot(h, w_ref[:, off:off + 2 * LANES], preferred_element_type=F32)
        return z[:, :LANES], z[:, LANES:]

    def rope(z):
        return _rope(z, cos, sina, sinb)

    for c in range(PAIRS_A // 2):
        z0, z1 = mm(c * 2 * LANES)
        qa_ref[0, 2 * c] = (rope(z0) * SCALE).astype(BF16)
        qa_ref[0, 2 * c + 1] = (rope(z1) * SCALE).astype(BF16)

    zk, zv = mm(OFF_KA)
    ka = rope(zk)
    k0, k1 = _dup_halves(ka, lo)
    v0, v1 = _dup_halves(zv, lo)
    kad_ref[0, 0] = k0.astype(BF16)
    kad_ref[0, 1] = k1.astype(BF16)
    for kv, vd in enumerate((v0, v1)):
        vad_ref[0, kv, 0] = jnp.where(lo, vd, 1.0).astype(BF16)
        vad_ref[0, kv, 1] = jnp.where(lo, 1.0, vd).astype(BF16)

    @pl.when(i == last)
    def _():
        pa_ref[0, 0] = ka[tm - WINDOW_A:, :].T
        pa_ref[0, 1] = zv[tm - WINDOW_A:, :].T

    b_refs = (b1_ref, b2_ref, b3_ref)
    for g, (win, dil) in enumerate(DILATED_PAIRS):
        q0, q1 = mm(OFF_QB + g * GRP_B)
        k0, k1 = mm(OFF_KB + g * GRP_B)
        v0, v1 = mm(OFF_VB + g * GRP_B)
        q0 = rope(q0) * SCALE
        q1 = rope(q1) * SCALE
        k0 = rope(k0)
        k1 = rope(k1)
        parts = ((q0, q1), (k0, k1),
                 (jnp.where(lo, v0, 1.0), jnp.where(lo, v1, 1.0)),
                 (jnp.where(lo, 1.0, v0), jnp.where(lo, 1.0, v1)))
        for part in range(B_PARTS):
            for pr in range(PAIRS_B):
                val = parts[part][pr]
                if dil == 1:
                    b_refs[g][0, part, pr, 0] = val.astype(BF16)
                else:
                    slot = part * PAIRS_B + pr
                    scr_ref[slot] = val
                    for r in range(dil):
                        b_refs[g][0, part, pr, r] = scr_ref[
                            slot, pl.ds(r, tm // dil, stride=dil), :].astype(BF16)
        kt = jnp.concatenate([k0, k1], axis=-1).T
        vt = jnp.concatenate([v0, v1], axis=-1).T
        if g == 2:
            pb3_ref[0, 0] = kt
            pb3_ref[0, 1] = vt
        else:
            st_ref = pb1_ref if g == 0 else pb2_ref
            rows = st_ref.shape[3]

            @pl.when(i == last)
            def _(kt=kt, vt=vt, st_ref=st_ref, rows=rows):
                st_ref[0, 0] = kt[:, tm - rows:]
                st_ref[0, 1] = vt[:, tm - rows:]

    for c in range(2 * D_MODEL // (2 * LANES)):
        z0, z1 = mm(OFF_GATE + c * 2 * LANES)
        gate_ref[0, :, c * 2 * LANES:c * 2 * LANES + LANES] = _sigmoid(z0)
        gate_ref[0, :, c * 2 * LANES + LANES:(c + 1) * 2 * LANES] = _sigmoid(z1)


def _proj_prompt(x, g1, w_in, tables, tm):
    bsz, seq, _ = x.shape
    nt = seq // tm
    cos, sina, sinb = tables
    const2 = lambda b, i: (0, 0)
    tab_spec = pl.BlockSpec((tm, LANES), lambda b, i: (i, 0))
    out_shape = [
        jax.ShapeDtypeStruct((bsz, PAIRS_A, seq, LANES), BF16),
        jax.ShapeDtypeStruct((bsz, N_KV_A, seq, LANES), BF16),
        jax.ShapeDtypeStruct((bsz, N_KV_A, 2, seq, LANES), BF16),
    ]
    out_specs = [
        pl.BlockSpec((1, PAIRS_A, tm, LANES), lambda b, i: (b, 0, i, 0)),
        pl.BlockSpec((1, N_KV_A, tm, LANES), lambda b, i: (b, 0, i, 0)),
        pl.BlockSpec((1, N_KV_A, 2, tm, LANES), lambda b, i: (b, 0, 0, i, 0)),
    ]
    for _, dil in DILATED_PAIRS:
        out_shape.append(jax.ShapeDtypeStruct((bsz, B_PARTS, PAIRS_B, dil, seq // dil, LANES), BF16))
        out_specs.append(pl.BlockSpec((1, B_PARTS, PAIRS_B, dil, tm // dil, LANES),
                                      lambda b, i: (b, 0, 0, 0, i, 0)))
    out_shape.append(jax.ShapeDtypeStruct((bsz, seq, 2 * D_MODEL), F32))
    out_specs.append(pl.BlockSpec((1, tm, 2 * D_MODEL), lambda b, i: (b, i, 0)))
    out_shape.append(jax.ShapeDtypeStruct((bsz, 2, KV_A, min(WINDOW_A, seq)), F32))
    out_specs.append(pl.BlockSpec((1, 2, KV_A, min(WINDOW_A, seq)), lambda b, i: (b, 0, 0, 0)))
    for g, (win, _) in enumerate(DILATED_PAIRS):
        rows = min(win, seq)
        out_shape.append(jax.ShapeDtypeStruct((bsz, 2, GRP_B, rows), F32))
        if g == 2:
            out_specs.append(pl.BlockSpec((1, 2, GRP_B, tm), lambda b, i: (b, 0, 0, i)))
        else:
            out_specs.append(pl.BlockSpec((1, 2, GRP_B, rows), lambda b, i: (b, 0, 0, 0)))
    return pl.pallas_call(
        _proj_prompt_kernel,
        grid=(bsz, nt),
        in_specs=[
            pl.BlockSpec((1, tm, D_MODEL), lambda b, i: (b, i, 0)),
            pl.BlockSpec((1, D_MODEL), const2),
            pl.BlockSpec(w_in.shape, const2, pipeline_mode=pl.Buffered(1)),
            tab_spec, tab_spec, tab_spec,
        ],
        out_specs=out_specs,
        out_shape=out_shape,
        scratch_shapes=[pltpu.VMEM((B_PARTS * PAIRS_B, tm, LANES), F32)],
        compiler_params=pltpu.CompilerParams(
            dimension_semantics=("arbitrary", "arbitrary"), vmem_limit_bytes=VMEM_LIMIT),
        name="proj_prompt",
    )(x, g1, w_in, cos, sina, sinb)


def _band_biases():
    qi = lax.broadcasted_iota(jnp.int32, (QBLK, 2 * QBLK), 0)
    ki = lax.broadcasted_iota(jnp.int32, (QBLK, 2 * QBLK), 1)
    band = jnp.where((ki >= qi) & (ki <= qi + SPAN), 0.0, NEG_INF).astype(F32)
    first = jnp.where(ki <= qi, 0.0, NEG_INF).astype(F32)
    return first, band, first[:, :QBLK]


def _head_scores(q2, k, bias, lo, hh):
    qm = jnp.where(lo if hh == 0 else jnp.logical_not(lo), q2, jnp.zeros_like(q2))
    s = lax.dot_general(qm, k, (((1,), (1,)), ((), ())), preferred_element_type=F32)
    return s + bias


def _pair_pv(e0, e1, v_lo, v_hi, lo):
    pv0 = jnp.dot(e0, v_lo, preferred_element_type=F32)
    pv1 = jnp.dot(e1, v_hi, preferred_element_type=F32)
    num = jnp.where(lo, pv0, pv1)
    den = pltpu.roll(jnp.where(lo, pv1, pv0), HEAD_DIM, 1)
    return num, den


def _blk(j):
    return j * QBLK if isinstance(j, int) else pl.multiple_of(j * QBLK, QBLK)


def _win(j):
    return max(j - 1, 0) * QBLK if isinstance(j, int) else pl.multiple_of(jnp.maximum(j - 1, 0) * QBLK, QBLK)


def _attn_a_kernel(sink_ref, qa_ref, kd_ref, va_ref, oa_ref, s_scr, e_scr, t_scr):
    seq = qa_ref.shape[2]
    nblk = seq // QBLK
    first, band, _ = _band_biases()
    lo = lax.broadcasted_iota(jnp.int32, (QBLK, LANES), 1) < HEAD_DIM
    pairs_per_kv = PAIRS_A // N_KV_A

    def stage_qk(jb, slot, bias):
        for kv in range(N_KV_A):
            k = kd_ref[0, kv, pl.ds(_win(jb), 2 * QBLK), :]
            for pr in range(pairs_per_kv):
                pair = kv * pairs_per_kv + pr
                q2 = qa_ref[0, pair, pl.ds(_blk(jb), QBLK), :]
                for hh in range(2):
                    s_scr[slot, 2 * pair + hh] = _head_scores(q2, k, bias, lo, hh)

    def stage_sm(slot):
        for pair in range(PAIRS_A):
            ts = []
            for hh in range(2):
                h = 2 * pair + hh
                s = s_scr[slot, h]
                sink = sink_ref[h]
                m = jnp.maximum(jnp.max(s, axis=1, keepdims=True), sink)
                e_scr[slot, h] = jnp.exp(s - m).astype(BF16)
                ts.append(jnp.exp(sink - m))
            t_scr[slot, pair] = jnp.where(lo, ts[0], ts[1])

    def stage_pv(jb, slot):
        for kv in range(N_KV_A):
            v_lo = va_ref[0, kv, 0, pl.ds(_win(jb), 2 * QBLK), :]
            v_hi = va_ref[0, kv, 1, pl.ds(_win(jb), 2 * QBLK), :]
            for pr in range(pairs_per_kv):
                pair = kv * pairs_per_kv + pr
                num, den = _pair_pv(e_scr[slot, 2 * pair], e_scr[slot, 2 * pair + 1], v_lo, v_hi, lo)
                den = den + t_scr[slot, pair]
                oa_ref[0, pair, pl.ds(_blk(jb), QBLK), :] = (num / den).astype(BF16)

    stage_qk(0, 0, first)
    stage_qk(1, 1, band)
    stage_sm(0)

    def body(i, carry):
        t = 2 + 2 * i
        stage_qk(t, 0, band)
        stage_sm(1)
        stage_pv(t - 2, 0)
        stage_qk(t + 1, 1, band)
        stage_sm(0)
        stage_pv(t - 1, 1)
        return carry

    lax.fori_loop(0, (nblk - 2) // 2, body, 0)
    stage_sm(1)
    stage_pv(nblk - 2, 0)
    stage_pv(nblk - 1, 1)


def _attn_a(sinks, qa, kad, vad):
    bsz, _, seq, _ = qa.shape
    assert (seq // QBLK) % 2 == 0 and seq >= 2 * QBLK
    return pl.pallas_call(
        _attn_a_kernel,
        grid=(bsz,),
        in_specs=[
            pl.BlockSpec(memory_space=pltpu.SMEM),
            pl.BlockSpec((1, PAIRS_A, seq, LANES), lambda b: (b, 0, 0, 0)),
            pl.BlockSpec((1, N_KV_A, seq, LANES), lambda b: (b, 0, 0, 0)),
            pl.BlockSpec((1, N_KV_A, 2, seq, LANES), lambda b: (b, 0, 0, 0, 0)),
        ],
        out_specs=pl.BlockSpec((1, PAIRS_A, seq, LANES), lambda b: (b, 0, 0, 0)),
        out_shape=jax.ShapeDtypeStruct((bsz, PAIRS_A, seq, LANES), BF16),
        scratch_shapes=[pltpu.VMEM((2, N_HEADS_A, QBLK, 2 * QBLK), F32),
                        pltpu.VMEM((2, N_HEADS_A, QBLK, 2 * QBLK), BF16),
                        pltpu.VMEM((2, PAIRS_A, QBLK, LANES), F32)],
        compiler_params=pltpu.CompilerParams(
            dimension_semantics=("arbitrary",), vmem_limit_bytes=VMEM_LIMIT),
        name="attn_a",
    )(sinks, qa, kad, vad)


def _attn_b_kernel(b1_ref, b2_ref, b3_ref, ob_ref, o_scr, lse_scr, s12_scr, s3_scr, e12_scr, e3_scr, m_scr):
    seq = ob_ref.shape[1]
    nunit = seq // QBLK
    first, band, causal = _band_biases()
    lo = lax.broadcasted_iota(jnp.int32, (QBLK, LANES), 1) < HEAD_DIM
    refs = (b1_ref, b2_ref, b3_ref)
    dils = tuple(d for _, d in DILATED_PAIRS)
    nblks = tuple(nunit // d for d in dils)

    def coords(g, u):
        if nblks[g] == 1:
            return u, 0
        if nblks[g] == nunit:
            return 0, u
        return u // nblks[g], u % nblks[g]

    def stage_qk(u, slot):
        for g in range(N_GROUPS_B):
            r, jb = coords(g, u)
            for pr in range(PAIRS_B):
                q2 = refs[g][0, 0, pr, r, pl.ds(_blk(jb), QBLK), :]
                if nblks[g] == 1:
                    k = refs[g][0, 1, pr, r, :, :]
                    bias = causal
                else:
                    k = refs[g][0, 1, pr, r, pl.ds(_win(jb), 2 * QBLK), :]
                    if isinstance(jb, int):
                        bias = first if jb == 0 else band
                    elif nblks[g] == nunit:
                        bias = band
                    else:
                        bias = jnp.where(jb == 0, first, band)
                for hh in range(2):
                    s = _head_scores(q2, k, bias, lo, hh)
                    if nblks[g] == 1:
                        s3_scr[slot, 2 * pr + hh] = s
                    else:
                        s12_scr[slot, g, 2 * pr + hh] = s

    def stage_sm(slot):
        for g in range(N_GROUPS_B):
            for pr in range(PAIRS_B):
                ms = []
                for hh in range(2):
                    h = 2 * pr + hh
                    s = s3_scr[slot, h] if nblks[g] == 1 else s12_scr[slot, g, h]
                    m = jnp.max(s, axis=1, keepdims=True)
                    e = jnp.exp(s - m).astype(BF16)
                    if nblks[g] == 1:
                        e3_scr[slot, h] = e
                    else:
                        e12_scr[slot, g, h] = e
                    ms.append(m)
                m_scr[slot, g, pr] = jnp.where(lo, ms[0], ms[1])

    def stage_pv(u, slot):
        for g in range(N_GROUPS_B):
            r, jb = coords(g, u)
            for pr in range(PAIRS_B):
                if nblks[g] == 1:
                    v_lo = refs[g][0, 2, pr, r, :, :]
                    v_hi = refs[g][0, 3, pr, r, :, :]
                    e0, e1 = e3_scr[slot, 2 * pr], e3_scr[slot, 2 * pr + 1]
                else:
                    v_lo = refs[g][0, 2, pr, r, pl.ds(_win(jb), 2 * QBLK), :]
                    v_hi = refs[g][0, 3, pr, r, pl.ds(_win(jb), 2 * QBLK), :]
                    e0, e1 = e12_scr[slot, g, 2 * pr], e12_scr[slot, g, 2 * pr + 1]
                num, den = _pair_pv(e0, e1, v_lo, v_hi, lo)
                row0 = _blk(jb) * dils[g] + r
                idx = pl.ds(row0, QBLK) if dils[g] == 1 else pl.ds(row0, QBLK, stride=dils[g])
                o_scr[g, pr, idx, :] = num / den
                lse_scr[g, pr, idx, :] = m_scr[slot, g, pr] + jnp.log(den)

    stage_qk(0, 0)
    stage_qk(1, 1)
    stage_sm(0)

    def body(i, carry):
        u = 2 + 2 * i
        stage_qk(u, 0)
        stage_sm(1)
        stage_pv(u - 2, 0)
        stage_qk(u + 1, 1)
        stage_sm(0)
        stage_pv(u - 1, 1)
        return carry

    lax.fori_loop(0, (nunit - 2) // 2, body, 0)
    stage_sm(1)
    stage_pv(nunit - 2, 0)
    stage_pv(nunit - 1, 1)

    mrows = 2 * QBLK

    def merge(c, carry):
        rows = pl.ds(pl.multiple_of(c * mrows, mrows), mrows)
        for pr in range(PAIRS_B):
            ls = [lse_scr[g, pr, rows, :] for g in range(N_GROUPS_B)]
            mx = jnp.maximum(jnp.maximum(ls[0], ls[1]), ls[2])
            ws = [jnp.exp(l - mx) for l in ls]
            num = ws[0] * o_scr[0, pr, rows, :] + ws[1] * o_scr[1, pr, rows, :] + ws[2] * o_scr[2, pr, rows, :]
            den = ws[0] + ws[1] + ws[2]
            ob_ref[0, rows, pr * LANES:(pr + 1) * LANES] = (num / den).astype(BF16)
        return carry

    lax.fori_loop(0, seq // mrows, merge, 0)


def _attn_b(b1, b2, b3):
    bsz = b1.shape[0]
    seq = b1.shape[4]
    nunit = seq // QBLK
    assert nunit % 2 == 0 and nunit >= 4 and nunit == DILATED_PAIRS[-1][1]
    assert b1.shape[3:5] == (1, seq) and b3.shape[4] == QBLK
    specs = [pl.BlockSpec((1,) + b.shape[1:], lambda b_: (b_, 0, 0, 0, 0, 0)) for b in (b1, b2, b3)]
    heads = 2 * PAIRS_B
    return pl.pallas_call(
        _attn_b_kernel,
        grid=(bsz,),
        in_specs=specs,
        out_specs=pl.BlockSpec((1, seq, GRP_B), lambda b_: (b_, 0, 0)),
        out_shape=jax.ShapeDtypeStruct((bsz, seq, GRP_B), BF16),
        scratch_shapes=[pltpu.VMEM((N_GROUPS_B, PAIRS_B, seq, LANES), F32),
                        pltpu.VMEM((N_GROUPS_B, PAIRS_B, seq, LANES), F32),
                        pltpu.VMEM((2, 2, heads, QBLK, 2 * QBLK), F32),
                        pltpu.VMEM((2, heads, QBLK, QBLK), F32),
                        pltpu.VMEM((2, 2, heads, QBLK, 2 * QBLK), BF16),
                        pltpu.VMEM((2, heads, QBLK, QBLK), BF16),
                        pltpu.VMEM((2, N_GROUPS_B, PAIRS_B, QBLK, LANES), F32)],
        compiler_params=pltpu.CompilerParams(
            dimension_semantics=("arbitrary",), vmem_limit_bytes=VMEM_LIMIT),
        name="attn_b",
    )(b1, b2, b3)


def _finish_kernel(x_ref, oa_ref, ob_ref, gate_ref, wpa_ref, wpb_ref, wo_ref, g2_ref,
                   wgu_ref, wdn_ref, gf_ref, y_ref):
    d_ff = wdn_ref.shape[0]
    oa = jnp.concatenate([oa_ref[0, p] for p in range(PAIRS_A)], axis=-1)
    ma = jnp.dot(oa, wpa_ref[...], preferred_element_type=F32)
    mb = jnp.dot(ob_ref[0], wpb_ref[...], preferred_element_type=F32)
    m = gate_ref[0, :, :D_MODEL] * ma + gate_ref[0, :, D_MODEL:] * mb
    x1 = x_ref[0] + jnp.dot(m.astype(BF16), wo_ref[...], preferred_element_type=F32)
    xn = _rms(x1, g2_ref[...]).astype(BF16)
    ck = 2 * LANES
    acc = jnp.zeros_like(x1)
    for c in range(d_ff // ck):
        gg = jnp.dot(xn, wgu_ref[:, c * ck:(c + 1) * ck], preferred_element_type=F32)
        uu = jnp.dot(xn, wgu_ref[:, d_ff + c * ck:d_ff + (c + 1) * ck], preferred_element_type=F32)
        act = (gg * _sigmoid(gg) * uu).astype(BF16)
        acc = acc + jnp.dot(act, wdn_ref[c * ck:(c + 1) * ck, :], preferred_element_type=F32)
    y_ref[0] = _rms(x1 + acc, gf_ref[...])


def _finish(x, oa, ob, gate, w_pa, w_pb, w_o, g2, w_gu, w_down, gf, tm):
    bsz, seq, _ = x.shape
    const2 = lambda b, i: (0, 0)
    wspec = lambda w: pl.BlockSpec(w.shape, const2, pipeline_mode=pl.Buffered(1))
    return pl.pallas_call(
        _finish_kernel,
        grid=(bsz, seq // tm),
        in_specs=[
            pl.BlockSpec((1, tm, D_MODEL), lambda b, i: (b, i, 0)),
            pl.BlockSpec((1, PAIRS_A, tm, LANES), lambda b, i: (b, 0, i, 0)),
            pl.BlockSpec((1, tm, GRP_B), lambda b, i: (b, i, 0)),
            pl.BlockSpec((1, tm, 2 * D_MODEL), lambda b, i: (b, i, 0)),
            wspec(w_pa), wspec(w_pb), wspec(w_o),
            pl.BlockSpec((1, D_MODEL), const2),
            wspec(w_gu), wspec(w_down),
            pl.BlockSpec((1, D_MODEL), const2),
        ],
        out_specs=pl.BlockSpec((1, tm, D_MODEL), lambda b, i: (b, i, 0)),
        out_shape=jax.ShapeDtypeStruct((bsz, seq, D_MODEL), F32),
        compiler_params=pltpu.CompilerParams(
            dimension_semantics=("arbitrary", "arbitrary"), vmem_limit_bytes=VMEM_LIMIT),
        name="finish",
    )(x, oa, ob, gate, w_pa, w_pb, w_o, g2, w_gu, w_down, gf)


ZS_QA = 0
ZS_QB = ZS_QA + Q_A
ZS_KA = ZS_QB + B_W
ZS_VA = ZS_KA + N_KV_A * GROUP_A * HEAD_DIM
ZS_KB = ZS_VA + N_KV_A * GROUP_A * HEAD_DIM
ZS_VB = ZS_KB + B_W
ZS_W = ZS_VB + B_W
KT_KA = 0
KT_VA = KT_KA + KV_A
KT_KB = KT_VA + KV_A
KT_VB = KT_KB + B_W
KT_W = KT_VB + B_W


def _proj_sample_kernel(x_ref, g_ref, w_ref, cos_ref, sina_ref, sinb_ref, zs_ref, kvt_ref, gate_ref):
    nb = x_ref.shape[0]
    h = _rms(x_ref[...], g_ref[...]).astype(BF16)
    cos = cos_ref[...]
    sina = sina_ref[...]
    sinb = sinb_ref[...]
    lo = lax.broadcasted_iota(jnp.int32, (nb, LANES), 1) < HEAD_DIM

    def mm(off):
        z = jnp.dot(h, w_ref[:, off:off + 2 * LANES], preferred_element_type=F32)
        return z[:, :LANES], z[:, LANES:]

    def rope(z):
        return _rope(z, cos, sina, sinb)

    def put(off, val):
        zs_ref[:, off:off + LANES] = val

    for c in range(PAIRS_A // 2):
        z0, z1 = mm(c * 2 * LANES)
        put(ZS_QA + c * 2 * LANES, rope(z0) * SCALE)
        put(ZS_QA + c * 2 * LANES + LANES, rope(z1) * SCALE)
    for c in range(B_W // (2 * LANES)):
        z0, z1 = mm(OFF_QB + c * 2 * LANES)
        put(ZS_QB + c * 2 * LANES, rope(z0) * SCALE)
        put(ZS_QB + c * 2 * LANES + LANES, rope(z1) * SCALE)

    zk, zv = mm(OFF_KA)
    ka = rope(zk)
    kvt_ref[KT_KA:KT_KA + KV_A, :] = ka.T
    kvt_ref[KT_VA:KT_VA + KV_A, :] = zv.T
    reps = GROUP_A * HEAD_DIM // LANES
    for base, val in ((ZS_KA, ka), (ZS_VA, zv)):
        d0, d1 = _dup_halves(val, lo)
        for kv, dup in enumerate((d0, d1)):
            for c in range(reps):
                put(base + (kv * reps + c) * LANES, dup)

    for c in range(B_W // (2 * LANES)):
        k0, k1 = mm(OFF_KB + c * 2 * LANES)
        v0, v1 = mm(OFF_VB + c * 2 * LANES)
        k0 = rope(k0)
        k1 = rope(k1)
        for j, (kk, vv) in enumerate(((k0, v0), (k1, v1))):
            off = (2 * c + j) * LANES
            put(ZS_KB + off, kk)
            put(ZS_VB + off, vv)
            kvt_ref[KT_KB + off:KT_KB + off + LANES, :] = kk.T
            kvt_ref[KT_VB + off:KT_VB + off + LANES, :] = vv.T

    for c in range(2 * D_MODEL // (2 * LANES)):
        z0, z1 = mm(OFF_GATE + c * 2 * LANES)
        gate_ref[:, c * 2 * LANES:c * 2 * LANES + LANES] = _sigmoid(z0)
        gate_ref[:, c * 2 * LANES + LANES:(c + 1) * 2 * LANES] = _sigmoid(z1)


def _proj_sample(x, g1, w_in, tables):
    nb = x.shape[0]
    cos, sina, sinb = tables
    return pl.pallas_call(
        _proj_sample_kernel,
        out_shape=[
            jax.ShapeDtypeStruct((nb, ZS_W), F32),
            jax.ShapeDtypeStruct((KT_W, nb), F32),
            jax.ShapeDtypeStruct((nb, 2 * D_MODEL), F32),
        ],
        compiler_params=pltpu.CompilerParams(vmem_limit_bytes=VMEM_LIMIT),
        name="proj_sample",
    )(x, g1, w_in, cos, sina, sinb)


def _sample_attn_kernel(zs_ref, kvt_ref, sink_ref, ca_ref, c1_ref, c2_ref, c3_ref,
                        sa_ref, s1_ref, s2_ref, s3_ref, oa_ref, ob_ref):
    b = pl.program_id(0)
    nb = kvt_ref.shape[1]
    pick = lax.broadcasted_iota(jnp.int32, (1, nb), 1) == b

    def row(off, width):
        return zs_ref[pl.ds(b, 1), off:off + width]

    def new_col(off, width):
        blk = kvt_ref[off:off + width, :]
        return jnp.sum(jnp.where(pick, blk, 0.0), axis=1, keepdims=True)

    def shifted(tile, col):
        w = tile.shape[1]
        lane = lax.broadcasted_iota(jnp.int32, tile.shape, 1)
        return jnp.where(lane == w - 1, col, pltpu.roll(tile, w - 1, 1))

    def head_mask(nh):
        rows = max(nh, 8)
        hid = lax.broadcasted_iota(jnp.int32, (rows, nh * HEAD_DIM), 0)
        lh = lax.broadcasted_iota(jnp.int32, (rows, nh * HEAD_DIM), 1) // HEAD_DIM
        return hid == lh

    def attend(q_row, kt, vt, knew_row, vnew_row, hm, dil, sink_col):
        qm = jnp.where(hm, q_row, 0.0)
        s = jnp.dot(qm.astype(BF16), kt.astype(BF16), preferred_element_type=F32)
        if dil > 1:
            lane = lax.broadcasted_iota(jnp.int32, s.shape, 1)
            s = jnp.where(lane % dil == 0, s, NEG_INF)
        s_new = jnp.sum(qm * knew_row, axis=1, keepdims=True)
        m = jnp.maximum(jnp.max(s, axis=1, keepdims=True), s_new)
        if sink_col is not None:
            m = jnp.maximum(m, sink_col)
        e = jnp.exp(s - m)
        e_new = jnp.exp(s_new - m)
        l = jnp.sum(e, axis=1, keepdims=True) + e_new
        if sink_col is not None:
            l = l + jnp.exp(sink_col - m)
        num = lax.dot_general(e.astype(BF16), vt.astype(BF16), (((1,), (1,)), ((), ())),
                              preferred_element_type=F32)
        return num + e_new * vnew_row, m, l

    hm_a = head_mask(GROUP_A)
    wq = GROUP_A * HEAD_DIM
    for kv in range(N_KV_A):
        kt = ca_ref[0, 0, kv * HEAD_DIM:(kv + 1) * HEAD_DIM, :]
        vt = ca_ref[0, 1, kv * HEAD_DIM:(kv + 1) * HEAD_DIM, :]
        num, m, l = attend(row(ZS_QA + kv * wq, wq),
                           jnp.concatenate([kt] * GROUP_A, axis=0),
                           jnp.concatenate([vt] * GROUP_A, axis=0),
                           row(ZS_KA + kv * wq, wq), row(ZS_VA + kv * wq, wq),
                           hm_a, 1, sink_ref[kv * GROUP_A:(kv + 1) * GROUP_A, :])
        o = jnp.sum(jnp.where(hm_a, num / l, 0.0), axis=0, keepdims=True)
        oa_ref[pl.ds(b, 1), kv * wq:(kv + 1) * wq] = o
    sa_ref[0, 0] = shifted(ca_ref[0, 0], new_col(KT_KA, KV_A))
    sa_ref[0, 1] = shifted(ca_ref[0, 1], new_col(KT_VA, KV_A))

    hm_b = head_mask(HEADS_B)
    nums, ms, ls = [], [], []
    for g, (c_ref, s_ref, (_, dil)) in enumerate(zip((c1_ref, c2_ref, c3_ref), (s1_ref, s2_ref, s3_ref),
                                                      DILATED_PAIRS)):
        kt = c_ref[0, 0]
        vt = c_ref[0, 1]
        num, m, l = attend(row(ZS_QB + g * GRP_B, GRP_B), kt, vt,
                           row(ZS_KB + g * GRP_B, GRP_B), row(ZS_VB + g * GRP_B, GRP_B),
                           hm_b, dil, None)
        nums.append(num)
        ms.append(m)
        ls.append(l)
        s_ref[0, 0] = shifted(kt, new_col(KT_KB + g * GRP_B, GRP_B))
        s_ref[0, 1] = shifted(vt, new_col(KT_VB + g * GRP_B, GRP_B))
    mx = jnp.maximum(jnp.maximum(ms[0], ms[1]), ms[2])
    ws = [jnp.exp(m - mx) for m in ms]
    num = ws[0] * nums[0] + ws[1] * nums[1] + ws[2] * nums[2]
    den = ws[0] * ls[0] + ws[1] * ls[1] + ws[2] * ls[2]
    ob_ref[pl.ds(b, 1), :] = jnp.sum(jnp.where(hm_b, num / den, 0.0), axis=0, keepdims=True)


def _sample_attn(zs, kvt, sinks_col, caches):
    nb = zs.shape[0]
    full2 = lambda a: pl.BlockSpec(a.shape, lambda b: (0, 0))
    cspec = lambda a: pl.BlockSpec((1,) + a.shape[1:], lambda b: (b, 0, 0, 0))
    return pl.pallas_call(
        _sample_attn_kernel,
        grid=(nb,),
        in_specs=[full2(zs), full2(kvt), full2(sinks_col)] + [cspec(c) for c in caches],
        out_specs=[cspec(c) for c in caches] + [
            pl.BlockSpec((nb, Q_A), lambda b: (0, 0)),
            pl.BlockSpec((nb, GRP_B), lambda b: (0, 0)),
        ],
        out_shape=[jax.ShapeDtypeStruct(c.shape, F32) for c in caches] + [
            jax.ShapeDtypeStruct((nb, Q_A), F32),
            jax.ShapeDtypeStruct((nb, GRP_B), F32),
        ],
        compiler_params=pltpu.CompilerParams(
            dimension_semantics=("arbitrary",), vmem_limit_bytes=VMEM_LIMIT),
        name="sample_attn",
    )(zs, kvt, sinks_col, *caches)


def _rows_minor(c):
    _, bsz, rows, two, nh, hd = c.shape
    return jnp.transpose(c, (0, 1, 3, 4, 5, 2)).reshape(bsz, two, nh * hd, rows)


def _rows_major(c, nh):
    bsz, two, _, rows = c.shape
    return jnp.transpose(c.reshape(1, bsz, two, nh, HEAD_DIM, rows), (0, 1, 5, 2, 3, 4))


def kernel(x_prompt, x_sample, cache_a_kv, cache_b1_kv, cache_b2_kv, cache_b3_kv, norm1_g, w_in, sinks,
           w_pa, w_pb, w_o, norm2_g, w_gu, w_down, final_norm_g):
    bsz, seq, _ = x_prompt.shape
    nb, dec_seq, _ = x_sample.shape
    assert dec_seq == 1 and norm1_g.shape[0] == 1
    assert cache_a_kv.shape[2] == WINDOW_A
    assert tuple(c.shape[2] for c in (cache_b1_kv, cache_b2_kv, cache_b3_kv)) == tuple(
        w for w, _ in DILATED_PAIRS)

    w_in_b = w_in[0].astype(BF16)
    w_pa_b = w_pa[0].astype(BF16)
    w_pb_b = w_pb[0].astype(BF16)
    w_o_b = w_o[0].astype(BF16)
    w_gu_b = w_gu[0].astype(BF16)
    w_dn_b = w_down[0].astype(BF16)
    gf = final_norm_g.reshape(1, D_MODEL)

    tables_p = _rope_tables(jnp.arange(seq, dtype=jnp.int32))
    (qa, kad, vad, b1, b2, b3, gate, pa, pb1, pb2, pb3) = _proj_prompt(
        x_prompt, norm1_g, w_in_b, tables_p, 512)
    oa = _attn_a(sinks[0], qa, kad, vad)
    ob = _attn_b(b1, b2, b3)
    y_prompt = _finish(x_prompt, oa, ob, gate, w_pa_b, w_pb_b, w_o_b, norm2_g, w_gu_b, w_dn_b, gf, 256)

    xs = x_sample.reshape(nb, D_MODEL)
    tables_s = _rope_tables(jnp.full((nb,), PAST_LEN, dtype=jnp.int32))
    zs, kvt, gate_s = _proj_sample(xs, norm1_g, w_in_b, tables_s)
    caches = [_rows_minor(c) for c in (cache_a_kv, cache_b1_kv, cache_b2_kv, cache_b3_kv)]
    sa, sb1, sb2, sb3, oa_s, ob_s = _sample_attn(zs, kvt, sinks[0].reshape(N_HEADS_A, 1), caches)
    oa_s = jnp.transpose(oa_s.reshape(nb, PAIRS_A, LANES), (1, 0, 2))[None].astype(BF16)
    y_sample = _finish(xs[None], oa_s, ob_s[None].astype(BF16), gate_s[None], w_pa_b, w_pb_b, w_o_b,
                       norm2_g, w_gu_b, w_dn_b, gf, nb)

    return (y_prompt, y_sample.reshape(nb, 1, D_MODEL),
            _rows_major(pa, N_KV_A), _rows_major(pb1, HEADS_B), _rows_major(pb2, HEADS_B),
            _rows_major(pb3, HEADS_B),
            _rows_major(sa, N_KV_A), _rows_major(sb1, HEADS_B), _rows_major(sb2, HEADS_B),
            _rows_major(sb3, HEADS_B))
```

```python
import functools

import jax
import jax.numpy as jnp
from jax import lax
from jax.experimental import pallas as pl
from jax.experimental.pallas import tpu as pltpu

F32 = jnp.float32
BF16 = jnp.bfloat16

D_MODEL = 1024
HEAD_DIM = 64
HALF = HEAD_DIM // 2
N_HEADS_A = 16
N_KV_A = 2
GROUP_A = N_HEADS_A // N_KV_A
WINDOW_A = 128
DILATED_PAIRS = ((128, 1), (512, 4), (2048, 16))
N_GROUPS_B = 3
HEADS_B = 4
SPAN = 128
PAST_LEN = 16384
ROPE_THETA = 10000.0
NORM_EPS = 1e-6
NEG_INF = -1e30
SCALE = HEAD_DIM ** -0.5

LANES = 128
PAIRS_A = N_HEADS_A // 2
PAIRS_B = HEADS_B // 2
Q_A = N_HEADS_A * HEAD_DIM
KV_A = N_KV_A * HEAD_DIM
GRP_B = HEADS_B * HEAD_DIM
B_W = N_GROUPS_B * GRP_B
OFF_KA = Q_A
OFF_VA = OFF_KA + KV_A
OFF_QB = OFF_VA + KV_A
OFF_KB = OFF_QB + B_W
OFF_VB = OFF_KB + B_W
OFF_GATE = OFF_VB + B_W
QBLK = 128
B_PARTS = 4
FIN_HALF_MIN = 128

VMEM_LIMIT = 56 * 1024 * 1024


def _rms(x, g):
    return x * lax.rsqrt(jnp.mean(x * x, axis=-1, keepdims=True) + NORM_EPS) * g


def _rope(z, cos, sina, sinb):
    return z * cos + pltpu.roll(z, LANES - HALF, 1) * sina + pltpu.roll(z, HALF, 1) * sinb


def _sigmoid(z):
    return 1.0 / (1.0 + jnp.exp(-z))


def _dup_halves(z, lo):
    r = pltpu.roll(z, HEAD_DIM, 1)
    return jnp.where(lo, z, r), jnp.where(lo, r, z)


def _rope_tables(pos):
    inv = 1.0 / (ROPE_THETA ** (jnp.arange(HALF, dtype=F32) / HALF))
    ang = pos.astype(F32)[:, None] * inv[None, :]
    ang = jnp.concatenate([ang, ang, ang, ang], axis=-1)
    first = (jnp.arange(LANES) % HEAD_DIM) < HALF
    cos = jnp.cos(ang)
    sin = jnp.sin(ang)
    return cos, jnp.where(first, -sin, 0.0), jnp.where(first, 0.0, sin)


def _proj_prompt_kernel(x_ref, g_ref, w_ref, cos_ref, sina_ref, sinb_ref,
                        qa_ref, kad_ref, vad_ref, b1_ref, b2_ref, b3_ref, gate_ref,
                        pa_ref, pb1_ref, pb2_ref, pb3_ref, scr_ref):
    tm = x_ref.shape[1]
    i = pl.program_id(1)
    last = pl.num_programs(1) - 1
    h = _rms(x_ref[0], g_ref[...]).astype(BF16)
    cos = cos_ref[...]
    sina = sina_ref[...]
    sinb = sinb_ref[...]
    lo = lax.broadcasted_iota(jnp.int32, (tm, LANES), 1) < HEAD_DIM

    def mm(off):
        z = jnp.dot(h, w_ref[:, off:off + 2 * LANES], preferred_element_type=F32)
        return z[:, :LANES], z[:, LANES:]

    def rope(z):
        return _rope(z, cos, sina, sinb)

    for c in range(PAIRS_A // 2):
        z0, z1 = mm(c * 2 * LANES)
        qa_ref[0, 2 * c] = (rope(z0) * SCALE).astype(BF16)
        qa_ref[0, 2 * c + 1] = (rope(z1) * SCALE).astype(BF16)

    zk, zv = mm(OFF_KA)
    ka = rope(zk)
    k0, k1 = _dup_halves(ka, lo)
    v0, v1 = _dup_halves(zv, lo)
    kad_ref[0, 0] = k0.astype(BF16)
    kad_ref[0, 1] = k1.astype(BF16)
    for kv, vd in enumerate((v0, v1)):
        vad_ref[0, kv, 0] = jnp.where(lo, vd, 1.0).astype(BF16)
        vad_ref[0, kv, 1] = jnp.where(lo, 1.0, vd).astype(BF16)

    @pl.when(i == last)
    def _():
        pa_ref[0, 0] = ka[tm - WINDOW_A:, :].T
        pa_ref[0, 1] = zv[tm - WINDOW_A:, :].T

    b_refs = (b1_ref, b2_ref, b3_ref)
    for g, (win, dil) in enumerate(DILATED_PAIRS):
        q0, q1 = mm(OFF_QB + g * GRP_B)
        k0, k1 = mm(OFF_KB + g * GRP_B)
        v0, v1 = mm(OFF_VB + g * GRP_B)
        q0 = rope(q0) * SCALE
        q1 = rope(q1) * SCALE
        k0 = rope(k0)
        k1 = rope(k1)
        parts = ((q0, q1), (k0, k1),
                 (jnp.where(lo, v0, 1.0), jnp.where(lo, v1, 1.0)),
                 (jnp.where(lo, 1.0, v0), jnp.where(lo, 1.0, v1)))
        for part in range(B_PARTS):
            for pr in range(PAIRS_B):
                val = parts[part][pr]
                if dil == 1:
                    b_refs[g][0, part, pr, 0] = val.astype(BF16)
                else:
                    slot = part * PAIRS_B + pr
                    scr_ref[slot] = val
                    for r in range(dil):
                        b_refs[g][0, part, pr, r] = scr_ref[
                            slot, pl.ds(r, tm // dil, stride=dil), :].astype(BF16)
        kt = jnp.concatenate([k0, k1], axis=-1).T
        vt = jnp.concatenate([v0, v1], axis=-1).T
        if g == 2:
            pb3_ref[0, 0] = kt
            pb3_ref[0, 1] = vt
        else:
            st_ref = pb1_ref if g == 0 else pb2_ref
            rows = st_ref.shape[3]

            @pl.when(i == last)
            def _(kt=kt, vt=vt, st_ref=st_ref, rows=rows):
                st_ref[0, 0] = kt[:, tm - rows:]
                st_ref[0, 1] = vt[:, tm - rows:]

    for c in range(2 * D_MODEL // (2 * LANES)):
        z0, z1 = mm(OFF_GATE + c * 2 * LANES)
        gate_ref[0, :, c * 2 * LANES:c * 2 * LANES + LANES] = _sigmoid(z0)
        gate_ref[0, :, c * 2 * LANES + LANES:(c + 1) * 2 * LANES] = _sigmoid(z1)


def _proj_prompt(x, g1, w_in, tables, tm):
    bsz, seq, _ = x.shape
    nt = seq // tm
    cos, sina, sinb = tables
    const2 = lambda b, i: (0, 0)
    tab_spec = pl.BlockSpec((tm, LANES), lambda b, i: (i, 0))
    out_shape = [
        jax.ShapeDtypeStruct((bsz, PAIRS_A, seq, LANES), BF16),
        jax.ShapeDtypeStruct((bsz, N_KV_A, seq, LANES), BF16),
        jax.ShapeDtypeStruct((bsz, N_KV_A, 2, seq, LANES), BF16),
    ]
    out_specs = [
        pl.BlockSpec((1, PAIRS_A, tm, LANES), lambda b, i: (b, 0, i, 0)),
        pl.BlockSpec((1, N_KV_A, tm, LANES), lambda b, i: (b, 0, i, 0)),
        pl.BlockSpec((1, N_KV_A, 2, tm, LANES), lambda b, i: (b, 0, 0, i, 0)),
    ]
    for _, dil in DILATED_PAIRS:
        out_shape.append(jax.ShapeDtypeStruct((bsz, B_PARTS, PAIRS_B, dil, seq // dil, LANES), BF16))
        out_specs.append(pl.BlockSpec((1, B_PARTS, PAIRS_B, dil, tm // dil, LANES),
                                      lambda b, i: (b, 0, 0, 0, i, 0)))
    out_shape.append(jax.ShapeDtypeStruct((bsz, seq, 2 * D_MODEL), F32))
    out_specs.append(pl.BlockSpec((1, tm, 2 * D_MODEL), lambda b, i: (b, i, 0)))
    out_shape.append(jax.ShapeDtypeStruct((bsz, 2, KV_A, min(WINDOW_A, seq)), F32))
    out_specs.append(pl.BlockSpec((1, 2, KV_A, min(WINDOW_A, seq)), lambda b, i: (b, 0, 0, 0)))
    for g, (win, _) in enumerate(DILATED_PAIRS):
        rows = min(win, seq)
        out_shape.append(jax.ShapeDtypeStruct((bsz, 2, GRP_B, rows), F32))
        if g == 2:
            out_specs.append(pl.BlockSpec((1, 2, GRP_B, tm), lambda b, i: (b, 0, 0, i)))
        else:
            out_specs.append(pl.BlockSpec((1, 2, GRP_B, rows), lambda b, i: (b, 0, 0, 0)))
    return pl.pallas_call(
        _proj_prompt_kernel,
        grid=(bsz, nt),
        in_specs=[
            pl.BlockSpec((1, tm, D_MODEL), lambda b, i: (b, i, 0)),
            pl.BlockSpec((1, D_MODEL), const2),
            pl.BlockSpec(w_in.shape, const2, pipeline_mode=pl.Buffered(1)),
            tab_spec, tab_spec, tab_spec,
        ],
        out_specs=out_specs,
        out_shape=out_shape,
        scratch_shapes=[pltpu.VMEM((B_PARTS * PAIRS_B, tm, LANES), F32)],
        compiler_params=pltpu.CompilerParams(
            dimension_semantics=("arbitrary", "arbitrary"), vmem_limit_bytes=VMEM_LIMIT),
        name="proj_prompt",
    )(x, g1, w_in, cos, sina, sinb)


def _band_biases():
    qi = lax.broadcasted_iota(jnp.int32, (QBLK, 2 * QBLK), 0)
    ki = lax.broadcasted_iota(jnp.int32, (QBLK, 2 * QBLK), 1)
    band = jnp.where((ki >= qi) & (ki <= qi + SPAN), 0.0, NEG_INF).astype(F32)
    first = jnp.where(ki <= qi, 0.0, NEG_INF).astype(F32)
    return first, band, first[:, :QBLK]


def _head_scores(q2, k, bias, lo, hh):
    qm = jnp.where(lo if hh == 0 else jnp.logical_not(lo), q2, jnp.zeros_like(q2))
    s = lax.dot_general(qm, k, (((1,), (1,)), ((), ())), preferred_element_type=F32)
    return s + bias


def _pair_pv(e0, e1, v_lo, v_hi, lo):
    pv0 = jnp.dot(e0, v_lo, preferred_element_type=F32)
    pv1 = jnp.dot(e1, v_hi, preferred_element_type=F32)
    num = jnp.where(lo, pv0, pv1)
    den = pltpu.roll(jnp.where(lo, pv1, pv0), HEAD_DIM, 1)
    return num, den


def _blk(j):
    return j * QBLK if isinstance(j, int) else pl.multiple_of(j * QBLK, QBLK)


def _win(j):
    return max(j - 1, 0) * QBLK if isinstance(j, int) else pl.multiple_of(jnp.maximum(j - 1, 0) * QBLK, QBLK)


def _attn_a_kernel(sink_ref, qa_ref, kd_ref, va_ref, oa_ref, s_scr, e_scr, m_scr, t_scr):
    seq = qa_ref.shape[2]
    nblk = seq // QBLK
    first, band, _ = _band_biases()
    lo = lax.broadcasted_iota(jnp.int32, (QBLK, LANES), 1) < HEAD_DIM
    pairs_per_kv = PAIRS_A // N_KV_A

    def stage_qk(jb, slot, bias):
        for kv in range(N_KV_A):
            k = kd_ref[0, kv, pl.ds(_win(jb), 2 * QBLK), :]
            for pr in range(pairs_per_kv):
                pair = kv * pairs_per_kv + pr
                q2 = qa_ref[0, pair, pl.ds(_blk(jb), QBLK), :]
                for hh in range(2):
                    h = 2 * pair + hh
                    s = _head_scores(q2, k, bias, lo, hh)
                    s_scr[slot, h] = s
                    m = jnp.broadcast_to(jnp.max(s, axis=1, keepdims=True), (QBLK, LANES))
                    m_scr[slot, h] = jnp.maximum(m, sink_ref[h])

    def stage_sm(slot):
        for pair in range(PAIRS_A):
            ts = []
            for hh in range(2):
                h = 2 * pair + hh
                m = m_scr[slot, h]
                e_scr[slot, h] = jnp.exp(s_scr[slot, h] - jnp.tile(m, (1, 2))).astype(BF16)
                ts.append(jnp.exp(sink_ref[h] - m))
            t_scr[slot, pair] = jnp.where(lo, ts[0], ts[1])

    def stage_pv(jb, slot):
        for kv in range(N_KV_A):
            v_lo = va_ref[0, kv, 0, pl.ds(_win(jb), 2 * QBLK), :]
            v_hi = va_ref[0, kv, 1, pl.ds(_win(jb), 2 * QBLK), :]
            for pr in range(pairs_per_kv):
                pair = kv * pairs_per_kv + pr
                num, den = _pair_pv(e_scr[slot, 2 * pair], e_scr[slot, 2 * pair + 1], v_lo, v_hi, lo)
                den = den + t_scr[slot, pair]
                oa_ref[0, pair, pl.ds(_blk(jb), QBLK), :] = (num / den).astype(BF16)

    stage_qk(0, 0, first)
    stage_qk(1, 1, band)
    stage_sm(0)

    def body(i, carry):
        t = 2 + 2 * i
        stage_pv(t - 2, 0)
        stage_qk(t, 0, band)
        stage_sm(1)
        stage_pv(t - 1, 1)
        stage_qk(t + 1, 1, band)
        stage_sm(0)
        return carry

    lax.fori_loop(0, (nblk - 2) // 2, body, 0)
    stage_sm(1)
    stage_pv(nblk - 2, 0)
    stage_pv(nblk - 1, 1)


def _attn_a(sinks, qa, kad, vad):
    bsz, _, seq, _ = qa.shape
    assert (seq // QBLK) % 2 == 0 and seq >= 2 * QBLK
    return pl.pallas_call(
        _attn_a_kernel,
        grid=(bsz,),
        in_specs=[
            pl.BlockSpec(memory_space=pltpu.SMEM),
            pl.BlockSpec((1, PAIRS_A, seq, LANES), lambda b: (b, 0, 0, 0)),
            pl.BlockSpec((1, N_KV_A, seq, LANES), lambda b: (b, 0, 0, 0)),
            pl.BlockSpec((1, N_KV_A, 2, seq, LANES), lambda b: (b, 0, 0, 0, 0)),
        ],
        out_specs=pl.BlockSpec((1, PAIRS_A, seq, LANES), lambda b: (b, 0, 0, 0)),
        out_shape=jax.ShapeDtypeStruct((bsz, PAIRS_A, seq, LANES), BF16),
        scratch_shapes=[pltpu.VMEM((2, N_HEADS_A, QBLK, 2 * QBLK), F32),
                        pltpu.VMEM((2, N_HEADS_A, QBLK, 2 * QBLK), BF16),
                        pltpu.VMEM((2, N_HEADS_A, QBLK, LANES), F32),
                        pltpu.VMEM((2, PAIRS_A, QBLK, LANES), F32)],
        compiler_params=pltpu.CompilerParams(
            dimension_semantics=("arbitrary",), vmem_limit_bytes=VMEM_LIMIT),
        name="attn_a",
    )(sinks, qa, kad, vad)


def _attn_b_kernel(b1_ref, b2_ref, b3_ref, ob_ref, o_scr, lse_scr, s12_scr, s3_scr, e12_scr, e3_scr, m_scr,
                   mp_scr):
    seq = ob_ref.shape[1]
    nunit = seq // QBLK
    first, band, causal = _band_biases()
    lo = lax.broadcasted_iota(jnp.int32, (QBLK, LANES), 1) < HEAD_DIM
    refs = (b1_ref, b2_ref, b3_ref)
    dils = tuple(d for _, d in DILATED_PAIRS)
    nblks = tuple(nunit // d for d in dils)

    def coords(g, u):
        if nblks[g] == 1:
            return u, 0
        if nblks[g] == nunit:
            return 0, u
        return u // nblks[g], u % nblks[g]

    def stage_qk(u, slot):
        for g in range(N_GROUPS_B):
            r, jb = coords(g, u)
            for pr in range(PAIRS_B):
                q2 = refs[g][0, 0, pr, r, pl.ds(_blk(jb), QBLK), :]
                if nblks[g] == 1:
                    k = refs[g][0, 1, pr, r, :, :]
                    bias = causal
                else:
                    k = refs[g][0, 1, pr, r, pl.ds(_win(jb), 2 * QBLK), :]
                    if isinstance(jb, int):
                        bias = first if jb == 0 else band
                    elif nblks[g] == nunit:
                        bias = band
                    else:
                        bias = jnp.where(jb == 0, first, band)
                for hh in range(2):
                    s = _head_scores(q2, k, bias, lo, hh)
                    if nblks[g] == 1:
                        s3_scr[slot, 2 * pr + hh] = s
                    else:
                        s12_scr[slot, g, 2 * pr + hh] = s
                    m_scr[slot, g, 2 * pr + hh] = jnp.broadcast_to(
                        jnp.max(s, axis=1, keepdims=True), (QBLK, LANES))

    def stage_sm(slot):
        for g in range(N_GROUPS_B):
            for h in range(2 * PAIRS_B):
                m = m_scr[slot, g, h]
                if nblks[g] == 1:
                    e3_scr[slot, h] = jnp.exp(s3_scr[slot, h] - m).astype(BF16)
                else:
                    e12_scr[slot, g, h] = jnp.exp(s12_scr[slot, g, h] - jnp.tile(m, (1, 2))).astype(BF16)
            for pr in range(PAIRS_B):
                mp_scr[slot, g, pr] = jnp.where(lo, m_scr[slot, g, 2 * pr], m_scr[slot, g, 2 * pr + 1])

    def stage_pv(u, slot):
        for g in range(N_GROUPS_B):
            r, jb = coords(g, u)
            for pr in range(PAIRS_B):
                if nblks[g] == 1:
                    v_lo = refs[g][0, 2, pr, r, :, :]
                    v_hi = refs[g][0, 3, pr, r, :, :]
                    e0, e1 = e3_scr[slot, 2 * pr], e3_scr[slot, 2 * pr + 1]
                else:
                    v_lo = refs[g][0, 2, pr, r, pl.ds(_win(jb), 2 * QBLK), :]
                    v_hi = refs[g][0, 3, pr, r, pl.ds(_win(jb), 2 * QBLK), :]
                    e0, e1 = e12_scr[slot, g, 2 * pr], e12_scr[slot, g, 2 * pr + 1]
                num, den = _pair_pv(e0, e1, v_lo, v_hi, lo)
                row0 = _blk(jb) * dils[g] + r
                idx = pl.ds(row0, QBLK) if dils[g] == 1 else pl.ds(row0, QBLK, stride=dils[g])
                o_scr[g, pr, idx, :] = num / den
                lse_scr[g, pr, idx, :] = mp_scr[slot, g, pr] + jnp.log(den)

    stage_qk(0, 0)
    stage_qk(1, 1)
    stage_sm(0)

    def body(i, carry):
        u = 2 + 2 * i
        stage_qk(u, 0)
        stage_sm(1)
        stage_pv(u - 2, 0)
        stage_qk(u + 1, 1)
        stage_sm(0)
        stage_pv(u - 1, 1)
        return carry

    lax.fori_loop(0, (nunit - 2) // 2, body, 0)
    stage_sm(1)
    stage_pv(nunit - 2, 0)
    stage_pv(nunit - 1, 1)

    mrows = 2 * QBLK

    def merge(c, carry):
        rows = pl.ds(pl.multiple_of(c * mrows, mrows), mrows)
        for pr in range(PAIRS_B):
            ls = [lse_scr[g, pr, rows, :] for g in range(N_GROUPS_B)]
            mx = jnp.maximum(jnp.maximum(ls[0], ls[1]), ls[2])
            ws = [jnp.exp(l - mx) for l in ls]
            num = ws[0] * o_scr[0, pr, rows, :] + ws[1] * o_scr[1, pr, rows, :] + ws[2] * o_scr[2, pr, rows, :]
            den = ws[0] + ws[1] + ws[2]
            ob_ref[0, rows, pr * LANES:(pr + 1) * LANES] = (num / den).astype(BF16)
        return carry

    lax.fori_loop(0, seq // mrows, merge, 0)


def _attn_b(b1, b2, b3):
    bsz = b1.shape[0]
    seq = b1.shape[4]
    nunit = seq // QBLK
    assert nunit % 2 == 0 and nunit >= 4 and nunit == DILATED_PAIRS[-1][1]
    assert b1.shape[3:5] == (1, seq) and b3.shape[4] == QBLK
    specs = [pl.BlockSpec((1,) + b.shape[1:], lambda b_: (b_, 0, 0, 0, 0, 0)) for b in (b1, b2, b3)]
    heads = 2 * PAIRS_B
    return pl.pallas_call(
        _attn_b_kernel,
        grid=(bsz,),
        in_specs=specs,
        out_specs=pl.BlockSpec((1, seq, GRP_B), lambda b_: (b_, 0, 0)),
        out_shape=jax.ShapeDtypeStruct((bsz, seq, GRP_B), BF16),
        scratch_shapes=[pltpu.VMEM((N_GROUPS_B, PAIRS_B, seq, LANES), F32),
                        pltpu.VMEM((N_GROUPS_B, PAIRS_B, seq, LANES), F32),
                        pltpu.VMEM((2, 2, heads, QBLK, 2 * QBLK), F32),
                        pltpu.VMEM((2, heads, QBLK, QBLK), F32),
                        pltpu.VMEM((2, 2, heads, QBLK, 2 * QBLK), BF16),
                        pltpu.VMEM((2, heads, QBLK, QBLK), BF16),
                        pltpu.VMEM((2, N_GROUPS_B, heads, QBLK, LANES), F32),
                        pltpu.VMEM((2, N_GROUPS_B, PAIRS_B, QBLK, LANES), F32)],
        compiler_params=pltpu.CompilerParams(
            dimension_semantics=("arbitrary",), vmem_limit_bytes=VMEM_LIMIT),
        name="attn_b",
    )(b1, b2, b3)


def _finish_kernel(x_ref, oa_ref, ob_ref, gate_ref, wpa_ref, wpb_ref, wo_ref, g2_ref,
                   wgu_ref, wdn_ref, gf_ref, y_ref):
    d_ff = wdn_ref.shape[0]
    tm = x_ref.shape[1]
    ck = 2 * LANES
    nchunk = d_ff // ck
    nhalf = 2 if tm % (2 * FIN_HALF_MIN) == 0 else 1
    rows = [pl.ds(i * (tm // nhalf), tm // nhalf) for i in range(nhalf)]

    def mix(r):
        oa = jnp.concatenate([oa_ref[0, p, r, :] for p in range(PAIRS_A)], axis=-1)
        ma = jnp.dot(oa, wpa_ref[...], preferred_element_type=F32)
        mb = jnp.dot(ob_ref[0, r, :], wpb_ref[...], preferred_element_type=F32)
        m = gate_ref[0, r, :D_MODEL] * ma + gate_ref[0, r, D_MODEL:] * mb
        x1 = x_ref[0, r, :] + jnp.dot(m.astype(BF16), wo_ref[...], preferred_element_type=F32)
        return x1, _rms(x1, g2_ref[...]).astype(BF16)

    def ff(xn, c):
        gg = jnp.dot(xn, wgu_ref[:, c * ck:(c + 1) * ck], preferred_element_type=F32)
        uu = jnp.dot(xn, wgu_ref[:, d_ff + c * ck:d_ff + (c + 1) * ck], preferred_element_type=F32)
        act = (gg * _sigmoid(gg) * uu).astype(BF16)
        return jnp.dot(act, wdn_ref[c * ck:(c + 1) * ck, :], preferred_element_type=F32)

    x1s, xns, accs = [], [], []
    for step in range(nchunk + nhalf):
        for i in range(nhalf):
            c = step - 1 - i
            if c == -1:
                x1, xn = mix(rows[i])
                x1s.append(x1)
                xns.append(xn)
                accs.append(None)
            elif 0 <= c < nchunk:
                d = ff(xns[i], c)
                accs[i] = d if accs[i] is None else accs[i] + d
                if c == nchunk - 1:
                    y_ref[0, rows[i], :] = _rms(x1s[i] + accs[i], gf_ref[...])


def _finish(x, oa, ob, gate, w_pa, w_pb, w_o, g2, w_gu, w_down, gf, tm):
    bsz, seq, _ = x.shape
    const2 = lambda b, i: (0, 0)
    wspec = lambda w: pl.BlockSpec(w.shape, const2, pipeline_mode=pl.Buffered(1))
    return pl.pallas_call(
        _finish_kernel,
        grid=(bsz, seq // tm),
        in_specs=[
            pl.BlockSpec((1, tm, D_MODEL), lambda b, i: (b, i, 0)),
            pl.BlockSpec((1, PAIRS_A, tm, LANES), lambda b, i: (b, 0, i, 0)),
            pl.BlockSpec((1, tm, GRP_B), lambda b, i: (b, i, 0)),
            pl.BlockSpec((1, tm, 2 * D_MODEL), lambda b, i: (b, i, 0)),
            wspec(w_pa), wspec(w_pb), wspec(w_o),
            pl.BlockSpec((1, D_MODEL), const2),
            wspec(w_gu), wspec(w_down),
            pl.BlockSpec((1, D_MODEL), const2),
        ],
        out_specs=pl.BlockSpec((1, tm, D_MODEL), lambda b, i: (b, i, 0)),
        out_shape=jax.ShapeDtypeStruct((bsz, seq, D_MODEL), F32),
        compiler_params=pltpu.CompilerParams(
            dimension_semantics=("arbitrary", "arbitrary"), vmem_limit_bytes=VMEM_LIMIT),
        name="finish",
    )(x, oa, ob, gate, w_pa, w_pb, w_o, g2, w_gu, w_down, gf)


ZS_QA = 0
ZS_QB = ZS_QA + Q_A
ZS_KA = ZS_QB + B_W
ZS_VA = ZS_KA + N_KV_A * GROUP_A * HEAD_DIM
ZS_KB = ZS_VA + N_KV_A * GROUP_A * HEAD_DIM
ZS_VB = ZS_KB + B_W
ZS_W = ZS_VB + B_W
KT_KA = 0
KT_VA = KT_KA + KV_A
KT_KB = KT_VA + KV_A
KT_VB = KT_KB + B_W
KT_W = KT_VB + B_W


def _proj_sample_kernel(x_ref, g_ref, w_ref, cos_ref, sina_ref, sinb_ref, zs_ref, kvt_ref, gate_ref):
    nb = x_ref.shape[0]
    h = _rms(x_ref[...], g_ref[...]).astype(BF16)
    cos = cos_ref[...]
    sina = sina_ref[...]
    sinb = sinb_ref[...]
    lo = lax.broadcasted_iota(jnp.int32, (nb, LANES), 1) < HEAD_DIM

    def mm(off):
        z = jnp.dot(h, w_ref[:, off:off + 2 * LANES], preferred_element_type=F32)
        return z[:, :LANES], z[:, LANES:]

    def rope(z):
        return _rope(z, cos, sina, sinb)

    def put(off, val):
        zs_ref[:, off:off + LANES] = val

    for c in range(PAIRS_A // 2):
        z0, z1 = mm(c * 2 * LANES)
        put(ZS_QA + c * 2 * LANES, rope(z0) * SCALE)
        put(ZS_QA + c * 2 * LANES + LANES, rope(z1) * SCALE)
    for c in range(B_W // (2 * LANES)):
        z0, z1 = mm(OFF_QB + c * 2 * LANES)
        put(ZS_QB + c * 2 * LANES, rope(z0) * SCALE)
        put(ZS_QB + c * 2 * LANES + LANES, rope(z1) * SCALE)

    zk, zv = mm(OFF_KA)
    ka = rope(zk)
    kvt_ref[KT_KA:KT_KA + KV_A, :] = ka.T
    kvt_ref[KT_VA:KT_VA + KV_A, :] = zv.T
    reps = GROUP_A * HEAD_DIM // LANES
    for base, val in ((ZS_KA, ka), (ZS_VA, zv)):
        d0, d1 = _dup_halves(val, lo)
        for kv, dup in enumerate((d0, d1)):
            for c in range(reps):
                put(base + (kv * reps + c) * LANES, dup)

    for c in range(B_W // (2 * LANES)):
        k0, k1 = mm(OFF_KB + c * 2 * LANES)
        v0, v1 = mm(OFF_VB + c * 2 * LANES)
        k0 = rope(k0)
        k1 = rope(k1)
        for j, (kk, vv) in enumerate(((k0, v0), (k1, v1))):
            off = (2 * c + j) * LANES
            put(ZS_KB + off, kk)
            put(ZS_VB + off, vv)
            kvt_ref[KT_KB + off:KT_KB + off + LANES, :] = kk.T
            kvt_ref[KT_VB + off:KT_VB + off + LANES, :] = vv.T

    for c in range(2 * D_MODEL // (2 * LANES)):
        z0, z1 = mm(OFF_GATE + c * 2 * LANES)
        gate_ref[:, c * 2 * LANES:c * 2 * LANES + LANES] = _sigmoid(z0)
        gate_ref[:, c * 2 * LANES + LANES:(c + 1) * 2 * LANES] = _sigmoid(z1)


def _proj_sample(x, g1, w_in, tables):
    nb = x.shape[0]
    cos, sina, sinb = tables
    return pl.pallas_call(
        _proj_sample_kernel,
        out_shape=[
            jax.ShapeDtypeStruct((nb, ZS_W), F32),
            jax.ShapeDtypeStruct((KT_W, nb), F32),
            jax.ShapeDtypeStruct((nb, 2 * D_MODEL), F32),
        ],
        compiler_params=pltpu.CompilerParams(vmem_limit_bytes=VMEM_LIMIT),
        name="proj_sample",
    )(x, g1, w_in, cos, sina, sinb)


def _sample_attn_kernel(zs_ref, kvt_ref, sink_ref, ca_ref, c1_ref, c2_ref, c3_ref,
                        sa_ref, s1_ref, s2_ref, s3_ref, oa_ref, ob_ref):
    b = pl.program_id(0)
    nb = kvt_ref.shape[1]
    pick = lax.broadcasted_iota(jnp.int32, (1, nb), 1) == b

    def row(off, width):
        return zs_ref[pl.ds(b, 1), off:off + width]

    def new_col(off, width):
        blk = kvt_ref[off:off + width, :]
        return jnp.sum(jnp.where(pick, blk, 0.0), axis=1, keepdims=True)

    def shifted(tile, col):
        w = tile.shape[1]
        lane = lax.broadcasted_iota(jnp.int32, tile.shape, 1)
        return jnp.where(lane == w - 1, col, pltpu.roll(tile, w - 1, 1))

    def head_mask(nh):
        rows = max(nh, 8)
        hid = lax.broadcasted_iota(jnp.int32, (rows, nh * HEAD_DIM), 0)
        lh = lax.broadcasted_iota(jnp.int32, (rows, nh * HEAD_DIM), 1) // HEAD_DIM
        return hid == lh

    def attend(q_row, kt, vt, knew_row, vnew_row, hm, dil, sink_col):
        qm = jnp.where(hm, q_row, 0.0)
        s = jnp.dot(qm.astype(BF16), kt.astype(BF16), preferred_element_type=F32)
        if dil > 1:
            lane = lax.broadcasted_iota(jnp.int32, s.shape, 1)
            s = jnp.where(lane % dil == 0, s, NEG_INF)
        s_new = jnp.sum(qm * knew_row, axis=1, keepdims=True)
        m = jnp.maximum(jnp.max(s, axis=1, keepdims=True), s_new)
        if sink_col is not None:
            m = jnp.maximum(m, sink_col)
        e = jnp.exp(s - m)
        e_new = jnp.exp(s_new - m)
        l = jnp.sum(e, axis=1, keepdims=True) + e_new
        if sink_col is not None:
            l = l + jnp.exp(sink_col - m)
        num = lax.dot_general(e.astype(BF16), vt.astype(BF16), (((1,), (1,)), ((), ())),
                              preferred_element_type=F32)
        return num + e_new * vnew_row, m, l

    hm_a = head_mask(GROUP_A)
    wq = GROUP_A * HEAD_DIM
    for kv in range(N_KV_A):
        kt = ca_ref[0, 0, kv * HEAD_DIM:(kv + 1) * HEAD_DIM, :]
        vt = ca_ref[0, 1, kv * HEAD_DIM:(kv + 1) * HEAD_DIM, :]
        num, m, l = attend(row(ZS_QA + kv * wq, wq),
                           jnp.concatenate([kt] * GROUP_A, axis=0),
                           jnp.concatenate([vt] * GROUP_A, axis=0),
                           row(ZS_KA + kv * wq, wq), row(ZS_VA + kv * wq, wq),
                           hm_a, 1, sink_ref[kv * GROUP_A:(kv + 1) * GROUP_A, :])
        o = jnp.sum(jnp.where(hm_a, num / l, 0.0), axis=0, keepdims=True)
        oa_ref[pl.ds(b, 1), kv * wq:(kv + 1) * wq] = o
    sa_ref[0, 0] = shifted(ca_ref[0, 0], new_col(KT_KA, KV_A))
    sa_ref[0, 1] = shifted(ca_ref[0, 1], new_col(KT_VA, KV_A))

    hm_b = head_mask(HEADS_B)
    nums, ms, ls = [], [], []
    for g, (c_ref, s_ref, (_, dil)) in enumerate(zip((c1_ref, c2_ref, c3_ref), (s1_ref, s2_ref, s3_ref),
                                                      DILATED_PAIRS)):
        kt = c_ref[0, 0]
        vt = c_ref[0, 1]
        num, m, l = attend(row(ZS_QB + g * GRP_B, GRP_B), kt, vt,
                           row(ZS_KB + g * GRP_B, GRP_B), row(ZS_VB + g * GRP_B, GRP_B),
                           hm_b, dil, None)
        nums.append(num)
        ms.append(m)
        ls.append(l)
        s_ref[0, 0] = shifted(kt, new_col(KT_KB + g * GRP_B, GRP_B))
        s_ref[0, 1] = shifted(vt, new_col(KT_VB + g * GRP_B, GRP_B))
    mx = jnp.maximum(jnp.maximum(ms[0], ms[1]), ms[2])
    ws = [jnp.exp(m - mx) for m in ms]
    num = ws[0] * nums[0] + ws[1] * nums[1] + ws[2] * nums[2]
    den = ws[0] * ls[0] + ws[1] * ls[1] + ws[2] * ls[2]
    ob_ref[pl.ds(b, 1), :] = jnp.sum(jnp.where(hm_b, num / den, 0.0), axis=0, keepdims=True)


def _sample_attn(zs, kvt, sinks_col, caches):
    nb = zs.shape[0]
    full2 = lambda a: pl.BlockSpec(a.shape, lambda b: (0, 0))
    cspec = lambda a: pl.BlockSpec((1,) + a.shape[1:], lambda b: (b, 0, 0, 0))
    return pl.pallas_call(
        _sample_attn_kernel,
        grid=(nb,),
        in_specs=[full2(zs), full2(kvt), full2(sinks_col)] + [cspec(c) for c in caches],
        out_specs=[cspec(c) for c in caches] + [
            pl.BlockSpec((nb, Q_A), lambda b: (0, 0)),
            pl.BlockSpec((nb, GRP_B), lambda b: (0, 0)),
        ],
        out_shape=[jax.ShapeDtypeStruct(c.shape, F32) for c in caches] + [
            jax.ShapeDtypeStruct((nb, Q_A), F32),
            jax.ShapeDtypeStruct((nb, GRP_B), F32),
        ],
        compiler_params=pltpu.CompilerParams(
            dimension_semantics=("arbitrary",), vmem_limit_bytes=VMEM_LIMIT),
        name="sample_attn",
    )(zs, kvt, sinks_col, *caches)


def _rows_minor(c):
    _, bsz, rows, two, nh, hd = c.shape
    return jnp.transpose(c, (0, 1, 3, 4, 5, 2)).reshape(bsz, two, nh * hd, rows)


def _rows_major(c, nh):
    bsz, two, _, rows = c.shape
    return jnp.transpose(c.reshape(1, bsz, two, nh, HEAD_DIM, rows), (0, 1, 5, 2, 3, 4))


def kernel(x_prompt, x_sample, cache_a_kv, cache_b1_kv, cache_b2_kv, cache_b3_kv, norm1_g, w_in, sinks,
           w_pa, w_pb, w_o, norm2_g, w_gu, w_down, final_norm_g):
    bsz, seq, _ = x_prompt.shape
    nb, dec_seq, _ = x_sample.shape
    assert dec_seq == 1 and norm1_g.shape[0] == 1
    assert cache_a_kv.shape[2] == WINDOW_A
    assert tuple(c.shape[2] for c in (cache_b1_kv, cache_b2_kv, cache_b3_kv)) == tuple(
        w for w, _ in DILATED_PAIRS)

    w_in_b = w_in[0].astype(BF16)
    w_pa_b = w_pa[0].astype(BF16)
    w_pb_b = w_pb[0].astype(BF16)
    w_o_b = w_o[0].astype(BF16)
    w_gu_b = w_gu[0].astype(BF16)
    w_dn_b = w_down[0].astype(BF16)
    gf = final_norm_g.reshape(1, D_MODEL)

    tables_p = _rope_tables(jnp.arange(seq, dtype=jnp.int32))
    (qa, kad, vad, b1, b2, b3, gate, pa, pb1, pb2, pb3) = _proj_prompt(
        x_prompt, norm1_g, w_in_b, tables_p, 512)
    oa = _attn_a(sinks[0], qa, kad, vad)
    ob = _attn_b(b1, b2, b3)
    y_prompt = _finish(x_prompt, oa, ob, gate, w_pa_b, w_pb_b, w_o_b, norm2_g, w_gu_b, w_dn_b, gf, 512)

    xs = x_sample.reshape(nb, D_MODEL)
    tables_s = _rope_tables(jnp.full((nb,), PAST_LEN, dtype=jnp.int32))
    zs, kvt, gate_s = _proj_sample(xs, norm1_g, w_in_b, tables_s)
    caches = [_rows_minor(c) for c in (cache_a_kv, cache_b1_kv, cache_b2_kv, cache_b3_kv)]
    sa, sb1, sb2, sb3, oa_s, ob_s = _sample_attn(zs, kvt, sinks[0].reshape(N_HEADS_A, 1), caches)
    oa_s = jnp.transpose(oa_s.reshape(nb, PAIRS_A, LANES), (1, 0, 2))[None].astype(BF16)
    y_sample = _finish(xs[None], oa_s, ob_s[None].astype(BF16), gate_s[None], w_pa_b, w_pb_b, w_o_b,
                       norm2_g, w_gu_b, w_dn_b, gf, nb)

    return (y_prompt, y_sample.reshape(nb, 1, D_MODEL),
            _rows_major(pa, N_KV_A), _rows_major(pb1, HEADS_B), _rows_major(pb2, HEADS_B),
            _rows_major(pb3, HEADS_B),
            _rows_major(sa, N_KV_A), _rows_major(sb1, HEADS_B), _rows_major(sb2, HEADS_B),
            _rows_major(sb3, HEADS_B))
```

```python
import jax
import jax.numpy as jnp
from jax import lax
from jax.experimental import pallas as pl
from jax.experimental.pallas import tpu as pltpu

F32 = jnp.float32
BF16 = jnp.bfloat16

D_MODEL = 1024
HEAD_DIM = 64
HALF = HEAD_DIM // 2
N_HEADS_A = 16
N_KV_A = 2
GROUP_A = N_HEADS_A // N_KV_A
WINDOW_A = 128
DILATED_PAIRS = ((128, 1), (512, 4), (2048, 16))
N_GROUPS_B = 3
HEADS_B = 4
SPAN = 128
PAST_LEN = 16384
ROPE_THETA = 10000.0
NORM_EPS = 1e-6
NEG_INF = -1e30
SCALE = HEAD_DIM ** -0.5

LANES = 128
PAIRS_A = N_HEADS_A // 2
PAIRS_B = HEADS_B // 2
Q_A = N_HEADS_A * HEAD_DIM
KV_A = N_KV_A * HEAD_DIM
GRP_B = HEADS_B * HEAD_DIM
B_W = N_GROUPS_B * GRP_B
OFF_KA = Q_A
OFF_VA = OFF_KA + KV_A
OFF_QB = OFF_VA + KV_A
OFF_KB = OFF_QB + B_W
OFF_VB = OFF_KB + B_W
OFF_GATE = OFF_VB + B_W
QBLK = 128
B_PARTS = 4

VMEM_LIMIT = 56 * 1024 * 1024


def _rms(x, g):
    return x * lax.rsqrt(jnp.mean(x * x, axis=-1, keepdims=True) + NORM_EPS) * g


def _rope(z, cos, sina, sinb):
    return z * cos + pltpu.roll(z, LANES - HALF, 1) * sina + pltpu.roll(z, HALF, 1) * sinb


def _sigmoid(z):
    return 1.0 / (1.0 + jnp.exp(-z))


def _dup_halves(z, lo):
    r = pltpu.roll(z, HEAD_DIM, 1)
    return jnp.where(lo, z, r), jnp.where(lo, r, z)


def _rope_tables(pos):
    inv = 1.0 / (ROPE_THETA ** (jnp.arange(HALF, dtype=F32) / HALF))
    ang = pos.astype(F32)[:, None] * inv[None, :]
    ang = jnp.concatenate([ang, ang, ang, ang], axis=-1)
    first = (jnp.arange(LANES) % HEAD_DIM) < HALF
    cos = jnp.cos(ang)
    sin = jnp.sin(ang)
    return cos, jnp.where(first, -sin, 0.0), jnp.where(first, 0.0, sin)


def _proj_prompt_kernel(x_ref, g_ref, w_ref, cos_ref, sina_ref, sinb_ref,
                        qa_ref, kad_ref, vad_ref, b1_ref, b2_ref, b3_ref, gate_ref,
                        pa_ref, pb1_ref, pb2_ref, pb3_ref, scr_ref):
    tm = x_ref.shape[1]
    i = pl.program_id(1)
    last = pl.num_programs(1) - 1
    h = _rms(x_ref[0], g_ref[...]).astype(BF16)
    cos = cos_ref[...]
    sina = sina_ref[...]
    sinb = sinb_ref[...]
    lo = lax.broadcasted_iota(jnp.int32, (tm, LANES), 1) < HEAD_DIM

    def mm(off):
        z = jnp.dot(h, w_ref[:, off:off + 2 * LANES], preferred_element_type=F32)
        return z[:, :LANES], z[:, LANES:]

    def rope(z):
        return _rope(z, cos, sina, sinb)

    for c in range(PAIRS_A // 2):
        z0, z1 = mm(c * 2 * LANES)
        qa_ref[0, 2 * c] = (rope(z0) * SCALE).astype(BF16)
        qa_ref[0, 2 * c + 1] = (rope(z1) * SCALE).astype(BF16)

    zk, zv = mm(OFF_KA)
    ka = rope(zk)
    k0, k1 = _dup_halves(ka, lo)
    v0, v1 = _dup_halves(zv, lo)
    kad_ref[0, 0] = k0.astype(BF16)
    kad_ref[0, 1] = k1.astype(BF16)
    for kv, vd in enumerate((v0, v1)):
        vad_ref[0, kv, 0] = jnp.where(lo, vd, 1.0).astype(BF16)
        vad_ref[0, kv, 1] = jnp.where(lo, 1.0, vd).astype(BF16)

    @pl.when(i == last)
    def _():
        pa_ref[0, 0] = ka[tm - WINDOW_A:, :].T
        pa_ref[0, 1] = zv[tm - WINDOW_A:, :].T

    b_refs = (b1_ref, b2_ref, b3_ref)
    for g, (win, dil) in enumerate(DILATED_PAIRS):
        q0, q1 = mm(OFF_QB + g * GRP_B)
        k0, k1 = mm(OFF_KB + g * GRP_B)
        v0, v1 = mm(OFF_VB + g * GRP_B)
        q0 = rope(q0) * SCALE
        q1 = rope(q1) * SCALE
        k0 = rope(k0)
        k1 = rope(k1)
        parts = ((q0, q1), (k0, k1),
                 (jnp.where(lo, v0, 1.0), jnp.where(lo, v1, 1.0)),
                 (jnp.where(lo, 1.0, v0), jnp.where(lo, 1.0, v1)))
        for part in range(B_PARTS):
            for pr in range(PAIRS_B):
                val = parts[part][pr]
                if dil == 1:
                    b_refs[g][0, part, pr, 0] = val.astype(BF16)
                else:
                    slot = part * PAIRS_B + pr
                    scr_ref[slot] = val
                    for r in range(dil):
                        b_refs[g][0, part, pr, r] = scr_ref[
                            slot, pl.ds(r, tm // dil, stride=dil), :].astype(BF16)
        kt = jnp.concatenate([k0, k1], axis=-1).T
        vt = jnp.concatenate([v0, v1], axis=-1).T
        if g == 2:
            pb3_ref[0, 0] = kt
            pb3_ref[0, 1] = vt
        else:
            st_ref = pb1_ref if g == 0 else pb2_ref
            rows = st_ref.shape[3]

            @pl.when(i == last)
            def _(kt=kt, vt=vt, st_ref=st_ref, rows=rows):
                st_ref[0, 0] = kt[:, tm - rows:]
                st_ref[0, 1] = vt[:, tm - rows:]

    for c in range(2 * D_MODEL // (2 * LANES)):
        z0, z1 = mm(OFF_GATE + c * 2 * LANES)
        gate_ref[0, :, c * 2 * LANES:c * 2 * LANES + LANES] = _sigmoid(z0)
        gate_ref[0, :, c * 2 * LANES + LANES:(c + 1) * 2 * LANES] = _sigmoid(z1)


def _proj_prompt(x, g1, w_in, tables, tm):
    bsz, seq, _ = x.shape
    nt = seq // tm
    cos, sina, sinb = tables
    const2 = lambda b, i: (0, 0)
    tab_spec = pl.BlockSpec((tm, LANES), lambda b, i: (i, 0))
    out_shape = [
        jax.ShapeDtypeStruct((bsz, PAIRS_A, seq, LANES), BF16),
        jax.ShapeDtypeStruct((bsz, N_KV_A, seq, LANES), BF16),
        jax.ShapeDtypeStruct((bsz, N_KV_A, 2, seq, LANES), BF16),
    ]
    out_specs = [
        pl.BlockSpec((1, PAIRS_A, tm, LANES), lambda b, i: (b, 0, i, 0)),
        pl.BlockSpec((1, N_KV_A, tm, LANES), lambda b, i: (b, 0, i, 0)),
        pl.BlockSpec((1, N_KV_A, 2, tm, LANES), lambda b, i: (b, 0, 0, i, 0)),
    ]
    for _, dil in DILATED_PAIRS:
        out_shape.append(jax.ShapeDtypeStruct((bsz, B_PARTS, PAIRS_B, dil, seq // dil, LANES), BF16))
        out_specs.append(pl.BlockSpec((1, B_PARTS, PAIRS_B, dil, tm // dil, LANES),
                                      lambda b, i: (b, 0, 0, 0, i, 0)))
    out_shape.append(jax.ShapeDtypeStruct((bsz, seq, 2 * D_MODEL), F32))
    out_specs.append(pl.BlockSpec((1, tm, 2 * D_MODEL), lambda b, i: (b, i, 0)))
    out_shape.append(jax.ShapeDtypeStruct((bsz, 2, KV_A, min(WINDOW_A, seq)), F32))
    out_specs.append(pl.BlockSpec((1, 2, KV_A, min(WINDOW_A, seq)), lambda b, i: (b, 0, 0, 0)))
    for g, (win, _) in enumerate(DILATED_PAIRS):
        rows = min(win, seq)
        out_shape.append(jax.ShapeDtypeStruct((bsz, 2, GRP_B, rows), F32))
        if g == 2:
            out_specs.append(pl.BlockSpec((1, 2, GRP_B, tm), lambda b, i: (b, 0, 0, i)))
        else:
            out_specs.append(pl.BlockSpec((1, 2, GRP_B, rows), lambda b, i: (b, 0, 0, 0)))
    return pl.pallas_call(
        _proj_prompt_kernel,
        grid=(bsz, nt),
        in_specs=[
            pl.BlockSpec((1, tm, D_MODEL), lambda b, i: (b, i, 0)),
            pl.BlockSpec((1, D_MODEL), const2),
            pl.BlockSpec(w_in.shape, const2, pipeline_mode=pl.Buffered(1)),
            tab_spec, tab_spec, tab_spec,
        ],
        out_specs=out_specs,
        out_shape=out_shape,
        scratch_shapes=[pltpu.VMEM((B_PARTS * PAIRS_B, tm, LANES), F32)],
        compiler_params=pltpu.CompilerParams(
            dimension_semantics=("arbitrary", "arbitrary"), vmem_limit_bytes=VMEM_LIMIT),
        name="proj_prompt",
    )(x, g1, w_in, cos, sina, sinb)


def _band_biases():
    qi = lax.broadcasted_iota(jnp.int32, (QBLK, 2 * QBLK), 0)
    ki = lax.broadcasted_iota(jnp.int32, (QBLK, 2 * QBLK), 1)
    band = jnp.where((ki >= qi) & (ki <= qi + SPAN), 0.0, NEG_INF).astype(F32)
    first = jnp.where(ki <= qi, 0.0, NEG_INF).astype(F32)
    return first, band, first[:, :QBLK]


def _head_scores(q2, k, bias, lo, hh):
    qm = jnp.where(lo if hh == 0 else jnp.logical_not(lo), q2, jnp.zeros_like(q2))
    s = lax.dot_general(qm, k, (((1,), (1,)), ((), ())), preferred_element_type=F32)
    return s + bias


def _pair_pv(e0, e1, v_lo, v_hi, lo):
    pv0 = jnp.dot(e0, v_lo, preferred_element_type=F32)
    pv1 = jnp.dot(e1, v_hi, preferred_element_type=F32)
    num = jnp.where(lo, pv0, pv1)
    den = pltpu.roll(jnp.where(lo, pv1, pv0), HEAD_DIM, 1)
    return num, den


def _blk(j):
    return j * QBLK if isinstance(j, int) else pl.multiple_of(j * QBLK, QBLK)


def _win(j):
    return max(j - 1, 0) * QBLK if isinstance(j, int) else pl.multiple_of(jnp.maximum(j - 1, 0) * QBLK, QBLK)


def _attn_a_kernel(sink_ref, qa_ref, kd_ref, va_ref, oa_ref, s_scr, e_scr, m_scr, t_scr):
    seq = qa_ref.shape[2]
    nblk = seq // QBLK
    first, band, _ = _band_biases()
    lo = lax.broadcasted_iota(jnp.int32, (QBLK, LANES), 1) < HEAD_DIM
    pairs_per_kv = PAIRS_A // N_KV_A

    def stage_qk(jb, slot, bias):
        for kv in range(N_KV_A):
            k = kd_ref[0, kv, pl.ds(_win(jb), 2 * QBLK), :]
            for pr in range(pairs_per_kv):
                pair = kv * pairs_per_kv + pr
                q2 = qa_ref[0, pair, pl.ds(_blk(jb), QBLK), :]
                for hh in range(2):
                    h = 2 * pair + hh
                    s = _head_scores(q2, k, bias, lo, hh)
                    s_scr[slot, h] = s
                    m = jnp.broadcast_to(jnp.max(s, axis=1, keepdims=True), (QBLK, LANES))
                    m_scr[slot, h] = jnp.maximum(m, sink_ref[h])

    def stage_sm(slot):
        for pair in range(PAIRS_A):
            ms = []
            for hh in range(2):
                h = 2 * pair + hh
                m = m_scr[slot, h]
                e_scr[slot, h] = jnp.exp(s_scr[slot, h] - jnp.tile(m, (1, 2))).astype(BF16)
                ms.append(m)
            sink_pair = jnp.where(lo, sink_ref[2 * pair], sink_ref[2 * pair + 1])
            t_scr[slot, pair] = jnp.exp(sink_pair - jnp.where(lo, ms[0], ms[1]))

    def stage_pv(jb, slot):
        for kv in range(N_KV_A):
            v_lo = va_ref[0, kv, 0, pl.ds(_win(jb), 2 * QBLK), :]
            v_hi = va_ref[0, kv, 1, pl.ds(_win(jb), 2 * QBLK), :]
            for pr in range(pairs_per_kv):
                pair = kv * pairs_per_kv + pr
                num, den = _pair_pv(e_scr[slot, 2 * pair], e_scr[slot, 2 * pair + 1], v_lo, v_hi, lo)
                den = den + t_scr[slot, pair]
                oa_ref[0, pair, pl.ds(_blk(jb), QBLK), :] = (num / den).astype(BF16)

    stage_qk(0, 0, first)
    stage_qk(1, 1, band)
    stage_sm(0)

    def body(i, carry):
        t = 2 + 2 * i
        stage_pv(t - 2, 0)
        stage_qk(t, 0, band)
        stage_sm(1)
        stage_pv(t - 1, 1)
        stage_qk(t + 1, 1, band)
        stage_sm(0)
        return carry

    lax.fori_loop(0, (nblk - 2) // 2, body, 0)
    stage_sm(1)
    stage_pv(nblk - 2, 0)
    stage_pv(nblk - 1, 1)


def _attn_a(sinks, qa, kad, vad):
    bsz, _, seq, _ = qa.shape
    assert (seq // QBLK) % 2 == 0 and seq >= 2 * QBLK
    return pl.pallas_call(
        _attn_a_kernel,
        grid=(bsz,),
        in_specs=[
            pl.BlockSpec(memory_space=pltpu.SMEM),
            pl.BlockSpec((1, PAIRS_A, seq, LANES), lambda b: (b, 0, 0, 0)),
            pl.BlockSpec((1, N_KV_A, seq, LANES), lambda b: (b, 0, 0, 0)),
            pl.BlockSpec((1, N_KV_A, 2, seq, LANES), lambda b: (b, 0, 0, 0, 0)),
        ],
        out_specs=pl.BlockSpec((1, PAIRS_A, seq, LANES), lambda b: (b, 0, 0, 0)),
        out_shape=jax.ShapeDtypeStruct((bsz, PAIRS_A, seq, LANES), BF16),
        scratch_shapes=[pltpu.VMEM((2, N_HEADS_A, QBLK, 2 * QBLK), F32),
                        pltpu.VMEM((2, N_HEADS_A, QBLK, 2 * QBLK), BF16),
                        pltpu.VMEM((2, N_HEADS_A, QBLK, LANES), F32),
                        pltpu.VMEM((2, PAIRS_A, QBLK, LANES), F32)],
        compiler_params=pltpu.CompilerParams(
            dimension_semantics=("arbitrary",), vmem_limit_bytes=VMEM_LIMIT),
        name="attn_a",
    )(sinks, qa, kad, vad)


def _attn_b_kernel(b1_ref, b2_ref, b3_ref, ob_ref, o_scr, lse_scr, s12_scr, s3_scr, e12_scr, e3_scr, m_scr,
                   mp_scr):
    seq = ob_ref.shape[1]
    nunit = seq // QBLK
    first, band, causal = _band_biases()
    lo = lax.broadcasted_iota(jnp.int32, (QBLK, LANES), 1) < HEAD_DIM
    refs = (b1_ref, b2_ref, b3_ref)
    dils = tuple(d for _, d in DILATED_PAIRS)
    nblks = tuple(nunit // d for d in dils)

    def coords(g, u):
        if nblks[g] == 1:
            return u, 0
        if nblks[g] == nunit:
            return 0, u
        return u // nblks[g], u % nblks[g]

    def stage_qk(u, slot):
        for g in range(N_GROUPS_B):
            r, jb = coords(g, u)
            for pr in range(PAIRS_B):
                q2 = refs[g][0, 0, pr, r, pl.ds(_blk(jb), QBLK), :]
                if nblks[g] == 1:
                    k = refs[g][0, 1, pr, r, :, :]
                    bias = causal
                else:
                    k = refs[g][0, 1, pr, r, pl.ds(_win(jb), 2 * QBLK), :]
                    if isinstance(jb, int):
                        bias = first if jb == 0 else band
                    elif nblks[g] == nunit:
                        bias = band
                    else:
                        bias = jnp.where(jb == 0, first, band)
                for hh in range(2):
                    s = _head_scores(q2, k, bias, lo, hh)
                    if nblks[g] == 1:
                        s3_scr[slot, 2 * pr + hh] = s
                    else:
                        s12_scr[slot, g, 2 * pr + hh] = s
                    m_scr[slot, g, 2 * pr + hh] = jnp.broadcast_to(
                        jnp.max(s, axis=1, keepdims=True), (QBLK, LANES))

    def stage_sm(slot):
        for g in range(N_GROUPS_B):
            for h in range(2 * PAIRS_B):
                m = m_scr[slot, g, h]
                if nblks[g] == 1:
                    e3_scr[slot, h] = jnp.exp(s3_scr[slot, h] - m).astype(BF16)
                else:
                    e12_scr[slot, g, h] = jnp.exp(s12_scr[slot, g, h] - jnp.tile(m, (1, 2))).astype(BF16)
            for pr in range(PAIRS_B):
                mp_scr[slot, g, pr] = jnp.where(lo, m_scr[slot, g, 2 * pr], m_scr[slot, g, 2 * pr + 1])

    def stage_pv(u, slot):
        for g in range(N_GROUPS_B):
            r, jb = coords(g, u)
            for pr in range(PAIRS_B):
                if nblks[g] == 1:
                    v_lo = refs[g][0, 2, pr, r, :, :]
                    v_hi = refs[g][0, 3, pr, r, :, :]
                    e0, e1 = e3_scr[slot, 2 * pr], e3_scr[slot, 2 * pr + 1]
                else:
                    v_lo = refs[g][0, 2, pr, r, pl.ds(_win(jb), 2 * QBLK), :]
                    v_hi = refs[g][0, 3, pr, r, pl.ds(_win(jb), 2 * QBLK), :]
                    e0, e1 = e12_scr[slot, g, 2 * pr], e12_scr[slot, g, 2 * pr + 1]
                num, den = _pair_pv(e0, e1, v_lo, v_hi, lo)
                row0 = _blk(jb) * dils[g] + r
                idx = pl.ds(row0, QBLK) if dils[g] == 1 else pl.ds(row0, QBLK, stride=dils[g])
                o_scr[g, pr, idx, :] = num / den
                lse_scr[g, pr, idx, :] = mp_scr[slot, g, pr] + jnp.log(den)

    stage_qk(0, 0)
    stage_qk(1, 1)
    stage_sm(0)

    def body(i, carry):
        u = 2 + 2 * i
        stage_qk(u, 0)
        stage_sm(1)
        stage_pv(u - 2, 0)
        stage_qk(u + 1, 1)
        stage_sm(0)
        stage_pv(u - 1, 1)
        return carry

    lax.fori_loop(0, (nunit - 2) // 2, body, 0)
    stage_sm(1)
    stage_pv(nunit - 2, 0)
    stage_pv(nunit - 1, 1)

    mrows = 2 * QBLK

    def merge(c, carry):
        rows = pl.ds(pl.multiple_of(c * mrows, mrows), mrows)
        for pr in range(PAIRS_B):
            ls = [lse_scr[g, pr, rows, :] for g in range(N_GROUPS_B)]
            mx = jnp.maximum(jnp.maximum(ls[0], ls[1]), ls[2])
            ws = [jnp.exp(l - mx) for l in ls]
            num = ws[0] * o_scr[0, pr, rows, :] + ws[1] * o_scr[1, pr, rows, :] + ws[2] * o_scr[2, pr, rows, :]
            den = ws[0] + ws[1] + ws[2]
            ob_ref[0, rows, pr * LANES:(pr + 1) * LANES] = (num / den).astype(BF16)
        return carry

    lax.fori_loop(0, seq // mrows, merge, 0)


def _attn_b(b1, b2, b3):
    bsz = b1.shape[0]
    seq = b1.shape[4]
    nunit = seq // QBLK
    assert nunit % 2 == 0 and nunit >= 4 and nunit == DILATED_PAIRS[-1][1]
    assert b1.shape[3:5] == (1, seq) and b3.shape[4] == QBLK
    specs = [pl.BlockSpec((1,) + b.shape[1:], lambda b_: (b_, 0, 0, 0, 0, 0)) for b in (b1, b2, b3)]
    heads = 2 * PAIRS_B
    return pl.pallas_call(
        _attn_b_kernel,
        grid=(bsz,),
        in_specs=specs,
        out_specs=pl.BlockSpec((1, seq, GRP_B), lambda b_: (b_, 0, 0)),
        out_shape=jax.ShapeDtypeStruct((bsz, seq, GRP_B), BF16),
        scratch_shapes=[pltpu.VMEM((N_GROUPS_B, PAIRS_B, seq, LANES), F32),
                        pltpu.VMEM((N_GROUPS_B, PAIRS_B, seq, LANES), F32),
                        pltpu.VMEM((2, 2, heads, QBLK, 2 * QBLK), F32),
                        pltpu.VMEM((2, heads, QBLK, QBLK), F32),
                        pltpu.VMEM((2, 2, heads, QBLK, 2 * QBLK), BF16),
                        pltpu.VMEM((2, heads, QBLK, QBLK), BF16),
                        pltpu.VMEM((2, N_GROUPS_B, heads, QBLK, LANES), F32),
                        pltpu.VMEM((2, N_GROUPS_B, PAIRS_B, QBLK, LANES), F32)],
        compiler_params=pltpu.CompilerParams(
            dimension_semantics=("arbitrary",), vmem_limit_bytes=VMEM_LIMIT),
        name="attn_b",
    )(b1, b2, b3)


def _finish_kernel(x_ref, oa_ref, ob_ref, gate_ref, wpa_ref, wpb_ref, wo_ref, g2_ref,
                   wgu_ref, wdn_ref, gf_ref, y_ref):
    d_ff = wdn_ref.shape[0]
    oa = jnp.concatenate([oa_ref[0, p] for p in range(PAIRS_A)], axis=-1)
    ma = jnp.dot(oa, wpa_ref[...], preferred_element_type=F32)
    mb = jnp.dot(ob_ref[0], wpb_ref[...], preferred_element_type=F32)
    m = gate_ref[0, :, :D_MODEL] * ma + gate_ref[0, :, D_MODEL:] * mb
    x1 = x_ref[0] + jnp.dot(m.astype(BF16), wo_ref[...], preferred_element_type=F32)
    xn = _rms(x1, g2_ref[...]).astype(BF16)
    gu = jnp.dot(xn, wgu_ref[...], preferred_element_type=F32)
    gg = gu[:, :d_ff]
    act = (gg * _sigmoid(gg) * gu[:, d_ff:]).astype(BF16)
    y_ref[0] = _rms(x1 + jnp.dot(act, wdn_ref[...], preferred_element_type=F32), gf_ref[...])


def _finish(x, oa, ob, gate, w_pa, w_pb, w_o, g2, w_gu, w_down, gf, tm):
    bsz, seq, _ = x.shape
    const2 = lambda b, i: (0, 0)
    wspec = lambda w: pl.BlockSpec(w.shape, const2, pipeline_mode=pl.Buffered(1))
    return pl.pallas_call(
        _finish_kernel,
        grid=(bsz, seq // tm),
        in_specs=[
            pl.BlockSpec((1, tm, D_MODEL), lambda b, i: (b, i, 0)),
            pl.BlockSpec((1, PAIRS_A, tm, LANES), lambda b, i: (b, 0, i, 0)),
            pl.BlockSpec((1, tm, GRP_B), lambda b, i: (b, i, 0)),
            pl.BlockSpec((1, tm, 2 * D_MODEL), lambda b, i: (b, i, 0)),
            wspec(w_pa), wspec(w_pb), wspec(w_o),
            pl.BlockSpec((1, D_MODEL), const2),
            wspec(w_gu), wspec(w_down),
            pl.BlockSpec((1, D_MODEL), const2),
        ],
        out_specs=pl.BlockSpec((1, tm, D_MODEL), lambda b, i: (b, i, 0)),
        out_shape=jax.ShapeDtypeStruct((bsz, seq, D_MODEL), F32),
        compiler_params=pltpu.CompilerParams(
            dimension_semantics=("arbitrary", "arbitrary"), vmem_limit_bytes=VMEM_LIMIT),
        name="finish",
    )(x, oa, ob, gate, w_pa, w_pb, w_o, g2, w_gu, w_down, gf)


ZS_QA = 0
ZS_QB = ZS_QA + Q_A
ZS_KA = ZS_QB + B_W
ZS_VA = ZS_KA + N_KV_A * GROUP_A * HEAD_DIM
ZS_KB = ZS_VA + N_KV_A * GROUP_A * HEAD_DIM
ZS_VB = ZS_KB + B_W
ZS_W = ZS_VB + B_W
KT_KA = 0
KT_VA = KT_KA + KV_A
KT_KB = KT_VA + KV_A
KT_VB = KT_KB + B_W
KT_W = KT_VB + B_W


def _proj_sample_kernel(x_ref, g_ref, w_ref, cos_ref, sina_ref, sinb_ref, zs_ref, kvt_ref, gate_ref):
    nb = x_ref.shape[0]
    h = _rms(x_ref[...], g_ref[...]).astype(BF16)
    cos = cos_ref[...]
    sina = sina_ref[...]
    sinb = sinb_ref[...]
    lo = lax.broadcasted_iota(jnp.int32, (nb, LANES), 1) < HEAD_DIM

    def mm(off):
        z = jnp.dot(h, w_ref[:, off:off + 2 * LANES], preferred_element_type=F32)
        return z[:, :LANES], z[:, LANES:]

    def rope(z):
        return _rope(z, cos, sina, sinb)

    def put(off, val):
        zs_ref[:, off:off + LANES] = val

    for c in range(PAIRS_A // 2):
        z0, z1 = mm(c * 2 * LANES)
        put(ZS_QA + c * 2 * LANES, rope(z0) * SCALE)
        put(ZS_QA + c * 2 * LANES + LANES, rope(z1) * SCALE)
    for c in range(B_W // (2 * LANES)):
        z0, z1 = mm(OFF_QB + c * 2 * LANES)
        put(ZS_QB + c * 2 * LANES, rope(z0) * SCALE)
        put(ZS_QB + c * 2 * LANES + LANES, rope(z1) * SCALE)

    zk, zv = mm(OFF_KA)
    ka = rope(zk)
    kvt_ref[KT_KA:KT_KA + KV_A, :] = ka.T
    kvt_ref[KT_VA:KT_VA + KV_A, :] = zv.T
    reps = GROUP_A * HEAD_DIM // LANES
    for base, val in ((ZS_KA, ka), (ZS_VA, zv)):
        d0, d1 = _dup_halves(val, lo)
        for kv, dup in enumerate((d0, d1)):
            for c in range(reps):
                put(base + (kv * reps + c) * LANES, dup)

    for c in range(B_W // (2 * LANES)):
        k0, k1 = mm(OFF_KB + c * 2 * LANES)
        v0, v1 = mm(OFF_VB + c * 2 * LANES)
        k0 = rope(k0)
        k1 = rope(k1)
        for j, (kk, vv) in enumerate(((k0, v0), (k1, v1))):
            off = (2 * c + j) * LANES
            put(ZS_KB + off, kk)
            put(ZS_VB + off, vv)
            kvt_ref[KT_KB + off:KT_KB + off + LANES, :] = kk.T
            kvt_ref[KT_VB + off:KT_VB + off + LANES, :] = vv.T

    for c in range(2 * D_MODEL // (2 * LANES)):
        z0, z1 = mm(OFF_GATE + c * 2 * LANES)
        gate_ref[:, c * 2 * LANES:c * 2 * LANES + LANES] = _sigmoid(z0)
        gate_ref[:, c * 2 * LANES + LANES:(c + 1) * 2 * LANES] = _sigmoid(z1)


def _proj_sample(x, g1, w_in, tables):
    nb = x.shape[0]
    cos, sina, sinb = tables
    return pl.pallas_call(
        _proj_sample_kernel,
        out_shape=[
            jax.ShapeDtypeStruct((nb, ZS_W), F32),
            jax.ShapeDtypeStruct((KT_W, nb), F32),
            jax.ShapeDtypeStruct((nb, 2 * D_MODEL), F32),
        ],
        compiler_params=pltpu.CompilerParams(vmem_limit_bytes=VMEM_LIMIT),
        name="proj_sample",
    )(x, g1, w_in, cos, sina, sinb)


def _sample_attn_kernel(zs_ref, kvt_ref, sink_ref, ca_ref, c1_ref, c2_ref, c3_ref,
                        sa_ref, s1_ref, s2_ref, s3_ref, oa_ref, ob_ref):
    b = pl.program_id(0)
    nb = kvt_ref.shape[1]
    pick = lax.broadcasted_iota(jnp.int32, (1, nb), 1) == b

    def row(off, width):
        return zs_ref[pl.ds(b, 1), off:off + width]

    def new_col(off, width):
        blk = kvt_ref[off:off + width, :]
        return jnp.sum(jnp.where(pick, blk, 0.0), axis=1, keepdims=True)

    def shifted(tile, col):
        w = tile.shape[1]
        lane = lax.broadcasted_iota(jnp.int32, tile.shape, 1)
        return jnp.where(lane == w - 1, col, pltpu.roll(tile, w - 1, 1))

    def head_mask(nh):
        rows = max(nh, 8)
        hid = lax.broadcasted_iota(jnp.int32, (rows, nh * HEAD_DIM), 0)
        lh = lax.broadcasted_iota(jnp.int32, (rows, nh * HEAD_DIM), 1) // HEAD_DIM
        return hid == lh

    def attend(q_row, kt, vt, knew_row, vnew_row, hm, dil, sink_col):
        qm = jnp.where(hm, q_row, 0.0)
        s = jnp.dot(qm.astype(BF16), kt.astype(BF16), preferred_element_type=F32)
        if dil > 1:
            lane = lax.broadcasted_iota(jnp.int32, s.shape, 1)
            s = jnp.where(lane % dil == 0, s, NEG_INF)
        s_new = jnp.sum(qm * knew_row, axis=1, keepdims=True)
        m = jnp.maximum(jnp.max(s, axis=1, keepdims=True), s_new)
        if sink_col is not None:
            m = jnp.maximum(m, sink_col)
        e = jnp.exp(s - m)
        e_new = jnp.exp(s_new - m)
        l = jnp.sum(e, axis=1, keepdims=True) + e_new
        if sink_col is not None:
            l = l + jnp.exp(sink_col - m)
        num = lax.dot_general(e.astype(BF16), vt.astype(BF16), (((1,), (1,)), ((), ())),
                              preferred_element_type=F32)
        return num + e_new * vnew_row, m, l

    hm_a = head_mask(GROUP_A)
    wq = GROUP_A * HEAD_DIM
    for kv in range(N_KV_A):
        kt = ca_ref[0, 0, kv * HEAD_DIM:(kv + 1) * HEAD_DIM, :]
        vt = ca_ref[0, 1, kv * HEAD_DIM:(kv + 1) * HEAD_DIM, :]
        num, m, l = attend(row(ZS_QA + kv * wq, wq),
                           jnp.concatenate([kt] * GROUP_A, axis=0),
                           jnp.concatenate([vt] * GROUP_A, axis=0),
                           row(ZS_KA + kv * wq, wq), row(ZS_VA + kv * wq, wq),
                           hm_a, 1, sink_ref[kv * GROUP_A:(kv + 1) * GROUP_A, :])
        o = jnp.sum(jnp.where(hm_a, num / l, 0.0), axis=0, keepdims=True)
        oa_ref[pl.ds(b, 1), kv * wq:(kv + 1) * wq] = o
    sa_ref[0, 0] = shifted(ca_ref[0, 0], new_col(KT_KA, KV_A))
    sa_ref[0, 1] = shifted(ca_ref[0, 1], new_col(KT_VA, KV_A))

    hm_b = head_mask(HEADS_B)
    nums, ms, ls = [], [], []
    for g, (c_ref, s_ref, (_, dil)) in enumerate(zip((c1_ref, c2_ref, c3_ref), (s1_ref, s2_ref, s3_ref),
                                                      DILATED_PAIRS)):
        kt = c_ref[0, 0]
        vt = c_ref[0, 1]
        num, m, l = attend(row(ZS_QB + g * GRP_B, GRP_B), kt, vt,
                           row(ZS_KB + g * GRP_B, GRP_B), row(ZS_VB + g * GRP_B, GRP_B),
                           hm_b, dil, None)
        nums.append(num)
        ms.append(m)
        ls.append(l)
        s_ref[0, 0] = shifted(kt, new_col(KT_KB + g * GRP_B, GRP_B))
        s_ref[0, 1] = shifted(vt, new_col(KT_VB + g * GRP_B, GRP_B))
    mx = jnp.maximum(jnp.maximum(ms[0], ms[1]), ms[2])
    ws = [jnp.exp(m - mx) for m in ms]
    num = ws[0] * nums[0] + ws[1] * nums[1] + ws[2] * nums[2]
    den = ws[0] * ls[0] + ws[1] * ls[1] + ws[2] * ls[2]
    ob_ref[pl.ds(b, 1), :] = jnp.sum(jnp.where(hm_b, num / den, 0.0), axis=0, keepdims=True)


def _sample_attn(zs, kvt, sinks_col, caches):
    nb = zs.shape[0]
    full2 = lambda a: pl.BlockSpec(a.shape, lambda b: (0, 0))
    cspec = lambda a: pl.BlockSpec((1,) + a.shape[1:], lambda b: (b, 0, 0, 0))
    return pl.pallas_call(
        _sample_attn_kernel,
        grid=(nb,),
        in_specs=[full2(zs), full2(kvt), full2(sinks_col)] + [cspec(c) for c in caches],
        out_specs=[cspec(c) for c in caches] + [
            pl.BlockSpec((nb, Q_A), lambda b: (0, 0)),
            pl.BlockSpec((nb, GRP_B), lambda b: (0, 0)),
        ],
        out_shape=[jax.ShapeDtypeStruct(c.shape, F32) for c in caches] + [
            jax.ShapeDtypeStruct((nb, Q_A), F32),
            jax.ShapeDtypeStruct((nb, GRP_B), F32),
        ],
        compiler_params=pltpu.CompilerParams(
            dimension_semantics=("arbitrary",), vmem_limit_bytes=VMEM_LIMIT),
        name="sample_attn",
    )(zs, kvt, sinks_col, *caches)


def _rows_minor(c):
    _, bsz, rows, two, nh, hd = c.shape
    return jnp.transpose(c, (0, 1, 3, 4, 5, 2)).reshape(bsz, two, nh * hd, rows)


def _rows_major(c, nh):
    bsz, two, _, rows = c.shape
    return jnp.transpose(c.reshape(1, bsz, two, nh, HEAD_DIM, rows), (0, 1, 5, 2, 3, 4))


def kernel(x_prompt, x_sample, cache_a_kv, cache_b1_kv, cache_b2_kv, cache_b3_kv, norm1_g, w_in, sinks,
           w_pa, w_pb, w_o, norm2_g, w_gu, w_down, final_norm_g):
    bsz, seq, _ = x_prompt.shape
    nb, dec_seq, _ = x_sample.shape
    assert dec_seq == 1 and norm1_g.shape[0] == 1
    assert cache_a_kv.shape[2] == WINDOW_A
    assert tuple(c.shape[2] for c in (cache_b1_kv, cache_b2_kv, cache_b3_kv)) == tuple(
        w for w, _ in DILATED_PAIRS)

    w_in_b = w_in[0].astype(BF16)
    w_pa_b = w_pa[0].astype(BF16)
    w_pb_b = w_pb[0].astype(BF16)
    w_o_b = w_o[0].astype(BF16)
    w_gu_b = w_gu[0].astype(BF16)
    w_dn_b = w_down[0].astype(BF16)
    gf = final_norm_g.reshape(1, D_MODEL)

    tables_p = _rope_tables(jnp.arange(seq, dtype=jnp.int32))
    (qa, kad, vad, b1, b2, b3, gate, pa, pb1, pb2, pb3) = _proj_prompt(
        x_prompt, norm1_g, w_in_b, tables_p, 512)
    oa = _attn_a(sinks[0], qa, kad, vad)
    ob = _attn_b(b1, b2, b3)
    y_prompt = _finish(x_prompt, oa, ob, gate, w_pa_b, w_pb_b, w_o_b, norm2_g, w_gu_b, w_dn_b, gf, 512)

    xs = x_sample.reshape(nb, D_MODEL)
    tables_s = _rope_tables(jnp.full((nb,), PAST_LEN, dtype=jnp.int32))
    zs, kvt, gate_s = _proj_sample(xs, norm1_g, w_in_b, tables_s)
    caches = [_rows_minor(c) for c in (cache_a_kv, cache_b1_kv, cache_b2_kv, cache_b3_kv)]
    sa, sb1, sb2, sb3, oa_s, ob_s = _sample_attn(zs, kvt, sinks[0].reshape(N_HEADS_A, 1), caches)
    oa_s = jnp.transpose(oa_s.reshape(nb, PAIRS_A, LANES), (1, 0, 2))[None].astype(BF16)
    y_sample = _finish(xs[None], oa_s, ob_s[None].astype(BF16), gate_s[None], w_pa_b, w_pb_b, w_o_b,
                       norm2_g, w_gu_b, w_dn_b, gf, nb)

    return (y_prompt, y_sample.reshape(nb, 1, D_MODEL),
            _rows_major(pa, N_KV_A), _rows_major(pb1, HEADS_B), _rows_major(pb2, HEADS_B),
            _rows_major(pb3, HEADS_B),
            _rows_major(sa, N_KV_A), _rows_major(sb1, HEADS_B), _rows_major(sb2, HEADS_B),
            _rows_major(sb3, HEADS_B))
```

```python
import jax
import jax.numpy as jnp
from jax import lax
from jax.experimental import pallas as pl
from jax.experimental.pallas import tpu as pltpu

F32 = jnp.float32
BF16 = jnp.bfloat16

D_MODEL = 1024
HEAD_DIM = 64
HALF = HEAD_DIM // 2
N_HEADS_A = 16
N_KV_A = 2
GROUP_A = N_HEADS_A // N_KV_A
WINDOW_A = 128
DILATED_PAIRS = ((128, 1), (512, 4), (2048, 16))
N_GROUPS_B = 3
HEADS_B = 4
SPAN = 128
PAST_LEN = 16384
ROPE_THETA = 10000.0
NORM_EPS = 1e-6
NEG_INF = -1e30
SCALE = HEAD_DIM ** -0.5

LANES = 128
PAIRS_A = N_HEADS_A // 2
PAIRS_B = HEADS_B // 2
Q_A = N_HEADS_A * HEAD_DIM
KV_A = N_KV_A * HEAD_DIM
GRP_B = HEADS_B * HEAD_DIM
B_W = N_GROUPS_B * GRP_B
OFF_KA = Q_A
OFF_VA = OFF_KA + KV_A
OFF_QB = OFF_VA + KV_A
OFF_KB = OFF_QB + B_W
OFF_VB = OFF_KB + B_W
OFF_GATE = OFF_VB + B_W
QBLK = 128
B_PARTS = 4

VMEM_LIMIT = 56 * 1024 * 1024


def _rms(x, g):
    return x * lax.rsqrt(jnp.mean(x * x, axis=-1, keepdims=True) + NORM_EPS) * g


def _rope(z, cos, sina, sinb):
    return z * cos + pltpu.roll(z, LANES - HALF, 1) * sina + pltpu.roll(z, HALF, 1) * sinb


def _sigmoid(z):
    return 1.0 / (1.0 + jnp.exp(-z))


def _dup_halves(z, lo):
    r = pltpu.roll(z, HEAD_DIM, 1)
    return jnp.where(lo, z, r), jnp.where(lo, r, z)


def _rope_tables(pos):
    inv = 1.0 / (ROPE_THETA ** (jnp.arange(HALF, dtype=F32) / HALF))
    ang = pos.astype(F32)[:, None] * inv[None, :]
    ang = jnp.concatenate([ang, ang, ang, ang], axis=-1)
    first = (jnp.arange(LANES) % HEAD_DIM) < HALF
    cos = jnp.cos(ang)
    sin = jnp.sin(ang)
    return cos, jnp.where(first, -sin, 0.0), jnp.where(first, 0.0, sin)


def _proj_prompt_kernel(x_ref, g_ref, w_ref, cos_ref, sina_ref, sinb_ref,
                        qa_ref, kad_ref, vad_ref, b1_ref, b2_ref, b3_ref, gate_ref,
                        pa_ref, pb1_ref, pb2_ref, pb3_ref, scr_ref):
    tm = x_ref.shape[1]
    i = pl.program_id(1)
    last = pl.num_programs(1) - 1
    h = _rms(x_ref[0], g_ref[...]).astype(BF16)
    cos = cos_ref[...]
    sina = sina_ref[...]
    sinb = sinb_ref[...]
    lo = lax.broadcasted_iota(jnp.int32, (tm, LANES), 1) < HEAD_DIM

    def mm(off):
        z = jnp.dot(h, w_ref[:, off:off + 2 * LANES], preferred_element_type=F32)
        return z[:, :LANES], z[:, LANES:]

    def rope(z):
        return _rope(z, cos, sina, sinb)

    offs = ([c * 2 * LANES for c in range(PAIRS_A // 2)] + [OFF_KA]
            + [o + g * GRP_B for g in range(N_GROUPS_B) for o in (OFF_QB, OFF_KB, OFF_VB)]
            + [OFF_GATE + c * 2 * LANES for c in range(2 * D_MODEL // (2 * LANES))])
    pending = {offs[0]: mm(offs[0])}

    def take(off):
        nxt = offs.index(off) + 1
        if nxt < len(offs):
            pending[offs[nxt]] = mm(offs[nxt])
        return pending.pop(off)

    for c in range(PAIRS_A // 2):
        z0, z1 = take(c * 2 * LANES)
        qa_ref[0, 2 * c] = (rope(z0) * SCALE).astype(BF16)
        qa_ref[0, 2 * c + 1] = (rope(z1) * SCALE).astype(BF16)

    zk, zv = take(OFF_KA)
    ka = rope(zk)
    k0, k1 = _dup_halves(ka, lo)
    v0, v1 = _dup_halves(zv, lo)
    kad_ref[0, 0] = k0.astype(BF16)
    kad_ref[0, 1] = k1.astype(BF16)
    for kv, vd in enumerate((v0, v1)):
        vad_ref[0, kv, 0] = jnp.where(lo, vd, 1.0).astype(BF16)
        vad_ref[0, kv, 1] = jnp.where(lo, 1.0, vd).astype(BF16)

    @pl.when(i == last)
    def _():
        pa_ref[0, 0] = ka[tm - WINDOW_A:, :].T
        pa_ref[0, 1] = zv[tm - WINDOW_A:, :].T

    b_refs = (b1_ref, b2_ref, b3_ref)
    for g, (win, dil) in enumerate(DILATED_PAIRS):
        q0, q1 = take(OFF_QB + g * GRP_B)
        k0, k1 = take(OFF_KB + g * GRP_B)
        v0, v1 = take(OFF_VB + g * GRP_B)
        q0 = rope(q0) * SCALE
        q1 = rope(q1) * SCALE
        k0 = rope(k0)
        k1 = rope(k1)
        parts = ((q0, q1), (k0, k1),
                 (jnp.where(lo, v0, 1.0), jnp.where(lo, v1, 1.0)),
                 (jnp.where(lo, 1.0, v0), jnp.where(lo, 1.0, v1)))
        for part in range(B_PARTS):
            for pr in range(PAIRS_B):
                val = parts[part][pr]
                if dil == 1:
                    b_refs[g][0, part, pr, 0] = val.astype(BF16)
                else:
                    slot = part * PAIRS_B + pr
                    scr_ref[slot] = val
                    for r in range(dil):
                        b_refs[g][0, part, pr, r] = scr_ref[
                            slot, pl.ds(r, tm // dil, stride=dil), :].astype(BF16)
        kt = jnp.concatenate([k0, k1], axis=-1).T
        vt = jnp.concatenate([v0, v1], axis=-1).T
        if g == 2:
            pb3_ref[0, 0] = kt
            pb3_ref[0, 1] = vt
        else:
            st_ref = pb1_ref if g == 0 else pb2_ref
            rows = st_ref.shape[3]

            @pl.when(i == last)
            def _(kt=kt, vt=vt, st_ref=st_ref, rows=rows):
                st_ref[0, 0] = kt[:, tm - rows:]
                st_ref[0, 1] = vt[:, tm - rows:]

    for c in range(2 * D_MODEL // (2 * LANES)):
        z0, z1 = take(OFF_GATE + c * 2 * LANES)
        gate_ref[0, :, c * 2 * LANES:c * 2 * LANES + LANES] = _sigmoid(z0)
        gate_ref[0, :, c * 2 * LANES + LANES:(c + 1) * 2 * LANES] = _sigmoid(z1)


def _proj_prompt(x, g1, w_in, tables, tm):
    bsz, seq, _ = x.shape
    nt = seq // tm
    cos, sina, sinb = tables
    const2 = lambda b, i: (0, 0)
    tab_spec = pl.BlockSpec((tm, LANES), lambda b, i: (i, 0))
    out_shape = [
        jax.ShapeDtypeStruct((bsz, PAIRS_A, seq, LANES), BF16),
        jax.ShapeDtypeStruct((bsz, N_KV_A, seq, LANES), BF16),
        jax.ShapeDtypeStruct((bsz, N_KV_A, 2, seq, LANES), BF16),
    ]
    out_specs = [
        pl.BlockSpec((1, PAIRS_A, tm, LANES), lambda b, i: (b, 0, i, 0)),
        pl.BlockSpec((1, N_KV_A, tm, LANES), lambda b, i: (b, 0, i, 0)),
        pl.BlockSpec((1, N_KV_A, 2, tm, LANES), lambda b, i: (b, 0, 0, i, 0)),
    ]
    for _, dil in DILATED_PAIRS:
        out_shape.append(jax.ShapeDtypeStruct((bsz, B_PARTS, PAIRS_B, dil, seq // dil, LANES), BF16))
        out_specs.append(pl.BlockSpec((1, B_PARTS, PAIRS_B, dil, tm // dil, LANES),
                                      lambda b, i: (b, 0, 0, 0, i, 0)))
    out_shape.append(jax.ShapeDtypeStruct((bsz, seq, 2 * D_MODEL), F32))
    out_specs.append(pl.BlockSpec((1, tm, 2 * D_MODEL), lambda b, i: (b, i, 0)))
    out_shape.append(jax.ShapeDtypeStruct((bsz, 2, KV_A, min(WINDOW_A, seq)), F32))
    out_specs.append(pl.BlockSpec((1, 2, KV_A, min(WINDOW_A, seq)), lambda b, i: (b, 0, 0, 0)))
    for g, (win, _) in enumerate(DILATED_PAIRS):
        rows = min(win, seq)
        out_shape.append(jax.ShapeDtypeStruct((bsz, 2, GRP_B, rows), F32))
        if g == 2:
            out_specs.append(pl.BlockSpec((1, 2, GRP_B, tm), lambda b, i: (b, 0, 0, i)))
        else:
            out_specs.append(pl.BlockSpec((1, 2, GRP_B, rows), lambda b, i: (b, 0, 0, 0)))
    return pl.pallas_call(
        _proj_prompt_kernel,
        grid=(bsz, nt),
        in_specs=[
            pl.BlockSpec((1, tm, D_MODEL), lambda b, i: (b, i, 0)),
            pl.BlockSpec((1, D_MODEL), const2),
            pl.BlockSpec(w_in.shape, const2, pipeline_mode=pl.Buffered(1)),
            tab_spec, tab_spec, tab_spec,
        ],
        out_specs=out_specs,
        out_shape=out_shape,
        scratch_shapes=[pltpu.VMEM((B_PARTS * PAIRS_B, tm, LANES), F32)],
        compiler_params=pltpu.CompilerParams(
            dimension_semantics=("arbitrary", "arbitrary"), vmem_limit_bytes=VMEM_LIMIT),
        name="proj_prompt",
    )(x, g1, w_in, cos, sina, sinb)


def _band_biases():
    qi = lax.broadcasted_iota(jnp.int32, (QBLK, 2 * QBLK), 0)
    ki = lax.broadcasted_iota(jnp.int32, (QBLK, 2 * QBLK), 1)
    band = jnp.where((ki >= qi) & (ki <= qi + SPAN), 0.0, NEG_INF).astype(F32)
    first = jnp.where(ki <= qi, 0.0, NEG_INF).astype(F32)
    return first, band, first[:, :QBLK]


def _head_scores(q2, k, bias, lo, hh):
    qm = jnp.where(lo if hh == 0 else jnp.logical_not(lo), q2, jnp.zeros_like(q2))
    s = lax.dot_general(qm, k, (((1,), (1,)), ((), ())), preferred_element_type=F32)
    return s + bias


def _pair_pv(e0, e1, v_lo, v_hi, lo):
    pv0 = jnp.dot(e0, v_lo, preferred_element_type=F32)
    pv1 = jnp.dot(e1, v_hi, preferred_element_type=F32)
    num = jnp.where(lo, pv0, pv1)
    den = pltpu.roll(jnp.where(lo, pv1, pv0), HEAD_DIM, 1)
    return num, den


def _blk(j):
    return j * QBLK if isinstance(j, int) else pl.multiple_of(j * QBLK, QBLK)


def _win(j):
    return max(j - 1, 0) * QBLK if isinstance(j, int) else pl.multiple_of(jnp.maximum(j - 1, 0) * QBLK, QBLK)


def _attn_a_kernel(sink_ref, qa_ref, kd_ref, va_ref, oa_ref, s_scr, e_scr, m_scr, t_scr):
    seq = qa_ref.shape[2]
    nblk = seq // QBLK
    first, band, _ = _band_biases()
    lo = lax.broadcasted_iota(jnp.int32, (QBLK, LANES), 1) < HEAD_DIM
    pairs_per_kv = PAIRS_A // N_KV_A

    def stage_qk(jb, slot, bias):
        for kv in range(N_KV_A):
            k = kd_ref[0, kv, pl.ds(_win(jb), 2 * QBLK), :]
            for pr in range(pairs_per_kv):
                pair = kv * pairs_per_kv + pr
                q2 = qa_ref[0, pair, pl.ds(_blk(jb), QBLK), :]
                for hh in range(2):
                    h = 2 * pair + hh
                    s = _head_scores(q2, k, bias, lo, hh)
                    s_scr[slot, h] = s
                    m = jnp.broadcast_to(jnp.max(s, axis=1, keepdims=True), (QBLK, LANES))
                    m_scr[slot, h] = jnp.maximum(m, sink_ref[h])

    def stage_sm(slot):
        for pair in range(PAIRS_A):
            ms = []
            for hh in range(2):
                h = 2 * pair + hh
                m = m_scr[slot, h]
                e_scr[slot, h] = jnp.exp(s_scr[slot, h] - jnp.tile(m, (1, 2))).astype(BF16)
                ms.append(m)
            sink_pair = jnp.where(lo, sink_ref[2 * pair], sink_ref[2 * pair + 1])
            t_scr[slot, pair] = jnp.exp(sink_pair - jnp.where(lo, ms[0], ms[1]))

    def stage_pv(jb, slot):
        for kv in range(N_KV_A):
            v_lo = va_ref[0, kv, 0, pl.ds(_win(jb), 2 * QBLK), :]
            v_hi = va_ref[0, kv, 1, pl.ds(_win(jb), 2 * QBLK), :]
            for pr in range(pairs_per_kv):
                pair = kv * pairs_per_kv + pr
                num, den = _pair_pv(e_scr[slot, 2 * pair], e_scr[slot, 2 * pair + 1], v_lo, v_hi, lo)
                den = den + t_scr[slot, pair]
                oa_ref[0, pair, pl.ds(_blk(jb), QBLK), :] = (num / den).astype(BF16)

    stage_qk(0, 0, first)
    stage_qk(1, 1, band)
    stage_sm(0)

    def body(i, carry):
        t = 2 + 2 * i
        stage_pv(t - 2, 0)
        stage_qk(t, 0, band)
        stage_sm(1)
        stage_pv(t - 1, 1)
        stage_qk(t + 1, 1, band)
        stage_sm(0)
        return carry

    lax.fori_loop(0, (nblk - 2) // 2, body, 0)
    stage_sm(1)
    stage_pv(nblk - 2, 0)
    stage_pv(nblk - 1, 1)


def _attn_a(sinks, qa, kad, vad):
    bsz, _, seq, _ = qa.shape
    assert (seq // QBLK) % 2 == 0 and seq >= 2 * QBLK
    return pl.pallas_call(
        _attn_a_kernel,
        grid=(bsz,),
        in_specs=[
            pl.BlockSpec(memory_space=pltpu.SMEM),
            pl.BlockSpec((1, PAIRS_A, seq, LANES), lambda b: (b, 0, 0, 0)),
            pl.BlockSpec((1, N_KV_A, seq, LANES), lambda b: (b, 0, 0, 0)),
            pl.BlockSpec((1, N_KV_A, 2, seq, LANES), lambda b: (b, 0, 0, 0, 0)),
        ],
        out_specs=pl.BlockSpec((1, PAIRS_A, seq, LANES), lambda b: (b, 0, 0, 0)),
        out_shape=jax.ShapeDtypeStruct((bsz, PAIRS_A, seq, LANES), BF16),
        scratch_shapes=[pltpu.VMEM((2, N_HEADS_A, QBLK, 2 * QBLK), F32),
                        pltpu.VMEM((2, N_HEADS_A, QBLK, 2 * QBLK), BF16),
                        pltpu.VMEM((2, N_HEADS_A, QBLK, LANES), F32),
                        pltpu.VMEM((2, PAIRS_A, QBLK, LANES), F32)],
        compiler_params=pltpu.CompilerParams(
            dimension_semantics=("arbitrary",), vmem_limit_bytes=VMEM_LIMIT),
        name="attn_a",
    )(sinks, qa, kad, vad)


def _attn_b_kernel(b1_ref, b2_ref, b3_ref, ob_ref, o_scr, lse_scr, s12_scr, s3_scr, e12_scr, e3_scr, m_scr,
                   mp_scr):
    seq = ob_ref.shape[1]
    nunit = seq // QBLK
    first, band, causal = _band_biases()
    lo = lax.broadcasted_iota(jnp.int32, (QBLK, LANES), 1) < HEAD_DIM
    refs = (b1_ref, b2_ref, b3_ref)
    dils = tuple(d for _, d in DILATED_PAIRS)
    nblks = tuple(nunit // d for d in dils)

    def coords(g, u):
        if nblks[g] == 1:
            return u, 0
        if nblks[g] == nunit:
            return 0, u
        return u // nblks[g], u % nblks[g]

    def stage_qk(u, slot):
        for g in range(N_GROUPS_B):
            r, jb = coords(g, u)
            for pr in range(PAIRS_B):
                q2 = refs[g][0, 0, pr, r, pl.ds(_blk(jb), QBLK), :]
                if nblks[g] == 1:
                    k = refs[g][0, 1, pr, r, :, :]
                    bias = causal
                else:
                    k = refs[g][0, 1, pr, r, pl.ds(_win(jb), 2 * QBLK), :]
                    if isinstance(jb, int):
                        bias = first if jb == 0 else band
                    elif nblks[g] == nunit:
                        bias = band
                    else:
                        bias = jnp.where(jb == 0, first, band)
                for hh in range(2):
                    s = _head_scores(q2, k, bias, lo, hh)
                    if nblks[g] == 1:
                        s3_scr[slot, 2 * pr + hh] = s
                    else:
                        s12_scr[slot, g, 2 * pr + hh] = s
                    m_scr[slot, g, 2 * pr + hh] = jnp.broadcast_to(
                        jnp.max(s, axis=1, keepdims=True), (QBLK, LANES))

    def stage_sm(slot):
        for g in range(N_GROUPS_B):
            for h in range(2 * PAIRS_B):
                m = m_scr[slot, g, h]
                if nblks[g] == 1:
                    e3_scr[slot, h] = jnp.exp(s3_scr[slot, h] - m).astype(BF16)
                else:
                    e12_scr[slot, g, h] = jnp.exp(s12_scr[slot, g, h] - jnp.tile(m, (1, 2))).astype(BF16)
            for pr in range(PAIRS_B):
                mp_scr[slot, g, pr] = jnp.where(lo, m_scr[slot, g, 2 * pr], m_scr[slot, g, 2 * pr + 1])

    def stage_pv(u, slot):
        for g in range(N_GROUPS_B):
            r, jb = coords(g, u)
            for pr in range(PAIRS_B):
                if nblks[g] == 1:
                    v_lo = refs[g][0, 2, pr, r, :, :]
                    v_hi = refs[g][0, 3, pr, r, :, :]
                    e0, e1 = e3_scr[slot, 2 * pr], e3_scr[slot, 2 * pr + 1]
                else:
                    v_lo = refs[g][0, 2, pr, r, pl.ds(_win(jb), 2 * QBLK), :]
                    v_hi = refs[g][0, 3, pr, r, pl.ds(_win(jb), 2 * QBLK), :]
                    e0, e1 = e12_scr[slot, g, 2 * pr], e12_scr[slot, g, 2 * pr + 1]
                num, den = _pair_pv(e0, e1, v_lo, v_hi, lo)
                row0 = _blk(jb) * dils[g] + r
                idx = pl.ds(row0, QBLK) if dils[g] == 1 else pl.ds(row0, QBLK, stride=dils[g])
                o_scr[g, pr, idx, :] = num / den
                lse_scr[g, pr, idx, :] = mp_scr[slot, g, pr] + jnp.log(den)

    stage_qk(0, 0)
    stage_qk(1, 1)
    stage_sm(0)

    def body(i, carry):
        u = 2 + 2 * i
        stage_qk(u, 0)
        stage_sm(1)
        stage_pv(u - 2, 0)
        stage_qk(u + 1, 1)
        stage_sm(0)
        stage_pv(u - 1, 1)
        return carry

    lax.fori_loop(0, (nunit - 2) // 2, body, 0)
    stage_sm(1)
    stage_pv(nunit - 2, 0)
    stage_pv(nunit - 1, 1)

    mrows = 2 * QBLK

    def merge(c, carry):
        rows = pl.ds(pl.multiple_of(c * mrows, mrows), mrows)
        for pr in range(PAIRS_B):
            ls = [lse_scr[g, pr, rows, :] for g in range(N_GROUPS_B)]
            mx = jnp.maximum(jnp.maximum(ls[0], ls[1]), ls[2])
            ws = [jnp.exp(l - mx) for l in ls]
            num = ws[0] * o_scr[0, pr, rows, :] + ws[1] * o_scr[1, pr, rows, :] + ws[2] * o_scr[2, pr, rows, :]
            den = ws[0] + ws[1] + ws[2]
            ob_ref[0, rows, pr * LANES:(pr + 1) * LANES] = (num / den).astype(BF16)
        return carry

    lax.fori_loop(0, seq // mrows, merge, 0)


def _attn_b(b1, b2, b3):
    bsz = b1.shape[0]
    seq = b1.shape[4]
    nunit = seq // QBLK
    assert nunit % 2 == 0 and nunit >= 4 and nunit == DILATED_PAIRS[-1][1]
    assert b1.shape[3:5] == (1, seq) and b3.shape[4] == QBLK
    specs = [pl.BlockSpec((1,) + b.shape[1:], lambda b_: (b_, 0, 0, 0, 0, 0)) for b in (b1, b2, b3)]
    heads = 2 * PAIRS_B
    return pl.pallas_call(
        _attn_b_kernel,
        grid=(bsz,),
        in_specs=specs,
        out_specs=pl.BlockSpec((1, seq, GRP_B), lambda b_: (b_, 0, 0)),
        out_shape=jax.ShapeDtypeStruct((bsz, seq, GRP_B), BF16),
        scratch_shapes=[pltpu.VMEM((N_GROUPS_B, PAIRS_B, seq, LANES), F32),
                        pltpu.VMEM((N_GROUPS_B, PAIRS_B, seq, LANES), F32),
                        pltpu.VMEM((2, 2, heads, QBLK, 2 * QBLK), F32),
                        pltpu.VMEM((2, heads, QBLK, QBLK), F32),
                        pltpu.VMEM((2, 2, heads, QBLK, 2 * QBLK), BF16),
                        pltpu.VMEM((2, heads, QBLK, QBLK), BF16),
                        pltpu.VMEM((2, N_GROUPS_B, heads, QBLK, LANES), F32),
                        pltpu.VMEM((2, N_GROUPS_B, PAIRS_B, QBLK, LANES), F32)],
        compiler_params=pltpu.CompilerParams(
            dimension_semantics=("arbitrary",), vmem_limit_bytes=VMEM_LIMIT),
        name="attn_b",
    )(b1, b2, b3)


def _finish_kernel(x_ref, oa_ref, ob_ref, gate_ref, wpa_ref, wpb_ref, wo_ref, g2_ref,
                   wgu_ref, wdn_ref, gf_ref, y_ref):
    d_ff = wdn_ref.shape[0]
    oa = jnp.concatenate([oa_ref[0, p] for p in range(PAIRS_A)], axis=-1)
    ma = jnp.dot(oa, wpa_ref[...], preferred_element_type=F32)
    mb = jnp.dot(ob_ref[0], wpb_ref[...], preferred_element_type=F32)
    m = gate_ref[0, :, :D_MODEL] * ma + gate_ref[0, :, D_MODEL:] * mb
    x1 = x_ref[0] + jnp.dot(m.astype(BF16), wo_ref[...], preferred_element_type=F32)
    xn = _rms(x1, g2_ref[...]).astype(BF16)
    gu = jnp.dot(xn, wgu_ref[...], preferred_element_type=F32)
    gg = gu[:, :d_ff]
    act = (gg * _sigmoid(gg) * gu[:, d_ff:]).astype(BF16)
    y_ref[0] = _rms(x1 + jnp.dot(act, wdn_ref[...], preferred_element_type=F32), gf_ref[...])


def _finish(x, oa, ob, gate, w_pa, w_pb, w_o, g2, w_gu, w_down, gf, tm):
    bsz, seq, _ = x.shape
    const2 = lambda b, i: (0, 0)
    wspec = lambda w: pl.BlockSpec(w.shape, const2, pipeline_mode=pl.Buffered(1))
    return pl.pallas_call(
        _finish_kernel,
        grid=(bsz, seq // tm),
        in_specs=[
            pl.BlockSpec((1, tm, D_MODEL), lambda b, i: (b, i, 0)),
            pl.BlockSpec((1, PAIRS_A, tm, LANES), lambda b, i: (b, 0, i, 0)),
            pl.BlockSpec((1, tm, GRP_B), lambda b, i: (b, i, 0)),
            pl.BlockSpec((1, tm, 2 * D_MODEL), lambda b, i: (b, i, 0)),
            wspec(w_pa), wspec(w_pb), wspec(w_o),
            pl.BlockSpec((1, D_MODEL), const2),
            wspec(w_gu), wspec(w_down),
            pl.BlockSpec((1, D_MODEL), const2),
        ],
        out_specs=pl.BlockSpec((1, tm, D_MODEL), lambda b, i: (b, i, 0)),
        out_shape=jax.ShapeDtypeStruct((bsz, seq, D_MODEL), F32),
        compiler_params=pltpu.CompilerParams(
            dimension_semantics=("arbitrary", "arbitrary"), vmem_limit_bytes=VMEM_LIMIT),
        name="finish",
    )(x, oa, ob, gate, w_pa, w_pb, w_o, g2, w_gu, w_down, gf)


ZS_QA = 0
ZS_QB = ZS_QA + Q_A
ZS_KA = ZS_QB + B_W
ZS_VA = ZS_KA + N_KV_A * GROUP_A * HEAD_DIM
ZS_KB = ZS_VA + N_KV_A * GROUP_A * HEAD_DIM
ZS_VB = ZS_KB + B_W
ZS_W = ZS_VB + B_W
KT_KA = 0
KT_VA = KT_KA + KV_A
KT_KB = KT_VA + KV_A
KT_VB = KT_KB + B_W
KT_W = KT_VB + B_W


def _proj_sample_kernel(x_ref, g_ref, w_ref, cos_ref, sina_ref, sinb_ref, zs_ref, kvt_ref, gate_ref):
    nb = x_ref.shape[0]
    h = _rms(x_ref[...], g_ref[...]).astype(BF16)
    cos = cos_ref[...]
    sina = sina_ref[...]
    sinb = sinb_ref[...]
    lo = lax.broadcasted_iota(jnp.int32, (nb, LANES), 1) < HEAD_DIM

    def mm(off):
        z = jnp.dot(h, w_ref[:, off:off + 2 * LANES], preferred_element_type=F32)
        return z[:, :LANES], z[:, LANES:]

    def rope(z):
        return _rope(z, cos, sina, sinb)

    def put(off, val):
        zs_ref[:, off:off + LANES] = val

    for c in range(PAIRS_A // 2):
        z0, z1 = mm(c * 2 * LANES)
        put(ZS_QA + c * 2 * LANES, rope(z0) * SCALE)
        put(ZS_QA + c * 2 * LANES + LANES, rope(z1) * SCALE)
    for c in range(B_W // (2 * LANES)):
        z0, z1 = mm(OFF_QB + c * 2 * LANES)
        put(ZS_QB + c * 2 * LANES, rope(z0) * SCALE)
        put(ZS_QB + c * 2 * LANES + LANES, rope(z1) * SCALE)

    zk, zv = mm(OFF_KA)
    ka = rope(zk)
    kvt_ref[KT_KA:KT_KA + KV_A, :] = ka.T
    kvt_ref[KT_VA:KT_VA + KV_A, :] = zv.T
    reps = GROUP_A * HEAD_DIM // LANES
    for base, val in ((ZS_KA, ka), (ZS_VA, zv)):
        d0, d1 = _dup_halves(val, lo)
        for kv, dup in enumerate((d0, d1)):
            for c in range(reps):
                put(base + (kv * reps + c) * LANES, dup)

    for c in range(B_W // (2 * LANES)):
        k0, k1 = mm(OFF_KB + c * 2 * LANES)
        v0, v1 = mm(OFF_VB + c * 2 * LANES)
        k0 = rope(k0)
        k1 = rope(k1)
        for j, (kk, vv) in enumerate(((k0, v0), (k1, v1))):
            off = (2 * c + j) * LANES
            put(ZS_KB + off, kk)
            put(ZS_VB + off, vv)
            kvt_ref[KT_KB + off:KT_KB + off + LANES, :] = kk.T
            kvt_ref[KT_VB + off:KT_VB + off + LANES, :] = vv.T

    for c in range(2 * D_MODEL // (2 * LANES)):
        z0, z1 = mm(OFF_GATE + c * 2 * LANES)
        gate_ref[:, c * 2 * LANES:c * 2 * LANES + LANES] = _sigmoid(z0)
        gate_ref[:, c * 2 * LANES + LANES:(c + 1) * 2 * LANES] = _sigmoid(z1)


def _proj_sample(x, g1, w_in, tables):
    nb = x.shape[0]
    cos, sina, sinb = tables
    return pl.pallas_call(
        _proj_sample_kernel,
        out_shape=[
            jax.ShapeDtypeStruct((nb, ZS_W), F32),
            jax.ShapeDtypeStruct((KT_W, nb), F32),
            jax.ShapeDtypeStruct((nb, 2 * D_MODEL), F32),
        ],
        compiler_params=pltpu.CompilerParams(vmem_limit_bytes=VMEM_LIMIT),
        name="proj_sample",
    )(x, g1, w_in, cos, sina, sinb)


SEQS_PER_STEP = 2


def _sample_attn_kernel(zs_ref, kvt_ref, sink_ref, ca_ref, c1_ref, c2_ref, c3_ref,
                        sa_ref, s1_ref, s2_ref, s3_ref, oa_ref, ob_ref):
    for j in range(ca_ref.shape[0]):
        _sample_attn_one(j, zs_ref, kvt_ref, sink_ref, ca_ref, c1_ref, c2_ref, c3_ref,
                         sa_ref, s1_ref, s2_ref, s3_ref, oa_ref, ob_ref)


def _sample_attn_one(j, zs_ref, kvt_ref, sink_ref, ca_ref, c1_ref, c2_ref, c3_ref,
                     sa_ref, s1_ref, s2_ref, s3_ref, oa_ref, ob_ref):
    b = pl.program_id(0) * ca_ref.shape[0] + j
    nb = kvt_ref.shape[1]
    pick = lax.broadcasted_iota(jnp.int32, (1, nb), 1) == b

    def row(off, width):
        return zs_ref[pl.ds(b, 1), off:off + width]

    def new_col(off, width):
        blk = kvt_ref[off:off + width, :]
        return jnp.sum(jnp.where(pick, blk, 0.0), axis=1, keepdims=True)

    def shifted(tile, col):
        w = tile.shape[1]
        lane = lax.broadcasted_iota(jnp.int32, tile.shape, 1)
        return jnp.where(lane == w - 1, col, pltpu.roll(tile, w - 1, 1))

    def head_mask(nh):
        rows = max(nh, 8)
        hid = lax.broadcasted_iota(jnp.int32, (rows, nh * HEAD_DIM), 0)
        lh = lax.broadcasted_iota(jnp.int32, (rows, nh * HEAD_DIM), 1) // HEAD_DIM
        return hid == lh

    def attend(q_row, kt, vt, knew_row, vnew_row, hm, dil, sink_col):
        qm = jnp.where(hm, q_row, 0.0)
        s = jnp.dot(qm.astype(BF16), kt.astype(BF16), preferred_element_type=F32)
        if dil > 1:
            lane = lax.broadcasted_iota(jnp.int32, s.shape, 1)
            s = jnp.where(lane % dil == 0, s, NEG_INF)
        s_new = jnp.sum(qm * knew_row, axis=1, keepdims=True)
        m = jnp.maximum(jnp.max(s, axis=1, keepdims=True), s_new)
        if sink_col is not None:
            m = jnp.maximum(m, sink_col)
        e = jnp.exp(s - m)
        e_new = jnp.exp(s_new - m)
        l = jnp.sum(e, axis=1, keepdims=True) + e_new
        if sink_col is not None:
            l = l + jnp.exp(sink_col - m)
        num = lax.dot_general(e.astype(BF16), vt.astype(BF16), (((1,), (1,)), ((), ())),
                              preferred_element_type=F32)
        return num + e_new * vnew_row, m, l

    hm_a = head_mask(GROUP_A)
    wq = GROUP_A * HEAD_DIM
    for kv in range(N_KV_A):
        kt = ca_ref[j, 0, kv * HEAD_DIM:(kv + 1) * HEAD_DIM, :]
        vt = ca_ref[j, 1, kv * HEAD_DIM:(kv + 1) * HEAD_DIM, :]
        num, m, l = attend(row(ZS_QA + kv * wq, wq),
                           jnp.concatenate([kt] * GROUP_A, axis=0),
                           jnp.concatenate([vt] * GROUP_A, axis=0),
                           row(ZS_KA + kv * wq, wq), row(ZS_VA + kv * wq, wq),
                           hm_a, 1, sink_ref[kv * GROUP_A:(kv + 1) * GROUP_A, :])
        o = jnp.sum(jnp.where(hm_a, num / l, 0.0), axis=0, keepdims=True)
        oa_ref[pl.ds(b, 1), kv * wq:(kv + 1) * wq] = o
    sa_ref[j, 0] = shifted(ca_ref[j, 0], new_col(KT_KA, KV_A))
    sa_ref[j, 1] = shifted(ca_ref[j, 1], new_col(KT_VA, KV_A))

    hm_b = head_mask(HEADS_B)
    nums, ms, ls = [], [], []
    for g, (c_ref, s_ref, (_, dil)) in enumerate(zip((c1_ref, c2_ref, c3_ref), (s1_ref, s2_ref, s3_ref),
                                                      DILATED_PAIRS)):
        kt = c_ref[j, 0]
        vt = c_ref[j, 1]
        num, m, l = attend(row(ZS_QB + g * GRP_B, GRP_B), kt, vt,
                           row(ZS_KB + g * GRP_B, GRP_B), row(ZS_VB + g * GRP_B, GRP_B),
                           hm_b, dil, None)
        nums.append(num)
        ms.append(m)
        ls.append(l)
        s_ref[j, 0] = shifted(kt, new_col(KT_KB + g * GRP_B, GRP_B))
        s_ref[j, 1] = shifted(vt, new_col(KT_VB + g * GRP_B, GRP_B))
    mx = jnp.maximum(jnp.maximum(ms[0], ms[1]), ms[2])
    ws = [jnp.exp(m - mx) for m in ms]
    num = ws[0] * nums[0] + ws[1] * nums[1] + ws[2] * nums[2]
    den = ws[0] * ls[0] + ws[1] * ls[1] + ws[2] * ls[2]
    ob_ref[pl.ds(b, 1), :] = jnp.sum(jnp.where(hm_b, num / den, 0.0), axis=0, keepdims=True)


def _sample_attn(zs, kvt, sinks_col, caches):
    nb = zs.shape[0]
    full2 = lambda a: pl.BlockSpec(a.shape, lambda b: (0, 0))
    cspec = lambda a: pl.BlockSpec((SEQS_PER_STEP,) + a.shape[1:], lambda b: (b, 0, 0, 0))
    return pl.pallas_call(
        _sample_attn_kernel,
        grid=(nb // SEQS_PER_STEP,),
        in_specs=[full2(zs), full2(kvt), full2(sinks_col)] + [cspec(c) for c in caches],
        out_specs=[cspec(c) for c in caches] + [
            pl.BlockSpec((nb, Q_A), lambda b: (0, 0)),
            pl.BlockSpec((nb, GRP_B), lambda b: (0, 0)),
        ],
        out_shape=[jax.ShapeDtypeStruct(c.shape, F32) for c in caches] + [
            jax.ShapeDtypeStruct((nb, Q_A), F32),
            jax.ShapeDtypeStruct((nb, GRP_B), F32),
        ],
        compiler_params=pltpu.CompilerParams(
            dimension_semantics=("arbitrary",), vmem_limit_bytes=VMEM_LIMIT),
        name="sample_attn",
    )(zs, kvt, sinks_col, *caches)


def _rows_minor(c):
    _, bsz, rows, two, nh, hd = c.shape
    return jnp.transpose(c, (0, 1, 3, 4, 5, 2)).reshape(bsz, two, nh * hd, rows)


def _rows_major(c, nh):
    bsz, two, _, rows = c.shape
    return jnp.transpose(c.reshape(1, bsz, two, nh, HEAD_DIM, rows), (0, 1, 5, 2, 3, 4))


def kernel(x_prompt, x_sample, cache_a_kv, cache_b1_kv, cache_b2_kv, cache_b3_kv, norm1_g, w_in, sinks,
           w_pa, w_pb, w_o, norm2_g, w_gu, w_down, final_norm_g):
    bsz, seq, _ = x_prompt.shape
    nb, dec_seq, _ = x_sample.shape
    assert dec_seq == 1 and norm1_g.shape[0] == 1
    assert cache_a_kv.shape[2] == WINDOW_A
    assert tuple(c.shape[2] for c in (cache_b1_kv, cache_b2_kv, cache_b3_kv)) == tuple(
        w for w, _ in DILATED_PAIRS)

    w_in_b = w_in[0].astype(BF16)
    w_pa_b = w_pa[0].astype(BF16)
    w_pb_b = w_pb[0].astype(BF16)
    w_o_b = w_o[0].astype(BF16)
    w_gu_b = w_gu[0].astype(BF16)
    w_dn_b = w_down[0].astype(BF16)
    gf = final_norm_g.reshape(1, D_MODEL)

    tables_p = _rope_tables(jnp.arange(seq, dtype=jnp.int32))
    (qa, kad, vad, b1, b2, b3, gate, pa, pb1, pb2, pb3) = _proj_prompt(
        x_prompt, norm1_g, w_in_b, tables_p, 512)
    oa = _attn_a(sinks[0], qa, kad, vad)
    ob = _attn_b(b1, b2, b3)
    y_prompt = _finish(x_prompt, oa, ob, gate, w_pa_b, w_pb_b, w_o_b, norm2_g, w_gu_b, w_dn_b, gf, 512)

    xs = x_sample.reshape(nb, D_MODEL)
    tables_s = _rope_tables(jnp.full((nb,), PAST_LEN, dtype=jnp.int32))
    zs, kvt, gate_s = _proj_sample(xs, norm1_g, w_in_b, tables_s)
    caches = [_rows_minor(c) for c in (cache_a_kv, cache_b1_kv, cache_b2_kv, cache_b3_kv)]
    sa, sb1, sb2, sb3, oa_s, ob_s = _sample_attn(zs, kvt, sinks[0].reshape(N_HEADS_A, 1), caches)
    oa_s = jnp.transpose(oa_s.reshape(nb, PAIRS_A, LANES), (1, 0, 2))[None].astype(BF16)
    y_sample = _finish(xs[None], oa_s, ob_s[None].astype(BF16), gate_s[None], w_pa_b, w_pb_b, w_o_b,
                       norm2_g, w_gu_b, w_dn_b, gf, nb)

    return (y_prompt, y_sample.reshape(nb, 1, D_MODEL),
            _rows_major(pa, N_KV_A), _rows_major(pb1, HEADS_B), _rows_major(pb2, HEADS_B),
            _rows_major(pb3, HEADS_B),
            _rows_major(sa, N_KV_A), _rows_major(sb1, HEADS_B), _rows_major(sb2, HEADS_B),
            _rows_major(sb3, HEADS_B))
```

```python
import jax
import jax.numpy as jnp
from jax import lax
from jax.experimental import pallas as pl
from jax.experimental.pallas import tpu as pltpu

F32 = jnp.float32
BF16 = jnp.bfloat16

D_MODEL = 1024
HEAD_DIM = 64
HALF = HEAD_DIM // 2
N_HEADS_A = 16
N_KV_A = 2
GROUP_A = N_HEADS_A // N_KV_A
WINDOW_A = 128
DILATED_PAIRS = ((128, 1), (512, 4), (2048, 16))
N_GROUPS_B = 3
HEADS_B = 4
SPAN = 128
PAST_LEN = 16384
ROPE_THETA = 10000.0
NORM_EPS = 1e-6
NEG_INF = -1e30
SCALE = HEAD_DIM ** -0.5

LANES = 128
PAIRS_A = N_HEADS_A // 2
PAIRS_B = HEADS_B // 2
Q_A = N_HEADS_A * HEAD_DIM
KV_A = N_KV_A * HEAD_DIM
GRP_B = HEADS_B * HEAD_DIM
B_W = N_GROUPS_B * GRP_B
OFF_KA = Q_A
OFF_VA = OFF_KA + KV_A
OFF_QB = OFF_VA + KV_A
OFF_KB = OFF_QB + B_W
OFF_VB = OFF_KB + B_W
OFF_GATE = OFF_VB + B_W
QBLK = 128
B_PARTS = 4

VMEM_LIMIT = 56 * 1024 * 1024
TM_PROJ = 512
TM_FINISH = 512


def _rms(x, g):
    return x * lax.rsqrt(jnp.mean(x * x, axis=-1, keepdims=True) + NORM_EPS) * g


def _rope(z, cos, sina, sinb):
    return z * cos + pltpu.roll(z, LANES - HALF, 1) * sina + pltpu.roll(z, HALF, 1) * sinb


def _sigmoid(z):
    return 1.0 / (1.0 + jnp.exp(-z))


def _dup_halves(z, lo):
    r = pltpu.roll(z, HEAD_DIM, 1)
    return jnp.where(lo, z, r), jnp.where(lo, r, z)


def _rope_tables(pos):
    inv = 1.0 / (ROPE_THETA ** (jnp.arange(HALF, dtype=F32) / HALF))
    ang = pos.astype(F32)[:, None] * inv[None, :]
    ang = jnp.concatenate([ang, ang, ang, ang], axis=-1)
    first = (jnp.arange(LANES) % HEAD_DIM) < HALF
    cos = jnp.cos(ang)
    sin = jnp.sin(ang)
    return cos, jnp.where(first, -sin, 0.0), jnp.where(first, 0.0, sin)


def _proj_prompt_kernel(x_ref, g_ref, w_ref, cos_ref, sina_ref, sinb_ref,
                        qa_ref, kad_ref, vad_ref, b1_ref, b2_ref, b3_ref, gate_ref,
                        pa_ref, pb1_ref, pb2_ref, pb3_ref, scr_ref):
    tm = x_ref.shape[1]
    i = pl.program_id(1)
    last = pl.num_programs(1) - 1
    h = _rms(x_ref[0], g_ref[...]).astype(BF16)
    cos = cos_ref[...]
    sina = sina_ref[...]
    sinb = sinb_ref[...]
    lo = lax.broadcasted_iota(jnp.int32, (tm, LANES), 1) < HEAD_DIM

    def mm(off):
        z = jnp.dot(h, w_ref[:, off:off + 2 * LANES], preferred_element_type=F32)
        return z[:, :LANES], z[:, LANES:]

    def rope(z):
        return _rope(z, cos, sina, sinb)

    offs = ([c * 2 * LANES for c in range(PAIRS_A // 2)] + [OFF_KA]
            + [o + g * GRP_B for g in range(N_GROUPS_B) for o in (OFF_QB, OFF_KB, OFF_VB)]
            + [OFF_GATE + c * 2 * LANES for c in range(2 * D_MODEL // (2 * LANES))])
    pending = {offs[0]: mm(offs[0])}

    def take(off):
        nxt = offs.index(off) + 1
        if nxt < len(offs):
            pending[offs[nxt]] = mm(offs[nxt])
        return pending.pop(off)

    for c in range(PAIRS_A // 2):
        z0, z1 = take(c * 2 * LANES)
        qa_ref[0, 2 * c] = (rope(z0) * SCALE).astype(BF16)
        qa_ref[0, 2 * c + 1] = (rope(z1) * SCALE).astype(BF16)

    zk, zv = take(OFF_KA)
    ka = rope(zk)
    k0, k1 = _dup_halves(ka, lo)
    v0, v1 = _dup_halves(zv, lo)
    kad_ref[0, 0] = k0.astype(BF16)
    kad_ref[0, 1] = k1.astype(BF16)
    for kv, vd in enumerate((v0, v1)):
        vad_ref[0, kv, 0] = jnp.where(lo, vd, 1.0).astype(BF16)
        vad_ref[0, kv, 1] = jnp.where(lo, 1.0, vd).astype(BF16)

    @pl.when(i == last)
    def _():
        pa_ref[0, 0] = ka[tm - WINDOW_A:, :].T
        pa_ref[0, 1] = zv[tm - WINDOW_A:, :].T

    b_refs = (b1_ref, b2_ref, b3_ref)
    for g, (win, dil) in enumerate(DILATED_PAIRS):
        q0, q1 = take(OFF_QB + g * GRP_B)
        k0, k1 = take(OFF_KB + g * GRP_B)
        v0, v1 = take(OFF_VB + g * GRP_B)
        q0 = rope(q0) * SCALE
        q1 = rope(q1) * SCALE
        k0 = rope(k0)
        k1 = rope(k1)
        parts = ((q0, q1), (k0, k1),
                 (jnp.where(lo, v0, 1.0), jnp.where(lo, v1, 1.0)),
                 (jnp.where(lo, 1.0, v0), jnp.where(lo, 1.0, v1)))
        for part in range(B_PARTS):
            for pr in range(PAIRS_B):
                val = parts[part][pr]
                if dil == 1:
                    b_refs[g][0, part, pr, 0] = val.astype(BF16)
                else:
                    slot = part * PAIRS_B + pr
                    scr_ref[slot] = val
                    for r in range(dil):
                        b_refs[g][0, part, pr, r] = scr_ref[
                            slot, pl.ds(r, tm // dil, stride=dil), :].astype(BF16)
        kt = jnp.concatenate([k0, k1], axis=-1).T
        vt = jnp.concatenate([v0, v1], axis=-1).T
        if g == 2:
            pb3_ref[0, 0] = kt
            pb3_ref[0, 1] = vt
        else:
            st_ref = pb1_ref if g == 0 else pb2_ref
            rows = st_ref.shape[3]

            @pl.when(i == last)
            def _(kt=kt, vt=vt, st_ref=st_ref, rows=rows):
                st_ref[0, 0] = kt[:, tm - rows:]
                st_ref[0, 1] = vt[:, tm - rows:]

    for c in range(2 * D_MODEL // (2 * LANES)):
        z0, z1 = take(OFF_GATE + c * 2 * LANES)
        gate_ref[0, :, c * 2 * LANES:c * 2 * LANES + LANES] = _sigmoid(z0)
        gate_ref[0, :, c * 2 * LANES + LANES:(c + 1) * 2 * LANES] = _sigmoid(z1)


def _proj_prompt(x, g1, w_in, tables, tm):
    bsz, seq, _ = x.shape
    nt = seq // tm
    cos, sina, sinb = tables
    const2 = lambda b, i: (0, 0)
    tab_spec = pl.BlockSpec((tm, LANES), lambda b, i: (i, 0))
    out_shape = [
        jax.ShapeDtypeStruct((bsz, PAIRS_A, seq, LANES), BF16),
        jax.ShapeDtypeStruct((bsz, N_KV_A, seq, LANES), BF16),
        jax.ShapeDtypeStruct((bsz, N_KV_A, 2, seq, LANES), BF16),
    ]
    out_specs = [
        pl.BlockSpec((1, PAIRS_A, tm, LANES), lambda b, i: (b, 0, i, 0)),
        pl.BlockSpec((1, N_KV_A, tm, LANES), lambda b, i: (b, 0, i, 0)),
        pl.BlockSpec((1, N_KV_A, 2, tm, LANES), lambda b, i: (b, 0, 0, i, 0)),
    ]
    for _, dil in DILATED_PAIRS:
        out_shape.append(jax.ShapeDtypeStruct((bsz, B_PARTS, PAIRS_B, dil, seq // dil, LANES), BF16))
        out_specs.append(pl.BlockSpec((1, B_PARTS, PAIRS_B, dil, tm // dil, LANES),
                                      lambda b, i: (b, 0, 0, 0, i, 0)))
    out_shape.append(jax.ShapeDtypeStruct((bsz, seq, 2 * D_MODEL), F32))
    out_specs.append(pl.BlockSpec((1, tm, 2 * D_MODEL), lambda b, i: (b, i, 0)))
    out_shape.append(jax.ShapeDtypeStruct((bsz, 2, KV_A, min(WINDOW_A, seq)), F32))
    out_specs.append(pl.BlockSpec((1, 2, KV_A, min(WINDOW_A, seq)), lambda b, i: (b, 0, 0, 0)))
    for g, (win, _) in enumerate(DILATED_PAIRS):
        rows = min(win, seq)
        out_shape.append(jax.ShapeDtypeStruct((bsz, 2, GRP_B, rows), F32))
        if g == 2:
            out_specs.append(pl.BlockSpec((1, 2, GRP_B, tm), lambda b, i: (b, 0, 0, i)))
        else:
            out_specs.append(pl.BlockSpec((1, 2, GRP_B, rows), lambda b, i: (b, 0, 0, 0)))
    return pl.pallas_call(
        _proj_prompt_kernel,
        grid=(bsz, nt),
        in_specs=[
            pl.BlockSpec((1, tm, D_MODEL), lambda b, i: (b, i, 0)),
            pl.BlockSpec((1, D_MODEL), const2),
            pl.BlockSpec(w_in.shape, const2, pipeline_mode=pl.Buffered(1)),
            tab_spec, tab_spec, tab_spec,
        ],
        out_specs=out_specs,
        out_shape=out_shape,
        scratch_shapes=[pltpu.VMEM((B_PARTS * PAIRS_B, tm, LANES), F32)],
        compiler_params=pltpu.CompilerParams(
            dimension_semantics=("arbitrary", "arbitrary"), vmem_limit_bytes=VMEM_LIMIT),
        name="proj_prompt",
    )(x, g1, w_in, cos, sina, sinb)


def _band_biases():
    qi = lax.broadcasted_iota(jnp.int32, (QBLK, 2 * QBLK), 0)
    ki = lax.broadcasted_iota(jnp.int32, (QBLK, 2 * QBLK), 1)
    band = jnp.where((ki >= qi) & (ki <= qi + SPAN), 0.0, NEG_INF).astype(F32)
    first = jnp.where(ki <= qi, 0.0, NEG_INF).astype(F32)
    return first, band, first[:, :QBLK]


def _head_scores(q2, k, bias, lo, hh):
    qm = jnp.where(lo if hh == 0 else jnp.logical_not(lo), q2, jnp.zeros_like(q2))
    s = lax.dot_general(qm, k, (((1,), (1,)), ((), ())), preferred_element_type=F32)
    return s + bias


def _pair_pv(e0, e1, v_lo, v_hi, lo):
    pv0 = jnp.dot(e0, v_lo, preferred_element_type=F32)
    pv1 = jnp.dot(e1, v_hi, preferred_element_type=F32)
    num = jnp.where(lo, pv0, pv1)
    den = pltpu.roll(jnp.where(lo, pv1, pv0), HEAD_DIM, 1)
    return num, den


def _blk(j):
    return j * QBLK if isinstance(j, int) else pl.multiple_of(j * QBLK, QBLK)


def _win(j):
    return max(j - 1, 0) * QBLK if isinstance(j, int) else pl.multiple_of(jnp.maximum(j - 1, 0) * QBLK, QBLK)


def _attn_a_kernel(sink_ref, qa_ref, kd_ref, va_ref, *rest):
    nw = (len(rest) - 5) // 2
    oa_ref = rest[nw]
    s_scr, e_scr, m_scr, t_scr = rest[2 * nw + 1:]
    for src, dst in zip(rest[:nw], rest[nw + 1:2 * nw + 1]):
        dst[...] = src[...].astype(BF16)
    seq = qa_ref.shape[2]
    nblk = seq // QBLK
    first, band, _ = _band_biases()
    lo = lax.broadcasted_iota(jnp.int32, (QBLK, LANES), 1) < HEAD_DIM
    pairs_per_kv = PAIRS_A // N_KV_A

    def stage_qk(jb, slot, bias):
        for kv in range(N_KV_A):
            k = kd_ref[0, kv, pl.ds(_win(jb), 2 * QBLK), :]
            for pr in range(pairs_per_kv):
                pair = kv * pairs_per_kv + pr
                q2 = qa_ref[0, pair, pl.ds(_blk(jb), QBLK), :]
                for hh in range(2):
                    h = 2 * pair + hh
                    s = _head_scores(q2, k, bias, lo, hh)
                    s_scr[slot, h] = s
                    m = jnp.broadcast_to(jnp.max(s, axis=1, keepdims=True), (QBLK, LANES))
                    m_scr[slot, h] = jnp.maximum(m, sink_ref[h])

    def stage_sm(slot):
        for pair in range(PAIRS_A):
            ms = []
            for hh in range(2):
                h = 2 * pair + hh
                m = m_scr[slot, h]
                e_scr[slot, h] = jnp.exp(s_scr[slot, h] - jnp.tile(m, (1, 2))).astype(BF16)
                ms.append(m)
            sink_pair = jnp.where(lo, sink_ref[2 * pair], sink_ref[2 * pair + 1])
            t_scr[slot, pair] = jnp.exp(sink_pair - jnp.where(lo, ms[0], ms[1]))

    def stage_pv(jb, slot):
        for kv in range(N_KV_A):
            v_lo = va_ref[0, kv, 0, pl.ds(_win(jb), 2 * QBLK), :]
            v_hi = va_ref[0, kv, 1, pl.ds(_win(jb), 2 * QBLK), :]
            for pr in range(pairs_per_kv):
                pair = kv * pairs_per_kv + pr
                num, den = _pair_pv(e_scr[slot, 2 * pair], e_scr[slot, 2 * pair + 1], v_lo, v_hi, lo)
                den = den + t_scr[slot, pair]
                oa_ref[0, pair, pl.ds(_blk(jb), QBLK), :] = (num / den).astype(BF16)

    stage_qk(0, 0, first)
    stage_qk(1, 1, band)
    stage_sm(0)

    def body(i, carry):
        t = 2 + 2 * i
        stage_pv(t - 2, 0)
        stage_qk(t, 0, band)
        stage_sm(1)
        stage_pv(t - 1, 1)
        stage_qk(t + 1, 1, band)
        stage_sm(0)
        return carry

    lax.fori_loop(0, (nblk - 2) // 2, body, 0)
    stage_sm(1)
    stage_pv(nblk - 2, 0)
    stage_pv(nblk - 1, 1)


def _attn_a(sinks, qa, kad, vad, weights):
    bsz, _, seq, _ = qa.shape
    assert (seq // QBLK) % 2 == 0 and seq >= 2 * QBLK
    bf16_rows = 16
    assert all(w.shape[0] % (bsz * bf16_rows) == 0 for w in weights)
    wspecs = [pl.BlockSpec((w.shape[0] // bsz, w.shape[1]), lambda b: (b, 0)) for w in weights]
    return pl.pallas_call(
        _attn_a_kernel,
        grid=(bsz,),
        in_specs=[
            pl.BlockSpec(memory_space=pltpu.SMEM),
            pl.BlockSpec((1, PAIRS_A, seq, LANES), lambda b: (b, 0, 0, 0)),
            pl.BlockSpec((1, N_KV_A, seq, LANES), lambda b: (b, 0, 0, 0)),
            pl.BlockSpec((1, N_KV_A, 2, seq, LANES), lambda b: (b, 0, 0, 0, 0)),
        ] + wspecs,
        out_specs=[pl.BlockSpec((1, PAIRS_A, seq, LANES), lambda b: (b, 0, 0, 0))] + wspecs,
        out_shape=[jax.ShapeDtypeStruct((bsz, PAIRS_A, seq, LANES), BF16)] + [
            jax.ShapeDtypeStruct(w.shape, BF16) for w in weights],
        scratch_shapes=[pltpu.VMEM((2, N_HEADS_A, QBLK, 2 * QBLK), F32),
                        pltpu.VMEM((2, N_HEADS_A, QBLK, 2 * QBLK), BF16),
                        pltpu.VMEM((2, N_HEADS_A, QBLK, LANES), F32),
                        pltpu.VMEM((2, PAIRS_A, QBLK, LANES), F32)],
        compiler_params=pltpu.CompilerParams(
            dimension_semantics=("arbitrary",), vmem_limit_bytes=VMEM_LIMIT),
        name="attn_a",
    )(sinks, qa, kad, vad, *weights)


def _attn_b_kernel(b1_ref, b2_ref, b3_ref, ob_ref, o_scr, lse_scr, s12_scr, s3_scr, e12_scr, e3_scr, m_scr,
                   mp_scr):
    seq = ob_ref.shape[1]
    nunit = seq // QBLK
    first, band, causal = _band_biases()
    lo = lax.broadcasted_iota(jnp.int32, (QBLK, LANES), 1) < HEAD_DIM
    refs = (b1_ref, b2_ref, b3_ref)
    dils = tuple(d for _, d in DILATED_PAIRS)
    nblks = tuple(nunit // d for d in dils)

    def coords(g, u):
        if nblks[g] == 1:
            return u, 0
        if nblks[g] == nunit:
            return 0, u
        return u // nblks[g], u % nblks[g]

    def stage_qk(u, slot):
        for g in range(N_GROUPS_B):
            r, jb = coords(g, u)
            for pr in range(PAIRS_B):
                q2 = refs[g][0, 0, pr, r, pl.ds(_blk(jb), QBLK), :]
                if nblks[g] == 1:
                    k = refs[g][0, 1, pr, r, :, :]
                    bias = causal
                else:
                    k = refs[g][0, 1, pr, r, pl.ds(_win(jb), 2 * QBLK), :]
                    if isinstance(jb, int):
                        bias = first if jb == 0 else band
                    elif nblks[g] == nunit:
                        bias = band
                    else:
                        bias = jnp.where(jb == 0, first, band)
                for hh in range(2):
                    s = _head_scores(q2, k, bias, lo, hh)
                    if nblks[g] == 1:
                        s3_scr[slot, 2 * pr + hh] = s
                    else:
                        s12_scr[slot, g, 2 * pr + hh] = s
                    m_scr[slot, g, 2 * pr + hh] = jnp.broadcast_to(
                        jnp.max(s, axis=1, keepdims=True), (QBLK, LANES))

    def stage_sm(slot):
        for g in range(N_GROUPS_B):
            for h in range(2 * PAIRS_B):
                m = m_scr[slot, g, h]
                if nblks[g] == 1:
                    e3_scr[slot, h] = jnp.exp(s3_scr[slot, h] - m).astype(BF16)
                else:
                    e12_scr[slot, g, h] = jnp.exp(s12_scr[slot, g, h] - jnp.tile(m, (1, 2))).astype(BF16)
            for pr in range(PAIRS_B):
                mp_scr[slot, g, pr] = jnp.where(lo, m_scr[slot, g, 2 * pr], m_scr[slot, g, 2 * pr + 1])

    def stage_pv(u, slot):
        for g in range(N_GROUPS_B):
            r, jb = coords(g, u)
            for pr in range(PAIRS_B):
                if nblks[g] == 1:
                    v_lo = refs[g][0, 2, pr, r, :, :]
                    v_hi = refs[g][0, 3, pr, r, :, :]
                    e0, e1 = e3_scr[slot, 2 * pr], e3_scr[slot, 2 * pr + 1]
                else:
                    v_lo = refs[g][0, 2, pr, r, pl.ds(_win(jb), 2 * QBLK), :]
                    v_hi = refs[g][0, 3, pr, r, pl.ds(_win(jb), 2 * QBLK), :]
                    e0, e1 = e12_scr[slot, g, 2 * pr], e12_scr[slot, g, 2 * pr + 1]
                num, den = _pair_pv(e0, e1, v_lo, v_hi, lo)
                row0 = _blk(jb) * dils[g] + r
                idx = pl.ds(row0, QBLK) if dils[g] == 1 else pl.ds(row0, QBLK, stride=dils[g])
                o_scr[g, pr, idx, :] = num / den
                lse_scr[g, pr, idx, :] = mp_scr[slot, g, pr] + jnp.log(den)

    stage_qk(0, 0)
    stage_qk(1, 1)
    stage_sm(0)

    def body(i, carry):
        u = 2 + 2 * i
        stage_qk(u, 0)
        stage_sm(1)
        stage_pv(u - 2, 0)
        stage_qk(u + 1, 1)
        stage_sm(0)
        stage_pv(u - 1, 1)
        return carry

    lax.fori_loop(0, (nunit - 2) // 2, body, 0)
    stage_sm(1)
    stage_pv(nunit - 2, 0)
    stage_pv(nunit - 1, 1)

    mrows = 2 * QBLK

    def merge(c, carry):
        rows = pl.ds(pl.multiple_of(c * mrows, mrows), mrows)
        for pr in range(PAIRS_B):
            ls = [lse_scr[g, pr, rows, :] for g in range(N_GROUPS_B)]
            mx = jnp.maximum(jnp.maximum(ls[0], ls[1]), ls[2])
            ws = [jnp.exp(l - mx) for l in ls]
            num = ws[0] * o_scr[0, pr, rows, :] + ws[1] * o_scr[1, pr, rows, :] + ws[2] * o_scr[2, pr, rows, :]
            den = ws[0] + ws[1] + ws[2]
            ob_ref[0, rows, pr * LANES:(pr + 1) * LANES] = (num / den).astype(BF16)
        return carry

    lax.fori_loop(0, seq // mrows, merge, 0)


def _attn_b(b1, b2, b3):
    bsz = b1.shape[0]
    seq = b1.shape[4]
    nunit = seq // QBLK
    assert nunit % 2 == 0 and nunit >= 4 and nunit == DILATED_PAIRS[-1][1]
    assert b1.shape[3:5] == (1, seq) and b3.shape[4] == QBLK
    specs = [pl.BlockSpec((1,) + b.shape[1:], lambda b_: (b_, 0, 0, 0, 0, 0)) for b in (b1, b2, b3)]
    heads = 2 * PAIRS_B
    return pl.pallas_call(
        _attn_b_kernel,
        grid=(bsz,),
        in_specs=specs,
        out_specs=pl.BlockSpec((1, seq, GRP_B), lambda b_: (b_, 0, 0)),
        out_shape=jax.ShapeDtypeStruct((bsz, seq, GRP_B), BF16),
        scratch_shapes=[pltpu.VMEM((N_GROUPS_B, PAIRS_B, seq, LANES), F32),
                        pltpu.VMEM((N_GROUPS_B, PAIRS_B, seq, LANES), F32),
                        pltpu.VMEM((2, 2, heads, QBLK, 2 * QBLK), F32),
                        pltpu.VMEM((2, heads, QBLK, QBLK), F32),
                        pltpu.VMEM((2, 2, heads, QBLK, 2 * QBLK), BF16),
                        pltpu.VMEM((2, heads, QBLK, QBLK), BF16),
                        pltpu.VMEM((2, N_GROUPS_B, heads, QBLK, LANES), F32),
                        pltpu.VMEM((2, N_GROUPS_B, PAIRS_B, QBLK, LANES), F32)],
        compiler_params=pltpu.CompilerParams(
            dimension_semantics=("arbitrary",), vmem_limit_bytes=VMEM_LIMIT),
        name="attn_b",
    )(b1, b2, b3)


def _finish_kernel(x_ref, oa_ref, ob_ref, gate_ref, wpa_ref, wpb_ref, wo_ref, g2_ref,
                   wgu_ref, wdn_ref, gf_ref, y_ref):
    d_ff = wdn_ref.shape[0]
    oa = jnp.concatenate([oa_ref[0, p] for p in range(PAIRS_A)], axis=-1)
    ma = jnp.dot(oa, wpa_ref[...], preferred_element_type=F32)
    mb = jnp.dot(ob_ref[0], wpb_ref[...], preferred_element_type=F32)
    m = gate_ref[0, :, :D_MODEL] * ma + gate_ref[0, :, D_MODEL:] * mb
    x1 = x_ref[0] + jnp.dot(m.astype(BF16), wo_ref[...], preferred_element_type=F32)
    xn = _rms(x1, g2_ref[...]).astype(BF16)
    gu = jnp.dot(xn, wgu_ref[...], preferred_element_type=F32)
    gg = gu[:, :d_ff]
    act = (gg * _sigmoid(gg) * gu[:, d_ff:]).astype(BF16)
    y_ref[0] = _rms(x1 + jnp.dot(act, wdn_ref[...], preferred_element_type=F32), gf_ref[...])


def _finish(x, oa, ob, gate, w_pa, w_pb, w_o, g2, w_gu, w_down, gf, tm):
    bsz, seq, _ = x.shape
    const2 = lambda b, i: (0, 0)
    wspec = lambda w: pl.BlockSpec(w.shape, const2, pipeline_mode=pl.Buffered(1))
    return pl.pallas_call(
        _finish_kernel,
        grid=(bsz, seq // tm),
        in_specs=[
            pl.BlockSpec((1, tm, D_MODEL), lambda b, i: (b, i, 0)),
            pl.BlockSpec((1, PAIRS_A, tm, LANES), lambda b, i: (b, 0, i, 0)),
            pl.BlockSpec((1, tm, GRP_B), lambda b, i: (b, i, 0)),
            pl.BlockSpec((1, tm, 2 * D_MODEL), lambda b, i: (b, i, 0)),
            wspec(w_pa), wspec(w_pb), wspec(w_o),
            pl.BlockSpec((1, D_MODEL), const2),
            wspec(w_gu), wspec(w_down),
            pl.BlockSpec((1, D_MODEL), const2),
        ],
        out_specs=pl.BlockSpec((1, tm, D_MODEL), lambda b, i: (b, i, 0)),
        out_shape=jax.ShapeDtypeStruct((bsz, seq, D_MODEL), F32),
        compiler_params=pltpu.CompilerParams(
            dimension_semantics=("arbitrary", "arbitrary"), vmem_limit_bytes=VMEM_LIMIT),
        name="finish",
    )(x, oa, ob, gate, w_pa, w_pb, w_o, g2, w_gu, w_down, gf)


ZS_QA = 0
ZS_QB = ZS_QA + Q_A
ZS_KA = ZS_QB + B_W
ZS_VA = ZS_KA + N_KV_A * GROUP_A * HEAD_DIM
ZS_KB = ZS_VA + N_KV_A * GROUP_A * HEAD_DIM
ZS_VB = ZS_KB + B_W
ZS_W = ZS_VB + B_W
KT_KA = 0
KT_VA = KT_KA + KV_A
KT_KB = KT_VA + KV_A
KT_VB = KT_KB + B_W
KT_W = KT_VB + B_W


def _proj_sample_kernel(x_ref, g_ref, w_ref, cos_ref, sina_ref, sinb_ref, zs_ref, kvt_ref, gate_ref):
    nb = x_ref.shape[0]
    h = _rms(x_ref[...], g_ref[...]).astype(BF16)
    cos = cos_ref[...]
    sina = sina_ref[...]
    sinb = sinb_ref[...]
    lo = lax.broadcasted_iota(jnp.int32, (nb, LANES), 1) < HEAD_DIM

    def mm(off):
        z = jnp.dot(h, w_ref[:, off:off + 2 * LANES], preferred_element_type=F32)
        return z[:, :LANES], z[:, LANES:]

    def rope(z):
        return _rope(z, cos, sina, sinb)

    def put(off, val):
        zs_ref[:, off:off + LANES] = val

    for c in range(PAIRS_A // 2):
        z0, z1 = mm(c * 2 * LANES)
        put(ZS_QA + c * 2 * LANES, rope(z0) * SCALE)
        put(ZS_QA + c * 2 * LANES + LANES, rope(z1) * SCALE)
    for c in range(B_W // (2 * LANES)):
        z0, z1 = mm(OFF_QB + c * 2 * LANES)
        put(ZS_QB + c * 2 * LANES, rope(z0) * SCALE)
        put(ZS_QB + c * 2 * LANES + LANES, rope(z1) * SCALE)

    zk, zv = mm(OFF_KA)
    ka = rope(zk)
    kvt_ref[KT_KA:KT_KA + KV_A, :] = ka.T
    kvt_ref[KT_VA:KT_VA + KV_A, :] = zv.T
    reps = GROUP_A * HEAD_DIM // LANES
    for base, val in ((ZS_KA, ka), (ZS_VA, zv)):
        d0, d1 = _dup_halves(val, lo)
        for kv, dup in enumerate((d0, d1)):
            for c in range(reps):
                put(base + (kv * reps + c) * LANES, dup)

    for c in range(B_W // (2 * LANES)):
        k0, k1 = mm(OFF_KB + c * 2 * LANES)
        v0, v1 = mm(OFF_VB + c * 2 * LANES)
        k0 = rope(k0)
        k1 = rope(k1)
        for j, (kk, vv) in enumerate(((k0, v0), (k1, v1))):
            off = (2 * c + j) * LANES
            put(ZS_KB + off, kk)
            put(ZS_VB + off, vv)
            kvt_ref[KT_KB + off:KT_KB + off + LANES, :] = kk.T
            kvt_ref[KT_VB + off:KT_VB + off + LANES, :] = vv.T

    for c in range(2 * D_MODEL // (2 * LANES)):
        z0, z1 = mm(OFF_GATE + c * 2 * LANES)
        gate_ref[:, c * 2 * LANES:c * 2 * LANES + LANES] = _sigmoid(z0)
        gate_ref[:, c * 2 * LANES + LANES:(c + 1) * 2 * LANES] = _sigmoid(z1)


def _proj_sample(x, g1, w_in, tables):
    nb = x.shape[0]
    cos, sina, sinb = tables
    return pl.pallas_call(
        _proj_sample_kernel,
        out_shape=[
            jax.ShapeDtypeStruct((nb, ZS_W), F32),
            jax.ShapeDtypeStruct((KT_W, nb), F32),
            jax.ShapeDtypeStruct((nb, 2 * D_MODEL), F32),
        ],
        compiler_params=pltpu.CompilerParams(vmem_limit_bytes=VMEM_LIMIT),
        name="proj_sample",
    )(x, g1, w_in, cos, sina, sinb)


SEQS_PER_STEP = 2


def _sample_attn_kernel(zs_ref, kvt_ref, sink_ref, ca_ref, c1_ref, c2_ref, c3_ref,
                        sa_ref, s1_ref, s2_ref, s3_ref, oa_ref, ob_ref):
    for j in range(ca_ref.shape[0]):
        _sample_attn_one(j, zs_ref, kvt_ref, sink_ref, ca_ref, c1_ref, c2_ref, c3_ref,
                         sa_ref, s1_ref, s2_ref, s3_ref, oa_ref, ob_ref)


def _sample_attn_one(j, zs_ref, kvt_ref, sink_ref, ca_ref, c1_ref, c2_ref, c3_ref,
                     sa_ref, s1_ref, s2_ref, s3_ref, oa_ref, ob_ref):
    b = pl.program_id(0) * ca_ref.shape[0] + j
    nb = kvt_ref.shape[1]
    pick = lax.broadcasted_iota(jnp.int32, (1, nb), 1) == b

    def row(off, width):
        return zs_ref[pl.ds(b, 1), off:off + width]

    def new_col(off, width):
        blk = kvt_ref[off:off + width, :]
        return jnp.sum(jnp.where(pick, blk, 0.0), axis=1, keepdims=True)

    def shifted(tile, col):
        w = tile.shape[1]
        lane = lax.broadcasted_iota(jnp.int32, tile.shape, 1)
        return jnp.where(lane == w - 1, col, pltpu.roll(tile, w - 1, 1))

    def head_mask(nh):
        rows = max(nh, 8)
        hid = lax.broadcasted_iota(jnp.int32, (rows, nh * HEAD_DIM), 0)
        lh = lax.broadcasted_iota(jnp.int32, (rows, nh * HEAD_DIM), 1) // HEAD_DIM
        return hid == lh

    def attend(q_row, kt, vt, knew_row, vnew_row, hm, dil, sink_col):
        qm = jnp.where(hm, q_row, 0.0)
        s = jnp.dot(qm.astype(BF16), kt.astype(BF16), preferred_element_type=F32)
        if dil > 1:
            lane = lax.broadcasted_iota(jnp.int32, s.shape, 1)
            s = jnp.where(lane % dil == 0, s, NEG_INF)
        s_new = jnp.sum(qm * knew_row, axis=1, keepdims=True)
        m = jnp.maximum(jnp.max(s, axis=1, keepdims=True), s_new)
        if sink_col is not None:
            m = jnp.maximum(m, sink_col)
        e = jnp.exp(s - m)
        e_new = jnp.exp(s_new - m)
        l = jnp.sum(e, axis=1, keepdims=True) + e_new
        if sink_col is not None:
            l = l + jnp.exp(sink_col - m)
        num = lax.dot_general(e.astype(BF16), vt.astype(BF16), (((1,), (1,)), ((), ())),
                              preferred_element_type=F32)
        return num + e_new * vnew_row, m, l

    hm_a = head_mask(GROUP_A)
    wq = GROUP_A * HEAD_DIM
    for kv in range(N_KV_A):
        kt = ca_ref[j, 0, kv * HEAD_DIM:(kv + 1) * HEAD_DIM, :]
        vt = ca_ref[j, 1, kv * HEAD_DIM:(kv + 1) * HEAD_DIM, :]
        num, m, l = attend(row(ZS_QA + kv * wq, wq),
                           jnp.concatenate([kt] * GROUP_A, axis=0),
                           jnp.concatenate([vt] * GROUP_A, axis=0),
                           row(ZS_KA + kv * wq, wq), row(ZS_VA + kv * wq, wq),
                           hm_a, 1, sink_ref[kv * GROUP_A:(kv + 1) * GROUP_A, :])
        o = jnp.sum(jnp.where(hm_a, num / l, 0.0), axis=0, keepdims=True)
        oa_ref[pl.ds(b, 1), kv * wq:(kv + 1) * wq] = o
    sa_ref[j, 0] = shifted(ca_ref[j, 0], new_col(KT_KA, KV_A))
    sa_ref[j, 1] = shifted(ca_ref[j, 1], new_col(KT_VA, KV_A))

    hm_b = head_mask(HEADS_B)
    nums, ms, ls = [], [], []
    for g, (c_ref, s_ref, (_, dil)) in enumerate(zip((c1_ref, c2_ref, c3_ref), (s1_ref, s2_ref, s3_ref),
                                                      DILATED_PAIRS)):
        kt = c_ref[j, 0]
        vt = c_ref[j, 1]
        num, m, l = attend(row(ZS_QB + g * GRP_B, GRP_B), kt, vt,
                           row(ZS_KB + g * GRP_B, GRP_B), row(ZS_VB + g * GRP_B, GRP_B),
                           hm_b, dil, None)
        nums.append(num)
        ms.append(m)
        ls.append(l)
        s_ref[j, 0] = shifted(kt, new_col(KT_KB + g * GRP_B, GRP_B))
        s_ref[j, 1] = shifted(vt, new_col(KT_VB + g * GRP_B, GRP_B))
    mx = jnp.maximum(jnp.maximum(ms[0], ms[1]), ms[2])
    ws = [jnp.exp(m - mx) for m in ms]
    num = ws[0] * nums[0] + ws[1] * nums[1] + ws[2] * nums[2]
    den = ws[0] * ls[0] + ws[1] * ls[1] + ws[2] * ls[2]
    ob_ref[pl.ds(b, 1), :] = jnp.sum(jnp.where(hm_b, num / den, 0.0), axis=0, keepdims=True)


def _sample_attn(zs, kvt, sinks_col, caches):
    nb = zs.shape[0]
    full2 = lambda a: pl.BlockSpec(a.shape, lambda b: (0, 0))
    cspec = lambda a: pl.BlockSpec((SEQS_PER_STEP,) + a.shape[1:], lambda b: (b, 0, 0, 0))
    return pl.pallas_call(
        _sample_attn_kernel,
        grid=(nb // SEQS_PER_STEP,),
        in_specs=[full2(zs), full2(kvt), full2(sinks_col)] + [cspec(c) for c in caches],
        out_specs=[cspec(c) for c in caches] + [
            pl.BlockSpec((nb, Q_A), lambda b: (0, 0)),
            pl.BlockSpec((nb, GRP_B), lambda b: (0, 0)),
        ],
        out_shape=[jax.ShapeDtypeStruct(c.shape, F32) for c in caches] + [
            jax.ShapeDtypeStruct((nb, Q_A), F32),
            jax.ShapeDtypeStruct((nb, GRP_B), F32),
        ],
        compiler_params=pltpu.CompilerParams(
            dimension_semantics=("arbitrary",), vmem_limit_bytes=VMEM_LIMIT),
        name="sample_attn",
    )(zs, kvt, sinks_col, *caches)


def _rows_minor(c):
    _, bsz, rows, two, nh, hd = c.shape
    return jnp.transpose(c, (0, 1, 3, 4, 5, 2)).reshape(bsz, two, nh * hd, rows)


def _rows_major(c, nh):
    bsz, two, _, rows = c.shape
    return jnp.transpose(c.reshape(1, bsz, two, nh, HEAD_DIM, rows), (0, 1, 5, 2, 3, 4))


def kernel(x_prompt, x_sample, cache_a_kv, cache_b1_kv, cache_b2_kv, cache_b3_kv, norm1_g, w_in, sinks,
           w_pa, w_pb, w_o, norm2_g, w_gu, w_down, final_norm_g):
    bsz, seq, _ = x_prompt.shape
    nb, dec_seq, _ = x_sample.shape
    assert dec_seq == 1 and norm1_g.shape[0] == 1
    assert cache_a_kv.shape[2] == WINDOW_A
    assert tuple(c.shape[2] for c in (cache_b1_kv, cache_b2_kv, cache_b3_kv)) == tuple(
        w for w, _ in DILATED_PAIRS)

    w_in_b = w_in[0].astype(BF16)
    gf = final_norm_g.reshape(1, D_MODEL)

    tables_p = _rope_tables(jnp.arange(seq, dtype=jnp.int32))
    (qa, kad, vad, b1, b2, b3, gate, pa, pb1, pb2, pb3) = _proj_prompt(
        x_prompt, norm1_g, w_in_b, tables_p, TM_PROJ)
    oa, w_pa_b, w_pb_b, w_o_b, w_gu_b, w_dn_b = _attn_a(
        sinks[0], qa, kad, vad, (w_pa[0], w_pb[0], w_o[0], w_gu[0], w_down[0]))
    ob = _attn_b(b1, b2, b3)
    y_prompt = _finish(x_prompt, oa, ob, gate, w_pa_b, w_pb_b, w_o_b, norm2_g, w_gu_b, w_dn_b, gf, TM_FINISH)

    xs = x_sample.reshape(nb, D_MODEL)
    tables_s = _rope_tables(jnp.full((nb,), PAST_LEN, dtype=jnp.int32))
    zs, kvt, gate_s = _proj_sample(xs, norm1_g, w_in_b, tables_s)
    caches = [_rows_minor(c) for c in (cache_a_kv, cache_b1_kv, cache_b2_kv, cache_b3_kv)]
    sa, sb1, sb2, sb3, oa_s, ob_s = _sample_attn(zs, kvt, sinks[0].reshape(N_HEADS_A, 1), caches)
    oa_s = jnp.transpose(oa_s.reshape(nb, PAIRS_A, LANES), (1, 0, 2))[None].astype(BF16)
    y_sample = _finish(xs[None], oa_s, ob_s[None].astype(BF16), gate_s[None], w_pa_b, w_pb_b, w_o_b,
                       norm2_g, w_gu_b, w_dn_b, gf, nb)

    return (y_prompt, y_sample.reshape(nb, 1, D_MODEL),
            _rows_major(pa, N_KV_A), _rows_major(pb1, HEADS_B), _rows_major(pb2, HEADS_B),
            _rows_major(pb3, HEADS_B),
            _rows_major(sa, N_KV_A), _rows_major(sb1, HEADS_B), _rows_major(sb2, HEADS_B),
            _rows_major(sb3, HEADS_B))
```

```python
import jax
import jax.numpy as jnp
from jax import lax
from jax.experimental import pallas as pl
from jax.experimental.pallas import tpu as pltpu

F32 = jnp.float32
BF16 = jnp.bfloat16

D_MODEL = 1024
HEAD_DIM = 64
HALF = HEAD_DIM // 2
N_HEADS_A = 16
N_KV_A = 2
GROUP_A = N_HEADS_A // N_KV_A
WINDOW_A = 128
DILATED_PAIRS = ((128, 1), (512, 4), (2048, 16))
N_GROUPS_B = 3
HEADS_B = 4
SPAN = 128
PAST_LEN = 16384
ROPE_THETA = 10000.0
NORM_EPS = 1e-6
NEG_INF = -1e30
SCALE = HEAD_DIM ** -0.5

LANES = 128
PAIRS_A = N_HEADS_A // 2
PAIRS_B = HEADS_B // 2
Q_A = N_HEADS_A * HEAD_DIM
KV_A = N_KV_A * HEAD_DIM
GRP_B = HEADS_B * HEAD_DIM
B_W = N_GROUPS_B * GRP_B
OFF_KA = Q_A
OFF_VA = OFF_KA + KV_A
OFF_QB = OFF_VA + KV_A
OFF_KB = OFF_QB + B_W
OFF_VB = OFF_KB + B_W
OFF_GATE = OFF_VB + B_W
QBLK = 128
B_PARTS = 4

VMEM_LIMIT = 56 * 1024 * 1024
TM_PROJ = 512
TM_FINISH = 512


def _rms(x, g):
    return x * lax.rsqrt(jnp.mean(x * x, axis=-1, keepdims=True) + NORM_EPS) * g


def _rope(z, cos, sina, sinb):
    return z * cos + pltpu.roll(z, LANES - HALF, 1) * sina + pltpu.roll(z, HALF, 1) * sinb


def _sigmoid(z):
    return 1.0 / (1.0 + jnp.exp(-z))


def _dup_halves(z, lo):
    r = pltpu.roll(z, HEAD_DIM, 1)
    return jnp.where(lo, z, r), jnp.where(lo, r, z)


def _rope_tables(pos):
    inv = 1.0 / (ROPE_THETA ** (jnp.arange(HALF, dtype=F32) / HALF))
    ang = pos.astype(F32)[:, None] * inv[None, :]
    ang = jnp.concatenate([ang, ang, ang, ang], axis=-1)
    first = (jnp.arange(LANES) % HEAD_DIM) < HALF
    cos = jnp.cos(ang)
    sin = jnp.sin(ang)
    return cos, jnp.where(first, -sin, 0.0), jnp.where(first, 0.0, sin)


def _proj_prompt_kernel(x_ref, g_ref, w_ref, cos_ref, sina_ref, sinb_ref,
                        qa_ref, kad_ref, vad_ref, b1_ref, b2_ref, b3_ref, gate_ref,
                        pa_ref, pb1_ref, pb2_ref, pb3_ref, scr_ref, scr2_ref):
    tm = x_ref.shape[1]
    i = pl.program_id(1)
    last = pl.num_programs(1) - 1
    h = _rms(x_ref[0], g_ref[...]).astype(BF16)
    cos = cos_ref[...]
    sina = sina_ref[...]
    sinb = sinb_ref[...]
    lo = lax.broadcasted_iota(jnp.int32, (tm, LANES), 1) < HEAD_DIM

    def mm(off):
        z = jnp.dot(h, w_ref[:, off:off + 2 * LANES], preferred_element_type=F32)
        return z[:, :LANES], z[:, LANES:]

    def rope(z):
        return _rope(z, cos, sina, sinb)

    offs = ([c * 2 * LANES for c in range(PAIRS_A // 2)] + [OFF_KA]
            + [o + g * GRP_B for g in range(N_GROUPS_B) for o in (OFF_QB, OFF_KB, OFF_VB)]
            + [OFF_GATE + c * 2 * LANES for c in range(2 * D_MODEL // (2 * LANES))])
    pending = {offs[0]: mm(offs[0])}

    def take(off):
        nxt = offs.index(off) + 1
        if nxt < len(offs):
            pending[offs[nxt]] = mm(offs[nxt])
        return pending.pop(off)

    for c in range(PAIRS_A // 2):
        z0, z1 = take(c * 2 * LANES)
        qa_ref[0, 2 * c] = (rope(z0) * SCALE).astype(BF16)
        qa_ref[0, 2 * c + 1] = (rope(z1) * SCALE).astype(BF16)

    zk, zv = take(OFF_KA)
    ka = rope(zk)
    k0, k1 = _dup_halves(ka, lo)
    v0, v1 = _dup_halves(zv, lo)
    kad_ref[0, 0] = k0.astype(BF16)
    kad_ref[0, 1] = k1.astype(BF16)
    for kv, vd in enumerate((v0, v1)):
        vad_ref[0, kv, 0] = jnp.where(lo, vd, 1.0).astype(BF16)
        vad_ref[0, kv, 1] = jnp.where(lo, 1.0, vd).astype(BF16)

    @pl.when(i == last)
    def _():
        pa_ref[0, 0] = ka[tm - WINDOW_A:, :].T
        pa_ref[0, 1] = zv[tm - WINDOW_A:, :].T

    b_refs = (b1_ref, b2_ref, b3_ref)
    for g, (win, dil) in enumerate(DILATED_PAIRS):
        q0, q1 = take(OFF_QB + g * GRP_B)
        k0, k1 = take(OFF_KB + g * GRP_B)
        v0, v1 = take(OFF_VB + g * GRP_B)
        q0 = rope(q0) * SCALE
        q1 = rope(q1) * SCALE
        k0 = rope(k0)
        k1 = rope(k1)
        parts = ((q0, q1), (k0, k1),
                 (jnp.where(lo, v0, 1.0), jnp.where(lo, v1, 1.0)),
                 (jnp.where(lo, 1.0, v0), jnp.where(lo, 1.0, v1)))
        for part in range(B_PARTS):
            for pr in range(PAIRS_B):
                val = parts[part][pr]
                if dil == 1:
                    b_refs[g][0, part, pr, 0] = val.astype(BF16)
                elif dil == 16:
                    slot = part * PAIRS_B + pr
                    scr_ref[slot] = val
                    q4 = tm // 4
                    for r4 in range(4):
                        scr2_ref[slot, r4 * q4:(r4 + 1) * q4, :] = scr_ref[slot, pl.ds(r4, q4, stride=4), :]
                    for r4 in range(4):
                        for r16 in range(4):
                            b_refs[g][0, part, pr, r4 + 4 * r16] = scr2_ref[
                                slot, pl.ds(r4 * q4 + r16, q4 // 4, stride=4), :].astype(BF16)
                else:
                    slot = part * PAIRS_B + pr
                    scr_ref[slot] = val
                    for r in range(dil):
                        b_refs[g][0, part, pr, r] = scr_ref[
                            slot, pl.ds(r, tm // dil, stride=dil), :].astype(BF16)
        kt = jnp.concatenate([k0, k1], axis=-1).T
        vt = jnp.concatenate([v0, v1], axis=-1).T
        if g == 2:
            pb3_ref[0, 0] = kt
            pb3_ref[0, 1] = vt
        else:
            st_ref = pb1_ref if g == 0 else pb2_ref
            rows = st_ref.shape[3]

            @pl.when(i == last)
            def _(kt=kt, vt=vt, st_ref=st_ref, rows=rows):
                st_ref[0, 0] = kt[:, tm - rows:]
                st_ref[0, 1] = vt[:, tm - rows:]

    for c in range(2 * D_MODEL // (2 * LANES)):
        z0, z1 = take(OFF_GATE + c * 2 * LANES)
        gate_ref[0, :, c * 2 * LANES:c * 2 * LANES + LANES] = _sigmoid(z0)
        gate_ref[0, :, c * 2 * LANES + LANES:(c + 1) * 2 * LANES] = _sigmoid(z1)


def _proj_prompt(x, g1, w_in, tables, tm):
    bsz, seq, _ = x.shape
    nt = seq // tm
    cos, sina, sinb = tables
    const2 = lambda b, i: (0, 0)
    tab_spec = pl.BlockSpec((tm, LANES), lambda b, i: (i, 0))
    out_shape = [
        jax.ShapeDtypeStruct((bsz, PAIRS_A, seq, LANES), BF16),
        jax.ShapeDtypeStruct((bsz, N_KV_A, seq, LANES), BF16),
        jax.ShapeDtypeStruct((bsz, N_KV_A, 2, seq, LANES), BF16),
    ]
    out_specs = [
        pl.BlockSpec((1, PAIRS_A, tm, LANES), lambda b, i: (b, 0, i, 0)),
        pl.BlockSpec((1, N_KV_A, tm, LANES), lambda b, i: (b, 0, i, 0)),
        pl.BlockSpec((1, N_KV_A, 2, tm, LANES), lambda b, i: (b, 0, 0, i, 0)),
    ]
    for _, dil in DILATED_PAIRS:
        out_shape.append(jax.ShapeDtypeStruct((bsz, B_PARTS, PAIRS_B, dil, seq // dil, LANES), BF16))
        out_specs.append(pl.BlockSpec((1, B_PARTS, PAIRS_B, dil, tm // dil, LANES),
                                      lambda b, i: (b, 0, 0, 0, i, 0)))
    out_shape.append(jax.ShapeDtypeStruct((bsz, seq, 2 * D_MODEL), F32))
    out_specs.append(pl.BlockSpec((1, tm, 2 * D_MODEL), lambda b, i: (b, i, 0)))
    out_shape.append(jax.ShapeDtypeStruct((bsz, 2, KV_A, min(WINDOW_A, seq)), F32))
    out_specs.append(pl.BlockSpec((1, 2, KV_A, min(WINDOW_A, seq)), lambda b, i: (b, 0, 0, 0)))
    for g, (win, _) in enumerate(DILATED_PAIRS):
        rows = min(win, seq)
        out_shape.append(jax.ShapeDtypeStruct((bsz, 2, GRP_B, rows), F32))
        if g == 2:
            out_specs.append(pl.BlockSpec((1, 2, GRP_B, tm), lambda b, i: (b, 0, 0, i)))
        else:
            out_specs.append(pl.BlockSpec((1, 2, GRP_B, rows), lambda b, i: (b, 0, 0, 0)))
    return pl.pallas_call(
        _proj_prompt_kernel,
        grid=(bsz, nt),
        in_specs=[
            pl.BlockSpec((1, tm, D_MODEL), lambda b, i: (b, i, 0)),
            pl.BlockSpec((1, D_MODEL), const2),
            pl.BlockSpec(w_in.shape, const2, pipeline_mode=pl.Buffered(1)),
            tab_spec, tab_spec, tab_spec,
        ],
        out_specs=out_specs,
        out_shape=out_shape,
        scratch_shapes=[pltpu.VMEM((B_PARTS * PAIRS_B, tm, LANES), F32),
                        pltpu.VMEM((B_PARTS * PAIRS_B, tm, LANES), F32)],
        compiler_params=pltpu.CompilerParams(
            dimension_semantics=("arbitrary", "arbitrary"), vmem_limit_bytes=VMEM_LIMIT),
        name="proj_prompt",
    )(x, g1, w_in, cos, sina, sinb)


def _band_biases():
    qi = lax.broadcasted_iota(jnp.int32, (QBLK, 2 * QBLK), 0)
    ki = lax.broadcasted_iota(jnp.int32, (QBLK, 2 * QBLK), 1)
    band = jnp.where((ki >= qi) & (ki <= qi + SPAN), 0.0, NEG_INF).astype(F32)
    first = jnp.where(ki <= qi, 0.0, NEG_INF).astype(F32)
    return first, band, first[:, :QBLK]


def _head_scores(q2, k, bias, lo, hh):
    qm = jnp.where(lo if hh == 0 else jnp.logical_not(lo), q2, jnp.zeros_like(q2))
    s = lax.dot_general(qm, k, (((1,), (1,)), ((), ())), preferred_element_type=F32)
    return s + bias


def _pair_pv(e0, e1, v_lo, v_hi, lo):
    pv0 = jnp.dot(e0, v_lo, preferred_element_type=F32)
    pv1 = jnp.dot(e1, v_hi, preferred_element_type=F32)
    num = jnp.where(lo, pv0, pv1)
    den = pltpu.roll(jnp.where(lo, pv1, pv0), HEAD_DIM, 1)
    return num, den


def _blk(j):
    return j * QBLK if isinstance(j, int) else pl.multiple_of(j * QBLK, QBLK)


def _win(j):
    return max(j - 1, 0) * QBLK if isinstance(j, int) else pl.multiple_of(jnp.maximum(j - 1, 0) * QBLK, QBLK)


def _attn_a_kernel(sink_ref, qa_ref, kd_ref, va_ref, *rest):
    nw = (len(rest) - 5) // 2
    oa_ref = rest[nw]
    s_scr, e_scr, m_scr, t_scr = rest[2 * nw + 1:]
    for src, dst in zip(rest[:nw], rest[nw + 1:2 * nw + 1]):
        dst[...] = src[...].astype(BF16)
    seq = qa_ref.shape[2]
    nblk = seq // QBLK
    first, band, _ = _band_biases()
    lo = lax.broadcasted_iota(jnp.int32, (QBLK, LANES), 1) < HEAD_DIM
    pairs_per_kv = PAIRS_A // N_KV_A

    def stage_qk(jb, slot, bias):
        for kv in range(N_KV_A):
            k = kd_ref[0, kv, pl.ds(_win(jb), 2 * QBLK), :]
            for pr in range(pairs_per_kv):
                pair = kv * pairs_per_kv + pr
                q2 = qa_ref[0, pair, pl.ds(_blk(jb), QBLK), :]
                for hh in range(2):
                    h = 2 * pair + hh
                    s = _head_scores(q2, k, bias, lo, hh)
                    s_scr[slot, h] = s
                    m = jnp.broadcast_to(jnp.max(s, axis=1, keepdims=True), (QBLK, LANES))
                    m_scr[slot, h] = jnp.maximum(m, sink_ref[h])

    def stage_sm(slot):
        for pair in range(PAIRS_A):
            ms = []
            for hh in range(2):
                h = 2 * pair + hh
                m = m_scr[slot, h]
                e_scr[slot, h] = jnp.exp(s_scr[slot, h] - jnp.tile(m, (1, 2))).astype(BF16)
                ms.append(m)
            sink_pair = jnp.where(lo, sink_ref[2 * pair], sink_ref[2 * pair + 1])
            t_scr[slot, pair] = jnp.exp(sink_pair - jnp.where(lo, ms[0], ms[1]))

    def stage_pv(jb, slot):
        for kv in range(N_KV_A):
            v_lo = va_ref[0, kv, 0, pl.ds(_win(jb), 2 * QBLK), :]
            v_hi = va_ref[0, kv, 1, pl.ds(_win(jb), 2 * QBLK), :]
            for pr in range(pairs_per_kv):
                pair = kv * pairs_per_kv + pr
                num, den = _pair_pv(e_scr[slot, 2 * pair], e_scr[slot, 2 * pair + 1], v_lo, v_hi, lo)
                den = den + t_scr[slot, pair]
                oa_ref[0, pair, pl.ds(_blk(jb), QBLK), :] = (num / den).astype(BF16)

    stage_qk(0, 0, first)
    stage_qk(1, 1, band)
    stage_sm(0)

    def body(i, carry):
        t = 2 + 2 * i
        stage_pv(t - 2, 0)
        stage_qk(t, 0, band)
        stage_sm(1)
        stage_pv(t - 1, 1)
        stage_qk(t + 1, 1, band)
        stage_sm(0)
        return carry

    lax.fori_loop(0, (nblk - 2) // 2, body, 0)
    stage_sm(1)
    stage_pv(nblk - 2, 0)
    stage_pv(nblk - 1, 1)


def _attn_a(sinks, qa, kad, vad, weights):
    bsz, _, seq, _ = qa.shape
    assert (seq // QBLK) % 2 == 0 and seq >= 2 * QBLK
    bf16_rows = 16
    assert all(w.shape[0] % (bsz * bf16_rows) == 0 for w in weights)
    wspecs = [pl.BlockSpec((w.shape[0] // bsz, w.shape[1]), lambda b: (b, 0)) for w in weights]
    return pl.pallas_call(
        _attn_a_kernel,
        grid=(bsz,),
        in_specs=[
            pl.BlockSpec(memory_space=pltpu.SMEM),
            pl.BlockSpec((1, PAIRS_A, seq, LANES), lambda b: (b, 0, 0, 0)),
            pl.BlockSpec((1, N_KV_A, seq, LANES), lambda b: (b, 0, 0, 0)),
            pl.BlockSpec((1, N_KV_A, 2, seq, LANES), lambda b: (b, 0, 0, 0, 0)),
        ] + wspecs,
        out_specs=[pl.BlockSpec((1, PAIRS_A, seq, LANES), lambda b: (b, 0, 0, 0))] + wspecs,
        out_shape=[jax.ShapeDtypeStruct((bsz, PAIRS_A, seq, LANES), BF16)] + [
            jax.ShapeDtypeStruct(w.shape, BF16) for w in weights],
        scratch_shapes=[pltpu.VMEM((2, N_HEADS_A, QBLK, 2 * QBLK), F32),
                        pltpu.VMEM((2, N_HEADS_A, QBLK, 2 * QBLK), BF16),
                        pltpu.VMEM((2, N_HEADS_A, QBLK, LANES), F32),
                        pltpu.VMEM((2, PAIRS_A, QBLK, LANES), F32)],
        compiler_params=pltpu.CompilerParams(
            dimension_semantics=("arbitrary",), vmem_limit_bytes=VMEM_LIMIT),
        name="attn_a",
    )(sinks, qa, kad, vad, *weights)


def _attn_b_kernel(b1_ref, b2_ref, b3_ref, ob_ref, o_scr, lse_scr, s12_scr, s3_scr, e12_scr, e3_scr, m_scr,
                   mp_scr):
    seq = ob_ref.shape[1]
    nunit = seq // QBLK
    first, band, causal = _band_biases()
    lo = lax.broadcasted_iota(jnp.int32, (QBLK, LANES), 1) < HEAD_DIM
    refs = (b1_ref, b2_ref, b3_ref)
    dils = tuple(d for _, d in DILATED_PAIRS)
    nblks = tuple(nunit // d for d in dils)

    def coords(g, u):
        if nblks[g] == 1:
            return u, 0
        if nblks[g] == nunit:
            return 0, u
        return u // nblks[g], u % nblks[g]

    def stage_qk(u, slot):
        for g in range(N_GROUPS_B):
            r, jb = coords(g, u)
            for pr in range(PAIRS_B):
                q2 = refs[g][0, 0, pr, r, pl.ds(_blk(jb), QBLK), :]
                if nblks[g] == 1:
                    k = refs[g][0, 1, pr, r, :, :]
                    bias = causal
                else:
                    k = refs[g][0, 1, pr, r, pl.ds(_win(jb), 2 * QBLK), :]
                    if isinstance(jb, int):
                        bias = first if jb == 0 else band
                    elif nblks[g] == nunit:
                        bias = band
                    else:
                        bias = jnp.where(jb == 0, first, band)
                for hh in range(2):
                    s = _head_scores(q2, k, bias, lo, hh)
                    if nblks[g] == 1:
                        s3_scr[slot, 2 * pr + hh] = s
                    else:
                        s12_scr[slot, g, 2 * pr + hh] = s
                    m_scr[slot, g, 2 * pr + hh] = jnp.broadcast_to(
                        jnp.max(s, axis=1, keepdims=True), (QBLK, LANES))

    def stage_sm(slot):
        for g in range(N_GROUPS_B):
            for h in range(2 * PAIRS_B):
                m = m_scr[slot, g, h]
                if nblks[g] == 1:
                    e3_scr[slot, h] = jnp.exp(s3_scr[slot, h] - m).astype(BF16)
                else:
                    e12_scr[slot, g, h] = jnp.exp(s12_scr[slot, g, h] - jnp.tile(m, (1, 2))).astype(BF16)
            for pr in range(PAIRS_B):
                mp_scr[slot, g, pr] = jnp.where(lo, m_scr[slot, g, 2 * pr], m_scr[slot, g, 2 * pr + 1])

    def stage_pv(u, slot):
        for g in range(N_GROUPS_B):
            r, jb = coords(g, u)
            for pr in range(PAIRS_B):
                if nblks[g] == 1:
                    v_lo = refs[g][0, 2, pr, r, :, :]
                    v_hi = refs[g][0, 3, pr, r, :, :]
                    e0, e1 = e3_scr[slot, 2 * pr], e3_scr[slot, 2 * pr + 1]
                else:
                    v_lo = refs[g][0, 2, pr, r, pl.ds(_win(jb), 2 * QBLK), :]
                    v_hi = refs[g][0, 3, pr, r, pl.ds(_win(jb), 2 * QBLK), :]
                    e0, e1 = e12_scr[slot, g, 2 * pr], e12_scr[slot, g, 2 * pr + 1]
                num, den = _pair_pv(e0, e1, v_lo, v_hi, lo)
                row0 = _blk(jb) * dils[g] + r
                idx = pl.ds(row0, QBLK) if dils[g] == 1 else pl.ds(row0, QBLK, stride=dils[g])
                o_scr[g, pr, idx, :] = num / den
                lse_scr[g, pr, idx, :] = mp_scr[slot, g, pr] + jnp.log(den)

    stage_qk(0, 0)
    stage_qk(1, 1)
    stage_sm(0)

    def body(i, carry):
        u = 2 + 2 * i
        stage_qk(u, 0)
        stage_sm(1)
        stage_pv(u - 2, 0)
        stage_qk(u + 1, 1)
        stage_sm(0)
        stage_pv(u - 1, 1)
        return carry

    lax.fori_loop(0, (nunit - 2) // 2, body, 0)
    stage_sm(1)
    stage_pv(nunit - 2, 0)
    stage_pv(nunit - 1, 1)

    mrows = 2 * QBLK

    def merge(c, carry):
        rows = pl.ds(pl.multiple_of(c * mrows, mrows), mrows)
        for pr in range(PAIRS_B):
            ls = [lse_scr[g, pr, rows, :] for g in range(N_GROUPS_B)]
            mx = jnp.maximum(jnp.maximum(ls[0], ls[1]), ls[2])
            ws = [jnp.exp(l - mx) for l in ls]
            num = ws[0] * o_scr[0, pr, rows, :] + ws[1] * o_scr[1, pr, rows, :] + ws[2] * o_scr[2, pr, rows, :]
            den = ws[0] + ws[1] + ws[2]
            ob_ref[0, rows, pr * LANES:(pr + 1) * LANES] = (num / den).astype(BF16)
        return carry

    lax.fori_loop(0, seq // mrows, merge, 0)


def _attn_b(b1, b2, b3):
    bsz = b1.shape[0]
    seq = b1.shape[4]
    nunit = seq // QBLK
    assert nunit % 2 == 0 and nunit >= 4 and nunit == DILATED_PAIRS[-1][1]
    assert b1.shape[3:5] == (1, seq) and b3.shape[4] == QBLK
    specs = [pl.BlockSpec((1,) + b.shape[1:], lambda b_: (b_, 0, 0, 0, 0, 0)) for b in (b1, b2, b3)]
    heads = 2 * PAIRS_B
    return pl.pallas_call(
        _attn_b_kernel,
        grid=(bsz,),
        in_specs=specs,
        out_specs=pl.BlockSpec((1, seq, GRP_B), lambda b_: (b_, 0, 0)),
        out_shape=jax.ShapeDtypeStruct((bsz, seq, GRP_B), BF16),
        scratch_shapes=[pltpu.VMEM((N_GROUPS_B, PAIRS_B, seq, LANES), F32),
                        pltpu.VMEM((N_GROUPS_B, PAIRS_B, seq, LANES), F32),
                        pltpu.VMEM((2, 2, heads, QBLK, 2 * QBLK), F32),
                        pltpu.VMEM((2, heads, QBLK, QBLK), F32),
                        pltpu.VMEM((2, 2, heads, QBLK, 2 * QBLK), BF16),
                        pltpu.VMEM((2, heads, QBLK, QBLK), BF16),
                        pltpu.VMEM((2, N_GROUPS_B, heads, QBLK, LANES), F32),
                        pltpu.VMEM((2, N_GROUPS_B, PAIRS_B, QBLK, LANES), F32)],
        compiler_params=pltpu.CompilerParams(
            dimension_semantics=("arbitrary",), vmem_limit_bytes=VMEM_LIMIT),
        name="attn_b",
    )(b1, b2, b3)


def _finish_kernel(x_ref, oa_ref, ob_ref, gate_ref, wpa_ref, wpb_ref, wo_ref, g2_ref,
                   wgu_ref, wdn_ref, gf_ref, y_ref):
    d_ff = wdn_ref.shape[0]
    oa = jnp.concatenate([oa_ref[0, p] for p in range(PAIRS_A)], axis=-1)
    ma = jnp.dot(oa, wpa_ref[...], preferred_element_type=F32)
    mb = jnp.dot(ob_ref[0], wpb_ref[...], preferred_element_type=F32)
    m = gate_ref[0, :, :D_MODEL] * ma + gate_ref[0, :, D_MODEL:] * mb
    x1 = x_ref[0] + jnp.dot(m.astype(BF16), wo_ref[...], preferred_element_type=F32)
    xn = _rms(x1, g2_ref[...]).astype(BF16)
    gu = jnp.dot(xn, wgu_ref[...], preferred_element_type=F32)
    gg = gu[:, :d_ff]
    act = (gg * _sigmoid(gg) * gu[:, d_ff:]).astype(BF16)
    y_ref[0] = _rms(x1 + jnp.dot(act, wdn_ref[...], preferred_element_type=F32), gf_ref[...])


def _finish(x, oa, ob, gate, w_pa, w_pb, w_o, g2, w_gu, w_down, gf, tm):
    bsz, seq, _ = x.shape
    const2 = lambda b, i: (0, 0)
    wspec = lambda w: pl.BlockSpec(w.shape, const2, pipeline_mode=pl.Buffered(1))
    return pl.pallas_call(
        _finish_kernel,
        grid=(bsz, seq // tm),
        in_specs=[
            pl.BlockSpec((1, tm, D_MODEL), lambda b, i: (b, i, 0)),
            pl.BlockSpec((1, PAIRS_A, tm, LANES), lambda b, i: (b, 0, i, 0)),
            pl.BlockSpec((1, tm, GRP_B), lambda b, i: (b, i, 0)),
            pl.BlockSpec((1, tm, 2 * D_MODEL), lambda b, i: (b, i, 0)),
            wspec(w_pa), wspec(w_pb), wspec(w_o),
            pl.BlockSpec((1, D_MODEL), const2),
            wspec(w_gu), wspec(w_down),
            pl.BlockSpec((1, D_MODEL), const2),
        ],
        out_specs=pl.BlockSpec((1, tm, D_MODEL), lambda b, i: (b, i, 0)),
        out_shape=jax.ShapeDtypeStruct((bsz, seq, D_MODEL), F32),
        compiler_params=pltpu.CompilerParams(
            dimension_semantics=("arbitrary", "arbitrary"), vmem_limit_bytes=VMEM_LIMIT),
        name="finish",
    )(x, oa, ob, gate, w_pa, w_pb, w_o, g2, w_gu, w_down, gf)


ZS_QA = 0
ZS_QB = ZS_QA + Q_A
ZS_KA = ZS_QB + B_W
ZS_VA = ZS_KA + N_KV_A * GROUP_A * HEAD_DIM
ZS_KB = ZS_VA + N_KV_A * GROUP_A * HEAD_DIM
ZS_VB = ZS_KB + B_W
ZS_W = ZS_VB + B_W
KT_KA = 0
KT_VA = KT_KA + KV_A
KT_KB = KT_VA + KV_A
KT_VB = KT_KB + B_W
KT_W = KT_VB + B_W


def _proj_sample_kernel(x_ref, g_ref, w_ref, cos_ref, sina_ref, sinb_ref, zs_ref, kvt_ref, gate_ref):
    nb = x_ref.shape[0]
    h = _rms(x_ref[...], g_ref[...]).astype(BF16)
    cos = cos_ref[...]
    sina = sina_ref[...]
    sinb = sinb_ref[...]
    lo = lax.broadcasted_iota(jnp.int32, (nb, LANES), 1) < HEAD_DIM

    def mm(off):
        z = jnp.dot(h, w_ref[:, off:off + 2 * LANES], preferred_element_type=F32)
        return z[:, :LANES], z[:, LANES:]

    def rope(z):
        return _rope(z, cos, sina, sinb)

    def put(off, val):
        zs_ref[:, off:off + LANES] = val

    for c in range(PAIRS_A // 2):
        z0, z1 = mm(c * 2 * LANES)
        put(ZS_QA + c * 2 * LANES, rope(z0) * SCALE)
        put(ZS_QA + c * 2 * LANES + LANES, rope(z1) * SCALE)
    for c in range(B_W // (2 * LANES)):
        z0, z1 = mm(OFF_QB + c * 2 * LANES)
        put(ZS_QB + c * 2 * LANES, rope(z0) * SCALE)
        put(ZS_QB + c * 2 * LANES + LANES, rope(z1) * SCALE)

    zk, zv = mm(OFF_KA)
    ka = rope(zk)
    kvt_ref[KT_KA:KT_KA + KV_A, :] = ka.T
    kvt_ref[KT_VA:KT_VA + KV_A, :] = zv.T
    reps = GROUP_A * HEAD_DIM // LANES
    for base, val in ((ZS_KA, ka), (ZS_VA, zv)):
        d0, d1 = _dup_halves(val, lo)
        for kv, dup in enumerate((d0, d1)):
            for c in range(reps):
                put(base + (kv * reps + c) * LANES, dup)

    for c in range(B_W // (2 * LANES)):
        k0, k1 = mm(OFF_KB + c * 2 * LANES)
        v0, v1 = mm(OFF_VB + c * 2 * LANES)
        k0 = rope(k0)
        k1 = rope(k1)
        for j, (kk, vv) in enumerate(((k0, v0), (k1, v1))):
            off = (2 * c + j) * LANES
            put(ZS_KB + off, kk)
            put(ZS_VB + off, vv)
            kvt_ref[KT_KB + off:KT_KB + off + LANES, :] = kk.T
            kvt_ref[KT_VB + off:KT_VB + off + LANES, :] = vv.T

    for c in range(2 * D_MODEL // (2 * LANES)):
        z0, z1 = mm(OFF_GATE + c * 2 * LANES)
        gate_ref[:, c * 2 * LANES:c * 2 * LANES + LANES] = _sigmoid(z0)
        gate_ref[:, c * 2 * LANES + LANES:(c + 1) * 2 * LANES] = _sigmoid(z1)


def _proj_sample(x, g1, w_in, tables):
    nb = x.shape[0]
    cos, sina, sinb = tables
    return pl.pallas_call(
        _proj_sample_kernel,
        out_shape=[
            jax.ShapeDtypeStruct((nb, ZS_W), F32),
            jax.ShapeDtypeStruct((KT_W, nb), F32),
            jax.ShapeDtypeStruct((nb, 2 * D_MODEL), F32),
        ],
        compiler_params=pltpu.CompilerParams(vmem_limit_bytes=VMEM_LIMIT),
        name="proj_sample",
    )(x, g1, w_in, cos, sina, sinb)


SEQS_PER_STEP = 2


def _sample_attn_kernel(zs_ref, kvt_ref, sink_ref, ca_ref, c1_ref, c2_ref, c3_ref,
                        sa_ref, s1_ref, s2_ref, s3_ref, oa_ref, ob_ref):
    for j in range(ca_ref.shape[0]):
        _sample_attn_one(j, zs_ref, kvt_ref, sink_ref, ca_ref, c1_ref, c2_ref, c3_ref,
                         sa_ref, s1_ref, s2_ref, s3_ref, oa_ref, ob_ref)


def _sample_attn_one(j, zs_ref, kvt_ref, sink_ref, ca_ref, c1_ref, c2_ref, c3_ref,
                     sa_ref, s1_ref, s2_ref, s3_ref, oa_ref, ob_ref):
    b = pl.program_id(0) * ca_ref.shape[0] + j
    nb = kvt_ref.shape[1]
    pick = lax.broadcasted_iota(jnp.int32, (1, nb), 1) == b

    def row(off, width):
        return zs_ref[pl.ds(b, 1), off:off + width]

    def new_col(off, width):
        blk = kvt_ref[off:off + width, :]
        return jnp.sum(jnp.where(pick, blk, 0.0), axis=1, keepdims=True)

    def shifted(tile, col):
        w = tile.shape[1]
        lane = lax.broadcasted_iota(jnp.int32, tile.shape, 1)
        return jnp.where(lane == w - 1, col, pltpu.roll(tile, w - 1, 1))

    def head_mask(nh):
        rows = max(nh, 8)
        hid = lax.broadcasted_iota(jnp.int32, (rows, nh * HEAD_DIM), 0)
        lh = lax.broadcasted_iota(jnp.int32, (rows, nh * HEAD_DIM), 1) // HEAD_DIM
        return hid == lh

    def attend(q_row, kt, vt, knew_row, vnew_row, hm, dil, sink_col):
        qm = jnp.where(hm, q_row, 0.0)
        s = jnp.dot(qm.astype(BF16), kt.astype(BF16), preferred_element_type=F32)
        if dil > 1:
            lane = lax.broadcasted_iota(jnp.int32, s.shape, 1)
            s = jnp.where(lane % dil == 0, s, NEG_INF)
        s_new = jnp.sum(qm * knew_row, axis=1, keepdims=True)
        m = jnp.maximum(jnp.max(s, axis=1, keepdims=True), s_new)
        if sink_col is not None:
            m = jnp.maximum(m, sink_col)
        e = jnp.exp(s - m)
        e_new = jnp.exp(s_new - m)
        l = jnp.sum(e, axis=1, keepdims=True) + e_new
        if sink_col is not None:
            l = l + jnp.exp(sink_col - m)
        num = lax.dot_general(e.astype(BF16), vt.astype(BF16), (((1,), (1,)), ((), ())),
                              preferred_element_type=F32)
        return num + e_new * vnew_row, m, l

    hm_a = head_mask(GROUP_A)
    wq = GROUP_A * HEAD_DIM
    for kv in range(N_KV_A):
        kt = ca_ref[j, 0, kv * HEAD_DIM:(kv + 1) * HEAD_DIM, :]
        vt = ca_ref[j, 1, kv * HEAD_DIM:(kv + 1) * HEAD_DIM, :]
        num, m, l = attend(row(ZS_QA + kv * wq, wq),
                           jnp.concatenate([kt] * GROUP_A, axis=0),
                           jnp.concatenate([vt] * GROUP_A, axis=0),
                           row(ZS_KA + kv * wq, wq), row(ZS_VA + kv * wq, wq),
                           hm_a, 1, sink_ref[kv * GROUP_A:(kv + 1) * GROUP_A, :])
        o = jnp.sum(jnp.where(hm_a, num / l, 0.0), axis=0, keepdims=True)
        oa_ref[pl.ds(b, 1), kv * wq:(kv + 1) * wq] = o
    sa_ref[j, 0] = shifted(ca_ref[j, 0], new_col(KT_KA, KV_A))
    sa_ref[j, 1] = shifted(ca_ref[j, 1], new_col(KT_VA, KV_A))

    hm_b = head_mask(HEADS_B)
    nums, ms, ls = [], [], []
    for g, (c_ref, s_ref, (_, dil)) in enumerate(zip((c1_ref, c2_ref, c3_ref), (s1_ref, s2_ref, s3_ref),
                                                      DILATED_PAIRS)):
        kt = c_ref[j, 0]
        vt = c_ref[j, 1]
        num, m, l = attend(row(ZS_QB + g * GRP_B, GRP_B), kt, vt,
                           row(ZS_KB + g * GRP_B, GRP_B), row(ZS_VB + g * GRP_B, GRP_B),
                           hm_b, dil, None)
        nums.append(num)
        ms.append(m)
        ls.append(l)
        s_ref[j, 0] = shifted(kt, new_col(KT_KB + g * GRP_B, GRP_B))
        s_ref[j, 1] = shifted(vt, new_col(KT_VB + g * GRP_B, GRP_B))
    mx = jnp.maximum(jnp.maximum(ms[0], ms[1]), ms[2])
    ws = [jnp.exp(m - mx) for m in ms]
    num = ws[0] * nums[0] + ws[1] * nums[1] + ws[2] * nums[2]
    den = ws[0] * ls[0] + ws[1] * ls[1] + ws[2] * ls[2]
    ob_ref[pl.ds(b, 1), :] = jnp.sum(jnp.where(hm_b, num / den, 0.0), axis=0, keepdims=True)


def _sample_attn(zs, kvt, sinks_col, caches):
    nb = zs.shape[0]
    full2 = lambda a: pl.BlockSpec(a.shape, lambda b: (0, 0))
    cspec = lambda a: pl.BlockSpec((SEQS_PER_STEP,) + a.shape[1:], lambda b: (b, 0, 0, 0))
    return pl.pallas_call(
        _sample_attn_kernel,
        grid=(nb // SEQS_PER_STEP,),
        in_specs=[full2(zs), full2(kvt), full2(sinks_col)] + [cspec(c) for c in caches],
        out_specs=[cspec(c) for c in caches] + [
            pl.BlockSpec((nb, Q_A), lambda b: (0, 0)),
            pl.BlockSpec((nb, GRP_B), lambda b: (0, 0)),
        ],
        out_shape=[jax.ShapeDtypeStruct(c.shape, F32) for c in caches] + [
            jax.ShapeDtypeStruct((nb, Q_A), F32),
            jax.ShapeDtypeStruct((nb, GRP_B), F32),
        ],
        compiler_params=pltpu.CompilerParams(
            dimension_semantics=("arbitrary",), vmem_limit_bytes=VMEM_LIMIT),
        name="sample_attn",
    )(zs, kvt, sinks_col, *caches)


def _rows_minor(c):
    _, bsz, rows, two, nh, hd = c.shape
    return jnp.transpose(c, (0, 1, 3, 4, 5, 2)).reshape(bsz, two, nh * hd, rows)


def _rows_major(c, nh):
    bsz, two, _, rows = c.shape
    return jnp.transpose(c.reshape(1, bsz, two, nh, HEAD_DIM, rows), (0, 1, 5, 2, 3, 4))


def kernel(x_prompt, x_sample, cache_a_kv, cache_b1_kv, cache_b2_kv, cache_b3_kv, norm1_g, w_in, sinks,
           w_pa, w_pb, w_o, norm2_g, w_gu, w_down, final_norm_g):
    bsz, seq, _ = x_prompt.shape
    nb, dec_seq, _ = x_sample.shape
    assert dec_seq == 1 and norm1_g.shape[0] == 1
    assert cache_a_kv.shape[2] == WINDOW_A
    assert tuple(c.shape[2] for c in (cache_b1_kv, cache_b2_kv, cache_b3_kv)) == tuple(
        w for w, _ in DILATED_PAIRS)

    w_in_b = w_in[0].astype(BF16)
    gf = final_norm_g.reshape(1, D_MODEL)

    tables_p = _rope_tables(jnp.arange(seq, dtype=jnp.int32))
    (qa, kad, vad, b1, b2, b3, gate, pa, pb1, pb2, pb3) = _proj_prompt(
        x_prompt, norm1_g, w_in_b, tables_p, TM_PROJ)
    oa, w_pa_b, w_pb_b, w_o_b, w_gu_b, w_dn_b = _attn_a(
        sinks[0], qa, kad, vad, (w_pa[0], w_pb[0], w_o[0], w_gu[0], w_down[0]))
    ob = _attn_b(b1, b2, b3)
    y_prompt = _finish(x_prompt, oa, ob, gate, w_pa_b, w_pb_b, w_o_b, norm2_g, w_gu_b, w_dn_b, gf, TM_FINISH)

    xs = x_sample.reshape(nb, D_MODEL)
    tables_s = _rope_tables(jnp.full((nb,), PAST_LEN, dtype=jnp.int32))
    zs, kvt, gate_s = _proj_sample(xs, norm1_g, w_in_b, tables_s)
    caches = [_rows_minor(c) for c in (cache_a_kv, cache_b1_kv, cache_b2_kv, cache_b3_kv)]
    sa, sb1, sb2, sb3, oa_s, ob_s = _sample_attn(zs, kvt, sinks[0].reshape(N_HEADS_A, 1), caches)
    oa_s = jnp.transpose(oa_s.reshape(nb, PAIRS_A, LANES), (1, 0, 2))[None].astype(BF16)
    y_sample = _finish(xs[None], oa_s, ob_s[None].astype(BF16), gate_s[None], w_pa_b, w_pb_b, w_o_b,
                       norm2_g, w_gu_b, w_dn_b, gf, nb)

    return (y_prompt, y_sample.reshape(nb, 1, D_MODEL),
            _rows_major(pa, N_KV_A), _rows_major(pb1, HEADS_B), _rows_major(pb2, HEADS_B),
            _rows_major(pb3, HEADS_B),
            _rows_major(sa, N_KV_A), _rows_major(sb1, HEADS_B), _rows_major(sb2, HEADS_B),
            _rows_major(sb3, HEADS_B))
```

```python
import jax
import jax.numpy as jnp
from jax import lax
from jax.experimental import pallas as pl
from jax.experimental.pallas import tpu as pltpu

F32 = jnp.float32
BF16 = jnp.bfloat16

D_MODEL = 1024
HEAD_DIM = 64
HALF = HEAD_DIM // 2
N_HEADS_A = 16
N_KV_A = 2
GROUP_A = N_HEADS_A // N_KV_A
WINDOW_A = 128
DILATED_PAIRS = ((128, 1), (512, 4), (2048, 16))
N_GROUPS_B = 3
HEADS_B = 4
SPAN = 128
PAST_LEN = 16384
ROPE_THETA = 10000.0
NORM_EPS = 1e-6
NEG_INF = -1e30
SCALE = HEAD_DIM ** -0.5

LANES = 128
PAIRS_A = N_HEADS_A // 2
PAIRS_B = HEADS_B // 2
Q_A = N_HEADS_A * HEAD_DIM
KV_A = N_KV_A * HEAD_DIM
GRP_B = HEADS_B * HEAD_DIM
B_W = N_GROUPS_B * GRP_B
OFF_KA = Q_A
OFF_VA = OFF_KA + KV_A
OFF_QB = OFF_VA + KV_A
OFF_KB = OFF_QB + B_W
OFF_VB = OFF_KB + B_W
OFF_GATE = OFF_VB + B_W
QBLK = 128
B_PARTS = 4

VMEM_LIMIT = 56 * 1024 * 1024
TM_PROJ = 512
TM_FINISH = 512


def _rms(x, g):
    return x * lax.rsqrt(jnp.mean(x * x, axis=-1, keepdims=True) + NORM_EPS) * g


def _rope(z, cos, sina, sinb):
    return z * cos + pltpu.roll(z, LANES - HALF, 1) * sina + pltpu.roll(z, HALF, 1) * sinb


def _sigmoid(z):
    return 1.0 / (1.0 + jnp.exp(-z))


def _dup_halves(z, lo):
    r = pltpu.roll(z, HEAD_DIM, 1)
    return jnp.where(lo, z, r), jnp.where(lo, r, z)


def _rope_tables(pos):
    inv = 1.0 / (ROPE_THETA ** (jnp.arange(HALF, dtype=F32) / HALF))
    ang = pos.astype(F32)[:, None] * inv[None, :]
    ang = jnp.concatenate([ang, ang, ang, ang], axis=-1)
    first = (jnp.arange(LANES) % HEAD_DIM) < HALF
    cos = jnp.cos(ang)
    sin = jnp.sin(ang)
    return cos, jnp.where(first, -sin, 0.0), jnp.where(first, 0.0, sin)


def _proj_prompt_kernel(x_ref, g_ref, w_ref, cos_ref, sina_ref, sinb_ref,
                        qa_ref, kad_ref, vad_ref, b1_ref, b2_ref, b3_ref, gate_ref,
                        pa_ref, pb1_ref, pb2_ref, pb3_ref, scr_ref, scr2_ref):
    tm = x_ref.shape[1]
    i = pl.program_id(1)
    last = pl.num_programs(1) - 1
    h = _rms(x_ref[0], g_ref[...]).astype(BF16)
    cos = cos_ref[...]
    sina = sina_ref[...]
    sinb = sinb_ref[...]
    lo = lax.broadcasted_iota(jnp.int32, (tm, LANES), 1) < HEAD_DIM

    wide = 8 * LANES
    chunks = {}

    def mm(ci):
        cols = slice(ci * wide, min((ci + 1) * wide, w_ref.shape[1]))
        chunks[ci] = jnp.dot(h, w_ref[:, cols], preferred_element_type=F32)

    def rope(z):
        return _rope(z, cos, sina, sinb)

    offs = ([c * 2 * LANES for c in range(PAIRS_A // 2)] + [OFF_KA]
            + [o + g * GRP_B for g in range(N_GROUPS_B) for o in (OFF_QB, OFF_KB, OFF_VB)]
            + [OFF_GATE + c * 2 * LANES for c in range(2 * D_MODEL // (2 * LANES))])
    mm(0)

    def take(off):
        nxt = offs.index(off) + 1
        for o in offs[nxt - 1:nxt + 1]:
            if o // wide not in chunks:
                mm(o // wide)
        ci, within = divmod(off, wide)
        z = chunks[ci]
        return z[:, within:within + LANES], z[:, within + LANES:within + 2 * LANES]

    for c in range(PAIRS_A // 2):
        z0, z1 = take(c * 2 * LANES)
        qa_ref[0, 2 * c] = (rope(z0) * SCALE).astype(BF16)
        qa_ref[0, 2 * c + 1] = (rope(z1) * SCALE).astype(BF16)

    zk, zv = take(OFF_KA)
    ka = rope(zk)
    k0, k1 = _dup_halves(ka, lo)
    v0, v1 = _dup_halves(zv, lo)
    kad_ref[0, 0] = k0.astype(BF16)
    kad_ref[0, 1] = k1.astype(BF16)
    for kv, vd in enumerate((v0, v1)):
        vad_ref[0, kv, 0] = jnp.where(lo, vd, 1.0).astype(BF16)
        vad_ref[0, kv, 1] = jnp.where(lo, 1.0, vd).astype(BF16)

    @pl.when(i == last)
    def _():
        pa_ref[0, 0] = ka[tm - WINDOW_A:, :].T
        pa_ref[0, 1] = zv[tm - WINDOW_A:, :].T

    b_refs = (b1_ref, b2_ref, b3_ref)
    for g, (win, dil) in enumerate(DILATED_PAIRS):
        q0, q1 = take(OFF_QB + g * GRP_B)
        k0, k1 = take(OFF_KB + g * GRP_B)
        v0, v1 = take(OFF_VB + g * GRP_B)
        q0 = rope(q0) * SCALE
        q1 = rope(q1) * SCALE
        k0 = rope(k0)
        k1 = rope(k1)
        parts = ((q0, q1), (k0, k1),
                 (jnp.where(lo, v0, 1.0), jnp.where(lo, v1, 1.0)),
                 (jnp.where(lo, 1.0, v0), jnp.where(lo, 1.0, v1)))
        for part in range(B_PARTS):
            for pr in range(PAIRS_B):
                val = parts[part][pr]
                if dil == 1:
                    b_refs[g][0, part, pr, 0] = val.astype(BF16)
                elif dil == 16:
                    slot = part * PAIRS_B + pr
                    scr_ref[slot] = val
                    q4 = tm // 4
                    for r4 in range(4):
                        scr2_ref[slot, r4 * q4:(r4 + 1) * q4, :] = scr_ref[slot, pl.ds(r4, q4, stride=4), :]
                    for r4 in range(4):
                        for r16 in range(4):
                            b_refs[g][0, part, pr, r4 + 4 * r16] = scr2_ref[
                                slot, pl.ds(r4 * q4 + r16, q4 // 4, stride=4), :].astype(BF16)
                else:
                    slot = part * PAIRS_B + pr
                    scr_ref[slot] = val
                    for r in range(dil):
                        b_refs[g][0, part, pr, r] = scr_ref[
                            slot, pl.ds(r, tm // dil, stride=dil), :].astype(BF16)
        kt = jnp.concatenate([k0, k1], axis=-1).T
        vt = jnp.concatenate([v0, v1], axis=-1).T
        if g == 2:
            pb3_ref[0, 0] = kt
            pb3_ref[0, 1] = vt
        else:
            st_ref = pb1_ref if g == 0 else pb2_ref
            rows = st_ref.shape[3]

            @pl.when(i == last)
            def _(kt=kt, vt=vt, st_ref=st_ref, rows=rows):
                st_ref[0, 0] = kt[:, tm - rows:]
                st_ref[0, 1] = vt[:, tm - rows:]

    for c in range(2 * D_MODEL // (2 * LANES)):
        z0, z1 = take(OFF_GATE + c * 2 * LANES)
        gate_ref[0, :, c * 2 * LANES:c * 2 * LANES + LANES] = _sigmoid(z0)
        gate_ref[0, :, c * 2 * LANES + LANES:(c + 1) * 2 * LANES] = _sigmoid(z1)


def _proj_prompt(x, g1, w_in, tables, tm):
    bsz, seq, _ = x.shape
    nt = seq // tm
    cos, sina, sinb = tables
    const2 = lambda b, i: (0, 0)
    tab_spec = pl.BlockSpec((tm, LANES), lambda b, i: (i, 0))
    out_shape = [
        jax.ShapeDtypeStruct((bsz, PAIRS_A, seq, LANES), BF16),
        jax.ShapeDtypeStruct((bsz, N_KV_A, seq, LANES), BF16),
        jax.ShapeDtypeStruct((bsz, N_KV_A, 2, seq, LANES), BF16),
    ]
    out_specs = [
        pl.BlockSpec((1, PAIRS_A, tm, LANES), lambda b, i: (b, 0, i, 0)),
        pl.BlockSpec((1, N_KV_A, tm, LANES), lambda b, i: (b, 0, i, 0)),
        pl.BlockSpec((1, N_KV_A, 2, tm, LANES), lambda b, i: (b, 0, 0, i, 0)),
    ]
    for _, dil in DILATED_PAIRS:
        out_shape.append(jax.ShapeDtypeStruct((bsz, B_PARTS, PAIRS_B, dil, seq // dil, LANES), BF16))
        out_specs.append(pl.BlockSpec((1, B_PARTS, PAIRS_B, dil, tm // dil, LANES),
                                      lambda b, i: (b, 0, 0, 0, i, 0)))
    out_shape.append(jax.ShapeDtypeStruct((bsz, seq, 2 * D_MODEL), F32))
    out_specs.append(pl.BlockSpec((1, tm, 2 * D_MODEL), lambda b, i: (b, i, 0)))
    out_shape.append(jax.ShapeDtypeStruct((bsz, 2, KV_A, min(WINDOW_A, seq)), F32))
    out_specs.append(pl.BlockSpec((1, 2, KV_A, min(WINDOW_A, seq)), lambda b, i: (b, 0, 0, 0)))
    for g, (win, _) in enumerate(DILATED_PAIRS):
        rows = min(win, seq)
        out_shape.append(jax.ShapeDtypeStruct((bsz, 2, GRP_B, rows), F32))
        if g == 2:
            out_specs.append(pl.BlockSpec((1, 2, GRP_B, tm), lambda b, i: (b, 0, 0, i)))
        else:
            out_specs.append(pl.BlockSpec((1, 2, GRP_B, rows), lambda b, i: (b, 0, 0, 0)))
    return pl.pallas_call(
        _proj_prompt_kernel,
        grid=(bsz, nt),
        in_specs=[
            pl.BlockSpec((1, tm, D_MODEL), lambda b, i: (b, i, 0)),
            pl.BlockSpec((1, D_MODEL), const2),
            pl.BlockSpec(w_in.shape, const2, pipeline_mode=pl.Buffered(1)),
            tab_spec, tab_spec, tab_spec,
        ],
        out_specs=out_specs,
        out_shape=out_shape,
        scratch_shapes=[pltpu.VMEM((B_PARTS * PAIRS_B, tm, LANES), F32),
                        pltpu.VMEM((B_PARTS * PAIRS_B, tm, LANES), F32)],
        compiler_params=pltpu.CompilerParams(
            dimension_semantics=("arbitrary", "arbitrary"), vmem_limit_bytes=VMEM_LIMIT),
        name="proj_prompt",
    )(x, g1, w_in, cos, sina, sinb)


def _band_biases():
    qi = lax.broadcasted_iota(jnp.int32, (QBLK, 2 * QBLK), 0)
    ki = lax.broadcasted_iota(jnp.int32, (QBLK, 2 * QBLK), 1)
    band = jnp.where((ki >= qi) & (ki <= qi + SPAN), 0.0, NEG_INF).astype(F32)
    first = jnp.where(ki <= qi, 0.0, NEG_INF).astype(F32)
    return first, band, first[:, :QBLK]


def _head_scores(q2, k, bias, lo, hh):
    qm = jnp.where(lo if hh == 0 else jnp.logical_not(lo), q2, jnp.zeros_like(q2))
    s = lax.dot_general(qm, k, (((1,), (1,)), ((), ())), preferred_element_type=F32)
    return s + bias


def _pair_pv(e0, e1, v_lo, v_hi, lo):
    pv0 = jnp.dot(e0, v_lo, preferred_element_type=F32)
    pv1 = jnp.dot(e1, v_hi, preferred_element_type=F32)
    num = jnp.where(lo, pv0, pv1)
    den = pltpu.roll(jnp.where(lo, pv1, pv0), HEAD_DIM, 1)
    return num, den


def _blk(j):
    return j * QBLK if isinstance(j, int) else pl.multiple_of(j * QBLK, QBLK)


def _win(j):
    return max(j - 1, 0) * QBLK if isinstance(j, int) else pl.multiple_of(jnp.maximum(j - 1, 0) * QBLK, QBLK)


def _attn_a_kernel(sink_ref, qa_ref, kd_ref, va_ref, *rest):
    nw = (len(rest) - 5) // 2
    oa_ref = rest[nw]
    s_scr, e_scr, m_scr, t_scr = rest[2 * nw + 1:]
    for src, dst in zip(rest[:nw], rest[nw + 1:2 * nw + 1]):
        dst[...] = src[...].astype(BF16)
    seq = qa_ref.shape[2]
    nblk = seq // QBLK
    first, band, _ = _band_biases()
    lo = lax.broadcasted_iota(jnp.int32, (QBLK, LANES), 1) < HEAD_DIM
    pairs_per_kv = PAIRS_A // N_KV_A

    def stage_qk(jb, slot, bias):
        for kv in range(N_KV_A):
            k = kd_ref[0, kv, pl.ds(_win(jb), 2 * QBLK), :]
            for pr in range(pairs_per_kv):
                pair = kv * pairs_per_kv + pr
                q2 = qa_ref[0, pair, pl.ds(_blk(jb), QBLK), :]
                for hh in range(2):
                    h = 2 * pair + hh
                    s = _head_scores(q2, k, bias, lo, hh)
                    s_scr[slot, h] = s
                    m = jnp.broadcast_to(jnp.max(s, axis=1, keepdims=True), (QBLK, LANES))
                    m_scr[slot, h] = jnp.maximum(m, sink_ref[h])

    def stage_sm(slot):
        for pair in range(PAIRS_A):
            ms = []
            for hh in range(2):
                h = 2 * pair + hh
                m = m_scr[slot, h]
                e_scr[slot, h] = jnp.exp(s_scr[slot, h] - jnp.tile(m, (1, 2))).astype(BF16)
                ms.append(m)
            sink_pair = jnp.where(lo, sink_ref[2 * pair], sink_ref[2 * pair + 1])
            t_scr[slot, pair] = jnp.exp(sink_pair - jnp.where(lo, ms[0], ms[1]))

    def stage_pv(jb, slot):
        for kv in range(N_KV_A):
            v_lo = va_ref[0, kv, 0, pl.ds(_win(jb), 2 * QBLK), :]
            v_hi = va_ref[0, kv, 1, pl.ds(_win(jb), 2 * QBLK), :]
            for pr in range(pairs_per_kv):
                pair = kv * pairs_per_kv + pr
                num, den = _pair_pv(e_scr[slot, 2 * pair], e_scr[slot, 2 * pair + 1], v_lo, v_hi, lo)
                den = den + t_scr[slot, pair]
                oa_ref[0, pair, pl.ds(_blk(jb), QBLK), :] = (num / den).astype(BF16)

    stage_qk(0, 0, first)
    stage_qk(1, 1, band)
    stage_sm(0)

    def body(i, carry):
        t = 2 + 2 * i
        stage_pv(t - 2, 0)
        stage_qk(t, 0, band)
        stage_sm(1)
        stage_pv(t - 1, 1)
        stage_qk(t + 1, 1, band)
        stage_sm(0)
        return carry

    lax.fori_loop(0, (nblk - 2) // 2, body, 0)
    stage_sm(1)
    stage_pv(nblk - 2, 0)
    stage_pv(nblk - 1, 1)


def _attn_a(sinks, qa, kad, vad, weights):
    bsz, _, seq, _ = qa.shape
    assert (seq // QBLK) % 2 == 0 and seq >= 2 * QBLK
    bf16_rows = 16
    assert all(w.shape[0] % (bsz * bf16_rows) == 0 for w in weights)
    wspecs = [pl.BlockSpec((w.shape[0] // bsz, w.shape[1]), lambda b: (b, 0)) for w in weights]
    return pl.pallas_call(
        _attn_a_kernel,
        grid=(bsz,),
        in_specs=[
            pl.BlockSpec(memory_space=pltpu.SMEM),
            pl.BlockSpec((1, PAIRS_A, seq, LANES), lambda b: (b, 0, 0, 0)),
            pl.BlockSpec((1, N_KV_A, seq, LANES), lambda b: (b, 0, 0, 0)),
            pl.BlockSpec((1, N_KV_A, 2, seq, LANES), lambda b: (b, 0, 0, 0, 0)),
        ] + wspecs,
        out_specs=[pl.BlockSpec((1, PAIRS_A, seq, LANES), lambda b: (b, 0, 0, 0))] + wspecs,
        out_shape=[jax.ShapeDtypeStruct((bsz, PAIRS_A, seq, LANES), BF16)] + [
            jax.ShapeDtypeStruct(w.shape, BF16) for w in weights],
        scratch_shapes=[pltpu.VMEM((2, N_HEADS_A, QBLK, 2 * QBLK), F32),
                        pltpu.VMEM((2, N_HEADS_A, QBLK, 2 * QBLK), BF16),
                        pltpu.VMEM((2, N_HEADS_A, QBLK, LANES), F32),
                        pltpu.VMEM((2, PAIRS_A, QBLK, LANES), F32)],
        compiler_params=pltpu.CompilerParams(
            dimension_semantics=("arbitrary",), vmem_limit_bytes=VMEM_LIMIT),
        name="attn_a",
    )(sinks, qa, kad, vad, *weights)


def _attn_b_kernel(b1_ref, b2_ref, b3_ref, ob_ref, o_scr, lse_scr, s12_scr, s3_scr, e12_scr, e3_scr, m_scr,
                   mp_scr):
    seq = ob_ref.shape[1]
    nunit = seq // QBLK
    first, band, causal = _band_biases()
    lo = lax.broadcasted_iota(jnp.int32, (QBLK, LANES), 1) < HEAD_DIM
    refs = (b1_ref, b2_ref, b3_ref)
    dils = tuple(d for _, d in DILATED_PAIRS)
    nblks = tuple(nunit // d for d in dils)

    def coords(g, u):
        if nblks[g] == 1:
            return u, 0
        if nblks[g] == nunit:
            return 0, u
        return u // nblks[g], u % nblks[g]

    def stage_qk(u, slot):
        for g in range(N_GROUPS_B):
            r, jb = coords(g, u)
            for pr in range(PAIRS_B):
                q2 = refs[g][0, 0, pr, r, pl.ds(_blk(jb), QBLK), :]
                if nblks[g] == 1:
                    k = refs[g][0, 1, pr, r, :, :]
                    bias = causal
                else:
                    k = refs[g][0, 1, pr, r, pl.ds(_win(jb), 2 * QBLK), :]
                    if isinstance(jb, int):
                        bias = first if jb == 0 else band
                    elif nblks[g] == nunit:
                        bias = band
                    else:
                        bias = jnp.where(jb == 0, first, band)
                for hh in range(2):
                    s = _head_scores(q2, k, bias, lo, hh)
                    if nblks[g] == 1:
                        s3_scr[slot, 2 * pr + hh] = s
                    else:
                        s12_scr[slot, g, 2 * pr + hh] = s
                    m_scr[slot, g, 2 * pr + hh] = jnp.broadcast_to(
                        jnp.max(s, axis=1, keepdims=True), (QBLK, LANES))

    def stage_sm(slot):
        for g in range(N_GROUPS_B):
            for h in range(2 * PAIRS_B):
                m = m_scr[slot, g, h]
                if nblks[g] == 1:
                    e3_scr[slot, h] = jnp.exp(s3_scr[slot, h] - m).astype(BF16)
                else:
                    e12_scr[slot, g, h] = jnp.exp(s12_scr[slot, g, h] - jnp.tile(m, (1, 2))).astype(BF16)
            for pr in range(PAIRS_B):
                mp_scr[slot, g, pr] = jnp.where(lo, m_scr[slot, g, 2 * pr], m_scr[slot, g, 2 * pr + 1])

    def stage_pv(u, slot):
        for g in range(N_GROUPS_B):
            r, jb = coords(g, u)
            for pr in range(PAIRS_B):
                if nblks[g] == 1:
                    v_lo = refs[g][0, 2, pr, r, :, :]
                    v_hi = refs[g][0, 3, pr, r, :, :]
                    e0, e1 = e3_scr[slot, 2 * pr], e3_scr[slot, 2 * pr + 1]
                else:
                    v_lo = refs[g][0, 2, pr, r, pl.ds(_win(jb), 2 * QBLK), :]
                    v_hi = refs[g][0, 3, pr, r, pl.ds(_win(jb), 2 * QBLK), :]
                    e0, e1 = e12_scr[slot, g, 2 * pr], e12_scr[slot, g, 2 * pr + 1]
                num, den = _pair_pv(e0, e1, v_lo, v_hi, lo)
                row0 = _blk(jb) * dils[g] + r
                idx = pl.ds(row0, QBLK) if dils[g] == 1 else pl.ds(row0, QBLK, stride=dils[g])
                o_scr[g, pr, idx, :] = num / den
                lse_scr[g, pr, idx, :] = mp_scr[slot, g, pr] + jnp.log(den)

    stage_qk(0, 0)
    stage_qk(1, 1)
    stage_sm(0)

    def body(i, carry):
        u = 2 + 2 * i
        stage_qk(u, 0)
        stage_sm(1)
        stage_pv(u - 2, 0)
        stage_qk(u + 1, 1)
        stage_sm(0)
        stage_pv(u - 1, 1)
        return carry

    lax.fori_loop(0, (nunit - 2) // 2, body, 0)
    stage_sm(1)
    stage_pv(nunit - 2, 0)
    stage_pv(nunit - 1, 1)

    mrows = 2 * QBLK

    def merge(c, carry):
        rows = pl.ds(pl.multiple_of(c * mrows, mrows), mrows)
        for pr in range(PAIRS_B):
            ls = [lse_scr[g, pr, rows, :] for g in range(N_GROUPS_B)]
            mx = jnp.maximum(jnp.maximum(ls[0], ls[1]), ls[2])
            ws = [jnp.exp(l - mx) for l in ls]
            num = ws[0] * o_scr[0, pr, rows, :] + ws[1] * o_scr[1, pr, rows, :] + ws[2] * o_scr[2, pr, rows, :]
            den = ws[0] + ws[1] + ws[2]
            ob_ref[0, rows, pr * LANES:(pr + 1) * LANES] = (num / den).astype(BF16)
        return carry

    lax.fori_loop(0, seq // mrows, merge, 0)


def _attn_b(b1, b2, b3):
    bsz = b1.shape[0]
    seq = b1.shape[4]
    nunit = seq // QBLK
    assert nunit % 2 == 0 and nunit >= 4 and nunit == DILATED_PAIRS[-1][1]
    assert b1.shape[3:5] == (1, seq) and b3.shape[4] == QBLK
    specs = [pl.BlockSpec((1,) + b.shape[1:], lambda b_: (b_, 0, 0, 0, 0, 0)) for b in (b1, b2, b3)]
    heads = 2 * PAIRS_B
    return pl.pallas_call(
        _attn_b_kernel,
        grid=(bsz,),
        in_specs=specs,
        out_specs=pl.BlockSpec((1, seq, GRP_B), lambda b_: (b_, 0, 0)),
        out_shape=jax.ShapeDtypeStruct((bsz, seq, GRP_B), BF16),
        scratch_shapes=[pltpu.VMEM((N_GROUPS_B, PAIRS_B, seq, LANES), F32),
                        pltpu.VMEM((N_GROUPS_B, PAIRS_B, seq, LANES), F32),
                        pltpu.VMEM((2, 2, heads, QBLK, 2 * QBLK), F32),
                        pltpu.VMEM((2, heads, QBLK, QBLK), F32),
                        pltpu.VMEM((2, 2, heads, QBLK, 2 * QBLK), BF16),
                        pltpu.VMEM((2, heads, QBLK, QBLK), BF16),
                        pltpu.VMEM((2, N_GROUPS_B, heads, QBLK, LANES), F32),
                        pltpu.VMEM((2, N_GROUPS_B, PAIRS_B, QBLK, LANES), F32)],
        compiler_params=pltpu.CompilerParams(
            dimension_semantics=("arbitrary",), vmem_limit_bytes=VMEM_LIMIT),
        name="attn_b",
    )(b1, b2, b3)


def _finish_kernel(x_ref, oa_ref, ob_ref, gate_ref, wpa_ref, wpb_ref, wo_ref, g2_ref,
                   wgu_ref, wdn_ref, gf_ref, y_ref):
    d_ff = wdn_ref.shape[0]
    oa = jnp.concatenate([oa_ref[0, p] for p in range(PAIRS_A)], axis=-1)
    ma = jnp.dot(oa, wpa_ref[...], preferred_element_type=F32)
    mb = jnp.dot(ob_ref[0], wpb_ref[...], preferred_element_type=F32)
    m = gate_ref[0, :, :D_MODEL] * ma + gate_ref[0, :, D_MODEL:] * mb
    x1 = x_ref[0] + jnp.dot(m.astype(BF16), wo_ref[...], preferred_element_type=F32)
    xn = _rms(x1, g2_ref[...]).astype(BF16)
    gu = jnp.dot(xn, wgu_ref[...], preferred_element_type=F32)
    gg = gu[:, :d_ff]
    act = (gg * _sigmoid(gg) * gu[:, d_ff:]).astype(BF16)
    y_ref[0] = _rms(x1 + jnp.dot(act, wdn_ref[...], preferred_element_type=F32), gf_ref[...])


def _finish(x, oa, ob, gate, w_pa, w_pb, w_o, g2, w_gu, w_down, gf, tm):
    bsz, seq, _ = x.shape
    const2 = lambda b, i: (0, 0)
    wspec = lambda w: pl.BlockSpec(w.shape, const2, pipeline_mode=pl.Buffered(1))
    return pl.pallas_call(
        _finish_kernel,
        grid=(bsz, seq // tm),
        in_specs=[
            pl.BlockSpec((1, tm, D_MODEL), lambda b, i: (b, i, 0)),
            pl.BlockSpec((1, PAIRS_A, tm, LANES), lambda b, i: (b, 0, i, 0)),
            pl.BlockSpec((1, tm, GRP_B), lambda b, i: (b, i, 0)),
            pl.BlockSpec((1, tm, 2 * D_MODEL), lambda b, i: (b, i, 0)),
            wspec(w_pa), wspec(w_pb), wspec(w_o),
            pl.BlockSpec((1, D_MODEL), const2),
            wspec(w_gu), wspec(w_down),
            pl.BlockSpec((1, D_MODEL), const2),
        ],
        out_specs=pl.BlockSpec((1, tm, D_MODEL), lambda b, i: (b, i, 0)),
        out_shape=jax.ShapeDtypeStruct((bsz, seq, D_MODEL), F32),
        compiler_params=pltpu.CompilerParams(
            dimension_semantics=("arbitrary", "arbitrary"), vmem_limit_bytes=VMEM_LIMIT),
        name="finish",
    )(x, oa, ob, gate, w_pa, w_pb, w_o, g2, w_gu, w_down, gf)


ZS_QA = 0
ZS_QB = ZS_QA + Q_A
ZS_KA = ZS_QB + B_W
ZS_VA = ZS_KA + N_KV_A * GROUP_A * HEAD_DIM
ZS_KB = ZS_VA + N_KV_A * GROUP_A * HEAD_DIM
ZS_VB = ZS_KB + B_W
ZS_W = ZS_VB + B_W
KT_KA = 0
KT_VA = KT_KA + KV_A
KT_KB = KT_VA + KV_A
KT_VB = KT_KB + B_W
KT_W = KT_VB + B_W


def _proj_sample_kernel(x_ref, g_ref, w_ref, cos_ref, sina_ref, sinb_ref, zs_ref, kvt_ref, gate_ref):
    nb = x_ref.shape[0]
    h = _rms(x_ref[...], g_ref[...]).astype(BF16)
    cos = cos_ref[...]
    sina = sina_ref[...]
    sinb = sinb_ref[...]
    lo = lax.broadcasted_iota(jnp.int32, (nb, LANES), 1) < HEAD_DIM

    def mm(off):
        z = jnp.dot(h, w_ref[:, off:off + 2 * LANES], preferred_element_type=F32)
        return z[:, :LANES], z[:, LANES:]

    def rope(z):
        return _rope(z, cos, sina, sinb)

    def put(off, val):
        zs_ref[:, off:off + LANES] = val

    for c in range(PAIRS_A // 2):
        z0, z1 = mm(c * 2 * LANES)
        put(ZS_QA + c * 2 * LANES, rope(z0) * SCALE)
        put(ZS_QA + c * 2 * LANES + LANES, rope(z1) * SCALE)
    for c in range(B_W // (2 * LANES)):
        z0, z1 = mm(OFF_QB + c * 2 * LANES)
        put(ZS_QB + c * 2 * LANES, rope(z0) * SCALE)
        put(ZS_QB + c * 2 * LANES + LANES, rope(z1) * SCALE)

    zk, zv = mm(OFF_KA)
    ka = rope(zk)
    kvt_ref[KT_KA:KT_KA + KV_A, :] = ka.T
    kvt_ref[KT_VA:KT_VA + KV_A, :] = zv.T
    reps = GROUP_A * HEAD_DIM // LANES
    for base, val in ((ZS_KA, ka), (ZS_VA, zv)):
        d0, d1 = _dup_halves(val, lo)
        for kv, dup in enumerate((d0, d1)):
            for c in range(reps):
                put(base + (kv * reps + c) * LANES, dup)

    for c in range(B_W // (2 * LANES)):
        k0, k1 = mm(OFF_KB + c * 2 * LANES)
        v0, v1 = mm(OFF_VB + c * 2 * LANES)
        k0 = rope(k0)
        k1 = rope(k1)
        for j, (kk, vv) in enumerate(((k0, v0), (k1, v1))):
            off = (2 * c + j) * LANES
            put(ZS_KB + off, kk)
            put(ZS_VB + off, vv)
            kvt_ref[KT_KB + off:KT_KB + off + LANES, :] = kk.T
            kvt_ref[KT_VB + off:KT_VB + off + LANES, :] = vv.T

    for c in range(2 * D_MODEL // (2 * LANES)):
        z0, z1 = mm(OFF_GATE + c * 2 * LANES)
        gate_ref[:, c * 2 * LANES:c * 2 * LANES + LANES] = _sigmoid(z0)
        gate_ref[:, c * 2 * LANES + LANES:(c + 1) * 2 * LANES] = _sigmoid(z1)


def _proj_sample(x, g1, w_in, tables):
    nb = x.shape[0]
    cos, sina, sinb = tables
    return pl.pallas_call(
        _proj_sample_kernel,
        out_shape=[
            jax.ShapeDtypeStruct((nb, ZS_W), F32),
            jax.ShapeDtypeStruct((KT_W, nb), F32),
            jax.ShapeDtypeStruct((nb, 2 * D_MODEL), F32),
        ],
        compiler_params=pltpu.CompilerParams(vmem_limit_bytes=VMEM_LIMIT),
        name="proj_sample",
    )(x, g1, w_in, cos, sina, sinb)


SEQS_PER_STEP = 2


def _sample_attn_kernel(zs_ref, kvt_ref, sink_ref, ca_ref, c1_ref, c2_ref, c3_ref,
                        sa_ref, s1_ref, s2_ref, s3_ref, oa_ref, ob_ref):
    for j in range(ca_ref.shape[0]):
        _sample_attn_one(j, zs_ref, kvt_ref, sink_ref, ca_ref, c1_ref, c2_ref, c3_ref,
                         sa_ref, s1_ref, s2_ref, s3_ref, oa_ref, ob_ref)


def _sample_attn_one(j, zs_ref, kvt_ref, sink_ref, ca_ref, c1_ref, c2_ref, c3_ref,
                     sa_ref, s1_ref, s2_ref, s3_ref, oa_ref, ob_ref):
    b = pl.program_id(0) * ca_ref.shape[0] + j
    nb = kvt_ref.shape[1]
    pick = lax.broadcasted_iota(jnp.int32, (1, nb), 1) == b

    def row(off, width):
        return zs_ref[pl.ds(b, 1), off:off + width]

    def new_col(off, width):
        blk = kvt_ref[off:off + width, :]
        return jnp.sum(jnp.where(pick, blk, 0.0), axis=1, keepdims=True)

    def shifted(tile, col):
        w = tile.shape[1]
        lane = lax.broadcasted_iota(jnp.int32, tile.shape, 1)
        return jnp.where(lane == w - 1, col, pltpu.roll(tile, w - 1, 1))

    def head_mask(nh):
        rows = max(nh, 8)
        hid = lax.broadcasted_iota(jnp.int32, (rows, nh * HEAD_DIM), 0)
        lh = lax.broadcasted_iota(jnp.int32, (rows, nh * HEAD_DIM), 1) // HEAD_DIM
        return hid == lh

    def attend(q_row, kt, vt, knew_row, vnew_row, hm, dil, sink_col):
        qm = jnp.where(hm, q_row, 0.0)
        s = jnp.dot(qm.astype(BF16), kt.astype(BF16), preferred_element_type=F32)
        if dil > 1:
            lane = lax.broadcasted_iota(jnp.int32, s.shape, 1)
            s = jnp.where(lane % dil == 0, s, NEG_INF)
        s_new = jnp.sum(qm * knew_row, axis=1, keepdims=True)
        m = jnp.maximum(jnp.max(s, axis=1, keepdims=True), s_new)
        if sink_col is not None:
            m = jnp.maximum(m, sink_col)
        e = jnp.exp(s - m)
        e_new = jnp.exp(s_new - m)
        l = jnp.sum(e, axis=1, keepdims=True) + e_new
        if sink_col is not None:
            l = l + jnp.exp(sink_col - m)
        num = lax.dot_general(e.astype(BF16), vt.astype(BF16), (((1,), (1,)), ((), ())),
                              preferred_element_type=F32)
        return num + e_new * vnew_row, m, l

    hm_a = head_mask(GROUP_A)
    wq = GROUP_A * HEAD_DIM
    for kv in range(N_KV_A):
        kt = ca_ref[j, 0, kv * HEAD_DIM:(kv + 1) * HEAD_DIM, :]
        vt = ca_ref[j, 1, kv * HEAD_DIM:(kv + 1) * HEAD_DIM, :]
        num, m, l = attend(row(ZS_QA + kv * wq, wq),
                           jnp.concatenate([kt] * GROUP_A, axis=0),
                           jnp.concatenate([vt] * GROUP_A, axis=0),
                           row(ZS_KA + kv * wq, wq), row(ZS_VA + kv * wq, wq),
                           hm_a, 1, sink_ref[kv * GROUP_A:(kv + 1) * GROUP_A, :])
        o = jnp.sum(jnp.where(hm_a, num / l, 0.0), axis=0, keepdims=True)
        oa_ref[pl.ds(b, 1), kv * wq:(kv + 1) * wq] = o
    sa_ref[j, 0] = shifted(ca_ref[j, 0], new_col(KT_KA, KV_A))
    sa_ref[j, 1] = shifted(ca_ref[j, 1], new_col(KT_VA, KV_A))

    hm_b = head_mask(HEADS_B)
    nums, ms, ls = [], [], []
    for g, (c_ref, s_ref, (_, dil)) in enumerate(zip((c1_ref, c2_ref, c3_ref), (s1_ref, s2_ref, s3_ref),
                                                      DILATED_PAIRS)):
        kt = c_ref[j, 0]
        vt = c_ref[j, 1]
        num, m, l = attend(row(ZS_QB + g * GRP_B, GRP_B), kt, vt,
                           row(ZS_KB + g * GRP_B, GRP_B), row(ZS_VB + g * GRP_B, GRP_B),
                           hm_b, dil, None)
        nums.append(num)
        ms.append(m)
        ls.append(l)
        s_ref[j, 0] = shifted(kt, new_col(KT_KB + g * GRP_B, GRP_B))
        s_ref[j, 1] = shifted(vt, new_col(KT_VB + g * GRP_B, GRP_B))
    mx = jnp.maximum(jnp.maximum(ms[0], ms[1]), ms[2])
    ws = [jnp.exp(m - mx) for m in ms]
    num = ws[0] * nums[0] + ws[1] * nums[1] + ws[2] * nums[2]
    den = ws[0] * ls[0] + ws[1] * ls[1] + ws[2] * ls[2]
    ob_ref[pl.ds(b, 1), :] = jnp.sum(jnp.where(hm_b, num / den, 0.0), axis=0, keepdims=True)


def _sample_attn(zs, kvt, sinks_col, caches):
    nb = zs.shape[0]
    full2 = lambda a: pl.BlockSpec(a.shape, lambda b: (0, 0))
    cspec = lambda a: pl.BlockSpec((SEQS_PER_STEP,) + a.shape[1:], lambda b: (b, 0, 0, 0))
    return pl.pallas_call(
        _sample_attn_kernel,
        grid=(nb // SEQS_PER_STEP,),
        in_specs=[full2(zs), full2(kvt), full2(sinks_col)] + [cspec(c) for c in caches],
        out_specs=[cspec(c) for c in caches] + [
            pl.BlockSpec((nb, Q_A), lambda b: (0, 0)),
            pl.BlockSpec((nb, GRP_B), lambda b: (0, 0)),
        ],
        out_shape=[jax.ShapeDtypeStruct(c.shape, F32) for c in caches] + [
            jax.ShapeDtypeStruct((nb, Q_A), F32),
            jax.ShapeDtypeStruct((nb, GRP_B), F32),
        ],
        compiler_params=pltpu.CompilerParams(
            dimension_semantics=("arbitrary",), vmem_limit_bytes=VMEM_LIMIT),
        name="sample_attn",
    )(zs, kvt, sinks_col, *caches)


def _rows_minor(c):
    _, bsz, rows, two, nh, hd = c.shape
    return jnp.transpose(c, (0, 1, 3, 4, 5, 2)).reshape(bsz, two, nh * hd, rows)


def _rows_major(c, nh):
    bsz, two, _, rows = c.shape
    return jnp.transpose(c.reshape(1, bsz, two, nh, HEAD_DIM, rows), (0, 1, 5, 2, 3, 4))


def kernel(x_prompt, x_sample, cache_a_kv, cache_b1_kv, cache_b2_kv, cache_b3_kv, norm1_g, w_in, sinks,
           w_pa, w_pb, w_o, norm2_g, w_gu, w_down, final_norm_g):
    bsz, seq, _ = x_prompt.shape
    nb, dec_seq, _ = x_sample.shape
    assert dec_seq == 1 and norm1_g.shape[0] == 1
    assert cache_a_kv.shape[2] == WINDOW_A
    assert tuple(c.shape[2] for c in (cache_b1_kv, cache_b2_kv, cache_b3_kv)) == tuple(
        w for w, _ in DILATED_PAIRS)

    w_in_b = w_in[0].astype(BF16)
    gf = final_norm_g.reshape(1, D_MODEL)

    tables_p = _rope_tables(jnp.arange(seq, dtype=jnp.int32))
    (qa, kad, vad, b1, b2, b3, gate, pa, pb1, pb2, pb3) = _proj_prompt(
        x_prompt, norm1_g, w_in_b, tables_p, TM_PROJ)
    oa, w_pa_b, w_pb_b, w_o_b, w_gu_b, w_dn_b = _attn_a(
        sinks[0], qa, kad, vad, (w_pa[0], w_pb[0], w_o[0], w_gu[0], w_down[0]))
    ob = _attn_b(b1, b2, b3)
    y_prompt = _finish(x_prompt, oa, ob, gate, w_pa_b, w_pb_b, w_o_b, norm2_g, w_gu_b, w_dn_b, gf, TM_FINISH)

    xs = x_sample.reshape(nb, D_MODEL)
    tables_s = _rope_tables(jnp.full((nb,), PAST_LEN, dtype=jnp.int32))
    zs, kvt, gate_s = _proj_sample(xs, norm1_g, w_in_b, tables_s)
    caches = [_rows_minor(c) for c in (cache_a_kv, cache_b1_kv, cache_b2_kv, cache_b3_kv)]
    sa, sb1, sb2, sb3, oa_s, ob_s = _sample_attn(zs, kvt, sinks[0].reshape(N_HEADS_A, 1), caches)
    oa_s = jnp.transpose(oa_s.reshape(nb, PAIRS_A, LANES), (1, 0, 2))[None].astype(BF16)
    y_sample = _finish(xs[None], oa_s, ob_s[None].astype(BF16), gate_s[None], w_pa_b, w_pb_b, w_o_b,
                       norm2_g, w_gu_b, w_dn_b, gf, nb)

    return (y_prompt, y_sample.reshape(nb, 1, D_MODEL),
            _rows_major(pa, N_KV_A), _rows_major(pb1, HEADS_B), _rows_major(pb2, HEADS_B),
            _rows_major(pb3, HEADS_B),
            _rows_major(sa, N_KV_A), _rows_major(sb1, HEADS_B), _rows_major(sb2, HEADS_B),
            _rows_major(sb3, HEADS_B))
```

```python
import jax
import jax.numpy as jnp
from jax import lax
from jax.experimental import pallas as pl
from jax.experimental.pallas import tpu as pltpu

F32 = jnp.float32
BF16 = jnp.bfloat16

D_MODEL = 1024
HEAD_DIM = 64
HALF = HEAD_DIM // 2
N_HEADS_A = 16
N_KV_A = 2
GROUP_A = N_HEADS_A // N_KV_A
WINDOW_A = 128
DILATED_PAIRS = ((128, 1), (512, 4), (2048, 16))
N_GROUPS_B = 3
HEADS_B = 4
SPAN = 128
PAST_LEN = 16384
ROPE_THETA = 10000.0
NORM_EPS = 1e-6
NEG_INF = -1e30
SCALE = HEAD_DIM ** -0.5

LANES = 128
PAIRS_A = N_HEADS_A // 2
PAIRS_B = HEADS_B // 2
Q_A = N_HEADS_A * HEAD_DIM
KV_A = N_KV_A * HEAD_DIM
GRP_B = HEADS_B * HEAD_DIM
B_W = N_GROUPS_B * GRP_B
OFF_KA = Q_A
OFF_VA = OFF_KA + KV_A
OFF_QB = OFF_VA + KV_A
OFF_KB = OFF_QB + B_W
OFF_VB = OFF_KB + B_W
OFF_GATE = OFF_VB + B_W
QBLK = 128
B_PARTS = 4

VMEM_LIMIT = 56 * 1024 * 1024
TM_PROJ = 512
TM_FINISH = 512


def _rms(x, g):
    return x * lax.rsqrt(jnp.mean(x * x, axis=-1, keepdims=True) + NORM_EPS) * g


def _rope(z, cos, sina, sinb):
    return z * cos + pltpu.roll(z, LANES - HALF, 1) * sina + pltpu.roll(z, HALF, 1) * sinb


def _sigmoid(z):
    return 1.0 / (1.0 + jnp.exp(-z))


def _dup_halves(z, lo):
    r = pltpu.roll(z, HEAD_DIM, 1)
    return jnp.where(lo, z, r), jnp.where(lo, r, z)


def _rope_tables(pos):
    inv = 1.0 / (ROPE_THETA ** (jnp.arange(HALF, dtype=F32) / HALF))
    ang = pos.astype(F32)[:, None] * inv[None, :]
    ang = jnp.concatenate([ang, ang, ang, ang], axis=-1)
    first = (jnp.arange(LANES) % HEAD_DIM) < HALF
    cos = jnp.cos(ang)
    sin = jnp.sin(ang)
    return cos, jnp.where(first, -sin, 0.0), jnp.where(first, 0.0, sin)


def _proj_prompt_kernel(x_ref, g_ref, w_ref, cos_ref, sina_ref, sinb_ref,
                        qa_ref, kad_ref, vad_ref, b1_ref, b2_ref, b3_ref, gate_ref,
                        pa_ref, pb1_ref, pb2_ref, pb3_ref, scr_ref, scr2_ref):
    tm = x_ref.shape[1]
    i = pl.program_id(1)
    last = pl.num_programs(1) - 1
    h = _rms(x_ref[0], g_ref[...]).astype(BF16)
    cos = cos_ref[...]
    sina = sina_ref[...]
    sinb = sinb_ref[...]
    lo = lax.broadcasted_iota(jnp.int32, (tm, LANES), 1) < HEAD_DIM

    wide = 8 * LANES
    chunks = {}

    def mm(ci):
        cols = slice(ci * wide, min((ci + 1) * wide, w_ref.shape[1]))
        chunks[ci] = jnp.dot(h, w_ref[:, cols], preferred_element_type=F32)

    def rope(z):
        return _rope(z, cos, sina, sinb)

    offs = ([c * 2 * LANES for c in range(PAIRS_A // 2)] + [OFF_KA]
            + [o + g * GRP_B for g in range(N_GROUPS_B) for o in (OFF_QB, OFF_KB, OFF_VB)]
            + [OFF_GATE + c * 2 * LANES for c in range(2 * D_MODEL // (2 * LANES))])
    mm(0)

    def take(off):
        nxt = offs.index(off) + 1
        for o in offs[nxt - 1:nxt + 1]:
            if o // wide not in chunks:
                mm(o // wide)
        ci, within = divmod(off, wide)
        z = chunks[ci]
        return z[:, within:within + LANES], z[:, within + LANES:within + 2 * LANES]

    for c in range(PAIRS_A // 2):
        z0, z1 = take(c * 2 * LANES)
        qa_ref[0, 2 * c] = (rope(z0) * SCALE).astype(BF16)
        qa_ref[0, 2 * c + 1] = (rope(z1) * SCALE).astype(BF16)

    zk, zv = take(OFF_KA)
    ka = rope(zk)
    k0, k1 = _dup_halves(ka, lo)
    v0, v1 = _dup_halves(zv, lo)
    kad_ref[0, 0] = k0.astype(BF16)
    kad_ref[0, 1] = k1.astype(BF16)
    for kv, vd in enumerate((v0, v1)):
        vad_ref[0, kv, 0] = jnp.where(lo, vd, 1.0).astype(BF16)
        vad_ref[0, kv, 1] = jnp.where(lo, 1.0, vd).astype(BF16)

    @pl.when(i == last)
    def _():
        pa_ref[0, 0] = ka[tm - WINDOW_A:, :].T
        pa_ref[0, 1] = zv[tm - WINDOW_A:, :].T

    b_refs = (b1_ref, b2_ref, b3_ref)
    for g, (win, dil) in enumerate(DILATED_PAIRS):
        q0, q1 = take(OFF_QB + g * GRP_B)
        k0, k1 = take(OFF_KB + g * GRP_B)
        v0, v1 = take(OFF_VB + g * GRP_B)
        q0 = rope(q0) * SCALE
        q1 = rope(q1) * SCALE
        k0 = rope(k0)
        k1 = rope(k1)
        parts = ((q0, q1), (k0, k1),
                 (jnp.where(lo, v0, 1.0), jnp.where(lo, v1, 1.0)),
                 (jnp.where(lo, 1.0, v0), jnp.where(lo, 1.0, v1)))
        for part in range(B_PARTS):
            for pr in range(PAIRS_B):
                val = parts[part][pr]
                if dil == 1:
                    b_refs[g][0, part, pr, 0] = val.astype(BF16)
                elif dil == 16:
                    slot = part * PAIRS_B + pr
                    scr_ref[slot] = val
                    q4 = tm // 4
                    for r4 in range(4):
                        scr2_ref[slot, r4 * q4:(r4 + 1) * q4, :] = scr_ref[slot, pl.ds(r4, q4, stride=4), :]
                    for r4 in range(4):
                        for r16 in range(4):
                            b_refs[g][0, part, pr, r4 + 4 * r16] = scr2_ref[
                                slot, pl.ds(r4 * q4 + r16, q4 // 4, stride=4), :].astype(BF16)
                else:
                    slot = part * PAIRS_B + pr
                    scr_ref[slot] = val
                    for r in range(dil):
                        b_refs[g][0, part, pr, r] = scr_ref[
                            slot, pl.ds(r, tm // dil, stride=dil), :].astype(BF16)
        kt = jnp.concatenate([k0, k1], axis=-1).T
        vt = jnp.concatenate([v0, v1], axis=-1).T
        if g == 2:
            pb3_ref[0, 0] = kt
            pb3_ref[0, 1] = vt
        else:
            st_ref = pb1_ref if g == 0 else pb2_ref
            rows = st_ref.shape[3]

            @pl.when(i == last)
            def _(kt=kt, vt=vt, st_ref=st_ref, rows=rows):
                st_ref[0, 0] = kt[:, tm - rows:]
                st_ref[0, 1] = vt[:, tm - rows:]

    for c in range(2 * D_MODEL // (2 * LANES)):
        z0, z1 = take(OFF_GATE + c * 2 * LANES)
        gate_ref[0, :, c * 2 * LANES:c * 2 * LANES + LANES] = _sigmoid(z0)
        gate_ref[0, :, c * 2 * LANES + LANES:(c + 1) * 2 * LANES] = _sigmoid(z1)


def _proj_prompt(x, g1, w_in, tables, tm):
    bsz, seq, _ = x.shape
    nt = seq // tm
    cos, sina, sinb = tables
    const2 = lambda b, i: (0, 0)
    tab_spec = pl.BlockSpec((tm, LANES), lambda b, i: (i, 0))
    out_shape = [
        jax.ShapeDtypeStruct((bsz, PAIRS_A, seq, LANES), BF16),
        jax.ShapeDtypeStruct((bsz, N_KV_A, seq, LANES), BF16),
        jax.ShapeDtypeStruct((bsz, N_KV_A, 2, seq, LANES), BF16),
    ]
    out_specs = [
        pl.BlockSpec((1, PAIRS_A, tm, LANES), lambda b, i: (b, 0, i, 0)),
        pl.BlockSpec((1, N_KV_A, tm, LANES), lambda b, i: (b, 0, i, 0)),
        pl.BlockSpec((1, N_KV_A, 2, tm, LANES), lambda b, i: (b, 0, 0, i, 0)),
    ]
    for _, dil in DILATED_PAIRS:
        out_shape.append(jax.ShapeDtypeStruct((bsz, B_PARTS, PAIRS_B, dil, seq // dil, LANES), BF16))
        out_specs.append(pl.BlockSpec((1, B_PARTS, PAIRS_B, dil, tm // dil, LANES),
                                      lambda b, i: (b, 0, 0, 0, i, 0)))
    out_shape.append(jax.ShapeDtypeStruct((bsz, seq, 2 * D_MODEL), F32))
    out_specs.append(pl.BlockSpec((1, tm, 2 * D_MODEL), lambda b, i: (b, i, 0)))
    out_shape.append(jax.ShapeDtypeStruct((bsz, 2, KV_A, min(WINDOW_A, seq)), F32))
    out_specs.append(pl.BlockSpec((1, 2, KV_A, min(WINDOW_A, seq)), lambda b, i: (b, 0, 0, 0)))
    for g, (win, _) in enumerate(DILATED_PAIRS):
        rows = min(win, seq)
        out_shape.append(jax.ShapeDtypeStruct((bsz, 2, GRP_B, rows), F32))
        if g == 2:
            out_specs.append(pl.BlockSpec((1, 2, GRP_B, tm), lambda b, i: (b, 0, 0, i)))
        else:
            out_specs.append(pl.BlockSpec((1, 2, GRP_B, rows), lambda b, i: (b, 0, 0, 0)))
    return pl.pallas_call(
        _proj_prompt_kernel,
        grid=(bsz, nt),
        in_specs=[
            pl.BlockSpec((1, tm, D_MODEL), lambda b, i: (b, i, 0)),
            pl.BlockSpec((1, D_MODEL), const2),
            pl.BlockSpec(w_in.shape, const2, pipeline_mode=pl.Buffered(1)),
            tab_spec, tab_spec, tab_spec,
        ],
        out_specs=out_specs,
        out_shape=out_shape,
        scratch_shapes=[pltpu.VMEM((B_PARTS * PAIRS_B, tm, LANES), F32),
                        pltpu.VMEM((B_PARTS * PAIRS_B, tm, LANES), F32)],
        compiler_params=pltpu.CompilerParams(
            dimension_semantics=("arbitrary", "arbitrary"), vmem_limit_bytes=VMEM_LIMIT),
        name="proj_prompt",
    )(x, g1, w_in, cos, sina, sinb)


def _band_biases():
    qi = lax.broadcasted_iota(jnp.int32, (QBLK, 2 * QBLK), 0)
    ki = lax.broadcasted_iota(jnp.int32, (QBLK, 2 * QBLK), 1)
    band = jnp.where((ki >= qi) & (ki <= qi + SPAN), 0.0, NEG_INF).astype(F32)
    first = jnp.where(ki <= qi, 0.0, NEG_INF).astype(F32)
    return first, band, first[:, :QBLK]


def _head_scores(q2, k, bias, lo, hh):
    qm = jnp.where(lo if hh == 0 else jnp.logical_not(lo), q2, jnp.zeros_like(q2))
    s = lax.dot_general(qm, k, (((1,), (1,)), ((), ())), preferred_element_type=F32)
    return s + bias


def _pair_pv(e0, e1, v_lo, v_hi, lo):
    pv0 = jnp.dot(e0, v_lo, preferred_element_type=F32)
    pv1 = jnp.dot(e1, v_hi, preferred_element_type=F32)
    num = jnp.where(lo, pv0, pv1)
    den = pltpu.roll(jnp.where(lo, pv1, pv0), HEAD_DIM, 1)
    return num, den


def _blk(j):
    return j * QBLK if isinstance(j, int) else pl.multiple_of(j * QBLK, QBLK)


def _win(j):
    return max(j - 1, 0) * QBLK if isinstance(j, int) else pl.multiple_of(jnp.maximum(j - 1, 0) * QBLK, QBLK)


def _attn_a_kernel(sink_ref, qa_ref, kd_ref, va_ref, *rest):
    nw = (len(rest) - 5) // 2
    oa_ref = rest[nw]
    s_scr, e_scr, m_scr, t_scr = rest[2 * nw + 1:]
    for src, dst in zip(rest[:nw], rest[nw + 1:2 * nw + 1]):
        dst[...] = src[...].astype(BF16)
    seq = qa_ref.shape[2]
    nblk = seq // QBLK
    first, band, _ = _band_biases()
    lo = lax.broadcasted_iota(jnp.int32, (QBLK, LANES), 1) < HEAD_DIM
    pairs_per_kv = PAIRS_A // N_KV_A

    def stage_qk(jb, slot, bias):
        for kv in range(N_KV_A):
            k = kd_ref[0, kv, pl.ds(_win(jb), 2 * QBLK), :]
            for pr in range(pairs_per_kv):
                pair = kv * pairs_per_kv + pr
                q2 = qa_ref[0, pair, pl.ds(_blk(jb), QBLK), :]
                for hh in range(2):
                    h = 2 * pair + hh
                    s = _head_scores(q2, k, bias, lo, hh)
                    s_scr[slot, h] = s
                    m = jnp.broadcast_to(jnp.max(s, axis=1, keepdims=True), (QBLK, LANES))
                    m_scr[slot, h] = jnp.maximum(m, sink_ref[h])

    def stage_sm(slot):
        for pair in range(PAIRS_A):
            ms = []
            for hh in range(2):
                h = 2 * pair + hh
                m = m_scr[slot, h]
                e_scr[slot, h] = jnp.exp(s_scr[slot, h] - jnp.tile(m, (1, 2))).astype(BF16)
                ms.append(m)
            sink_pair = jnp.where(lo, sink_ref[2 * pair], sink_ref[2 * pair + 1])
            t_scr[slot, pair] = jnp.exp(sink_pair - jnp.where(lo, ms[0], ms[1]))

    def stage_pv(jb, slot):
        for kv in range(N_KV_A):
            v_lo = va_ref[0, kv, 0, pl.ds(_win(jb), 2 * QBLK), :]
            v_hi = va_ref[0, kv, 1, pl.ds(_win(jb), 2 * QBLK), :]
            for pr in range(pairs_per_kv):
                pair = kv * pairs_per_kv + pr
                num, den = _pair_pv(e_scr[slot, 2 * pair], e_scr[slot, 2 * pair + 1], v_lo, v_hi, lo)
                den = den + t_scr[slot, pair]
                oa_ref[0, pair, pl.ds(_blk(jb), QBLK), :] = (num / den).astype(BF16)

    stage_qk(0, 0, first)
    stage_qk(1, 1, band)
    stage_sm(0)

    def two_blocks(t):
        stage_pv(t - 2, 0)
        stage_qk(t, 0, band)
        stage_sm(1)
        stage_pv(t - 1, 1)
        stage_qk(t + 1, 1, band)
        stage_sm(0)

    def body(i, carry):
        t = 2 + 4 * i
        two_blocks(t)
        two_blocks(t + 2)
        return carry

    nloop = (nblk - 2) // 4
    lax.fori_loop(0, nloop, body, 0)
    for t in range(2 + 4 * nloop, nblk, 2):
        two_blocks(t)
    stage_sm(1)
    stage_pv(nblk - 2, 0)
    stage_pv(nblk - 1, 1)


def _attn_a(sinks, qa, kad, vad, weights):
    bsz, _, seq, _ = qa.shape
    assert (seq // QBLK) % 2 == 0 and seq >= 2 * QBLK
    bf16_rows = 16
    assert all(w.shape[0] % (bsz * bf16_rows) == 0 for w in weights)
    wspecs = [pl.BlockSpec((w.shape[0] // bsz, w.shape[1]), lambda b: (b, 0)) for w in weights]
    return pl.pallas_call(
        _attn_a_kernel,
        grid=(bsz,),
        in_specs=[
            pl.BlockSpec(memory_space=pltpu.SMEM),
            pl.BlockSpec((1, PAIRS_A, seq, LANES), lambda b: (b, 0, 0, 0)),
            pl.BlockSpec((1, N_KV_A, seq, LANES), lambda b: (b, 0, 0, 0)),
            pl.BlockSpec((1, N_KV_A, 2, seq, LANES), lambda b: (b, 0, 0, 0, 0)),
        ] + wspecs,
        out_specs=[pl.BlockSpec((1, PAIRS_A, seq, LANES), lambda b: (b, 0, 0, 0))] + wspecs,
        out_shape=[jax.ShapeDtypeStruct((bsz, PAIRS_A, seq, LANES), BF16)] + [
            jax.ShapeDtypeStruct(w.shape, BF16) for w in weights],
        scratch_shapes=[pltpu.VMEM((2, N_HEADS_A, QBLK, 2 * QBLK), F32),
                        pltpu.VMEM((2, N_HEADS_A, QBLK, 2 * QBLK), BF16),
                        pltpu.VMEM((2, N_HEADS_A, QBLK, LANES), F32),
                        pltpu.VMEM((2, PAIRS_A, QBLK, LANES), F32)],
        compiler_params=pltpu.CompilerParams(
            dimension_semantics=("arbitrary",), vmem_limit_bytes=VMEM_LIMIT),
        name="attn_a",
    )(sinks, qa, kad, vad, *weights)


def _attn_b_kernel(b1_ref, b2_ref, b3_ref, ob_ref, o_scr, lse_scr, s12_scr, s3_scr, e12_scr, e3_scr, m_scr,
                   mp_scr):
    seq = ob_ref.shape[1]
    nunit = seq // QBLK
    first, band, causal = _band_biases()
    lo = lax.broadcasted_iota(jnp.int32, (QBLK, LANES), 1) < HEAD_DIM
    refs = (b1_ref, b2_ref, b3_ref)
    dils = tuple(d for _, d in DILATED_PAIRS)
    nblks = tuple(nunit // d for d in dils)

    def coords(g, u):
        if nblks[g] == 1:
            return u, 0
        if nblks[g] == nunit:
            return 0, u
        return u // nblks[g], u % nblks[g]

    def stage_qk(u, slot):
        for g in range(N_GROUPS_B):
            r, jb = coords(g, u)
            for pr in range(PAIRS_B):
                q2 = refs[g][0, 0, pr, r, pl.ds(_blk(jb), QBLK), :]
                if nblks[g] == 1:
                    k = refs[g][0, 1, pr, r, :, :]
                    bias = causal
                else:
                    k = refs[g][0, 1, pr, r, pl.ds(_win(jb), 2 * QBLK), :]
                    if isinstance(jb, int):
                        bias = first if jb == 0 else band
                    elif nblks[g] == nunit:
                        bias = band
                    else:
                        bias = jnp.where(jb == 0, first, band)
                for hh in range(2):
                    s = _head_scores(q2, k, bias, lo, hh)
                    if nblks[g] == 1:
                        s3_scr[slot, 2 * pr + hh] = s
                    else:
                        s12_scr[slot, g, 2 * pr + hh] = s
                    m_scr[slot, g, 2 * pr + hh] = jnp.broadcast_to(
                        jnp.max(s, axis=1, keepdims=True), (QBLK, LANES))

    def stage_sm(slot):
        for g in range(N_GROUPS_B):
            for h in range(2 * PAIRS_B):
                m = m_scr[slot, g, h]
                if nblks[g] == 1:
                    e3_scr[slot, h] = jnp.exp(s3_scr[slot, h] - m).astype(BF16)
                else:
                    e12_scr[slot, g, h] = jnp.exp(s12_scr[slot, g, h] - jnp.tile(m, (1, 2))).astype(BF16)
            for pr in range(PAIRS_B):
                mp_scr[slot, g, pr] = jnp.where(lo, m_scr[slot, g, 2 * pr], m_scr[slot, g, 2 * pr + 1])

    def stage_pv(u, slot):
        for g in range(N_GROUPS_B):
            r, jb = coords(g, u)
            for pr in range(PAIRS_B):
                if nblks[g] == 1:
                    v_lo = refs[g][0, 2, pr, r, :, :]
                    v_hi = refs[g][0, 3, pr, r, :, :]
                    e0, e1 = e3_scr[slot, 2 * pr], e3_scr[slot, 2 * pr + 1]
                else:
                    v_lo = refs[g][0, 2, pr, r, pl.ds(_win(jb), 2 * QBLK), :]
                    v_hi = refs[g][0, 3, pr, r, pl.ds(_win(jb), 2 * QBLK), :]
                    e0, e1 = e12_scr[slot, g, 2 * pr], e12_scr[slot, g, 2 * pr + 1]
                num, den = _pair_pv(e0, e1, v_lo, v_hi, lo)
                row0 = _blk(jb) * dils[g] + r
                idx = pl.ds(row0, QBLK) if dils[g] == 1 else pl.ds(row0, QBLK, stride=dils[g])
                o_scr[g, pr, idx, :] = num / den
                lse_scr[g, pr, idx, :] = mp_scr[slot, g, pr] + jnp.log(den)

    stage_qk(0, 0)
    stage_qk(1, 1)
    stage_sm(0)

    def two_units(u):
        stage_qk(u, 0)
        stage_sm(1)
        stage_pv(u - 2, 0)
        stage_qk(u + 1, 1)
        stage_sm(0)
        stage_pv(u - 1, 1)

    def body(i, carry):
        u = 2 + 4 * i
        two_units(u)
        two_units(u + 2)
        return carry

    nloop = (nunit - 2) // 4
    lax.fori_loop(0, nloop, body, 0)
    for u in range(2 + 4 * nloop, nunit, 2):
        two_units(u)
    stage_sm(1)
    stage_pv(nunit - 2, 0)
    stage_pv(nunit - 1, 1)

    mrows = 2 * QBLK

    def merge(c, carry):
        rows = pl.ds(pl.multiple_of(c * mrows, mrows), mrows)
        for pr in range(PAIRS_B):
            ls = [lse_scr[g, pr, rows, :] for g in range(N_GROUPS_B)]
            mx = jnp.maximum(jnp.maximum(ls[0], ls[1]), ls[2])
            ws = [jnp.exp(l - mx) for l in ls]
            num = ws[0] * o_scr[0, pr, rows, :] + ws[1] * o_scr[1, pr, rows, :] + ws[2] * o_scr[2, pr, rows, :]
            den = ws[0] + ws[1] + ws[2]
            ob_ref[0, rows, pr * LANES:(pr + 1) * LANES] = (num / den).astype(BF16)
        return carry

    lax.fori_loop(0, seq // mrows, merge, 0)


def _attn_b(b1, b2, b3):
    bsz = b1.shape[0]
    seq = b1.shape[4]
    nunit = seq // QBLK
    assert nunit % 2 == 0 and nunit >= 4 and nunit == DILATED_PAIRS[-1][1]
    assert b1.shape[3:5] == (1, seq) and b3.shape[4] == QBLK
    specs = [pl.BlockSpec((1,) + b.shape[1:], lambda b_: (b_, 0, 0, 0, 0, 0)) for b in (b1, b2, b3)]
    heads = 2 * PAIRS_B
    return pl.pallas_call(
        _attn_b_kernel,
        grid=(bsz,),
        in_specs=specs,
        out_specs=pl.BlockSpec((1, seq, GRP_B), lambda b_: (b_, 0, 0)),
        out_shape=jax.ShapeDtypeStruct((bsz, seq, GRP_B), BF16),
        scratch_shapes=[pltpu.VMEM((N_GROUPS_B, PAIRS_B, seq, LANES), F32),
                        pltpu.VMEM((N_GROUPS_B, PAIRS_B, seq, LANES), F32),
                        pltpu.VMEM((2, 2, heads, QBLK, 2 * QBLK), F32),
                        pltpu.VMEM((2, heads, QBLK, QBLK), F32),
                        pltpu.VMEM((2, 2, heads, QBLK, 2 * QBLK), BF16),
                        pltpu.VMEM((2, heads, QBLK, QBLK), BF16),
                        pltpu.VMEM((2, N_GROUPS_B, heads, QBLK, LANES), F32),
                        pltpu.VMEM((2, N_GROUPS_B, PAIRS_B, QBLK, LANES), F32)],
        compiler_params=pltpu.CompilerParams(
            dimension_semantics=("arbitrary",), vmem_limit_bytes=VMEM_LIMIT),
        name="attn_b",
    )(b1, b2, b3)


def _finish_kernel(x_ref, oa_ref, ob_ref, gate_ref, wpa_ref, wpb_ref, wo_ref, g2_ref,
                   wgu_ref, wdn_ref, gf_ref, y_ref):
    d_ff = wdn_ref.shape[0]
    oa = jnp.concatenate([oa_ref[0, p] for p in range(PAIRS_A)], axis=-1)
    ma = jnp.dot(oa, wpa_ref[...], preferred_element_type=F32)
    mb = jnp.dot(ob_ref[0], wpb_ref[...], preferred_element_type=F32)
    m = gate_ref[0, :, :D_MODEL] * ma + gate_ref[0, :, D_MODEL:] * mb
    x1 = x_ref[0] + jnp.dot(m.astype(BF16), wo_ref[...], preferred_element_type=F32)
    xn = _rms(x1, g2_ref[...]).astype(BF16)
    gu = jnp.dot(xn, wgu_ref[...], preferred_element_type=F32)
    gg = gu[:, :d_ff]
    act = (gg * _sigmoid(gg) * gu[:, d_ff:]).astype(BF16)
    y_ref[0] = _rms(x1 + jnp.dot(act, wdn_ref[...], preferred_element_type=F32), gf_ref[...])


def _finish(x, oa, ob, gate, w_pa, w_pb, w_o, g2, w_gu, w_down, gf, tm):
    bsz, seq, _ = x.shape
    const2 = lambda b, i: (0, 0)
    wspec = lambda w: pl.BlockSpec(w.shape, const2, pipeline_mode=pl.Buffered(1))
    return pl.pallas_call(
        _finish_kernel,
        grid=(bsz, seq // tm),
        in_specs=[
            pl.BlockSpec((1, tm, D_MODEL), lambda b, i: (b, i, 0)),
            pl.BlockSpec((1, PAIRS_A, tm, LANES), lambda b, i: (b, 0, i, 0)),
            pl.BlockSpec((1, tm, GRP_B), lambda b, i: (b, i, 0)),
            pl.BlockSpec((1, tm, 2 * D_MODEL), lambda b, i: (b, i, 0)),
            wspec(w_pa), wspec(w_pb), wspec(w_o),
            pl.BlockSpec((1, D_MODEL), const2),
            wspec(w_gu), wspec(w_down),
            pl.BlockSpec((1, D_MODEL), const2),
        ],
        out_specs=pl.BlockSpec((1, tm, D_MODEL), lambda b, i: (b, i, 0)),
        out_shape=jax.ShapeDtypeStruct((bsz, seq, D_MODEL), F32),
        compiler_params=pltpu.CompilerParams(
            dimension_semantics=("arbitrary", "arbitrary"), vmem_limit_bytes=VMEM_LIMIT),
        name="finish",
    )(x, oa, ob, gate, w_pa, w_pb, w_o, g2, w_gu, w_down, gf)


ZS_QA = 0
ZS_QB = ZS_QA + Q_A
ZS_KA = ZS_QB + B_W
ZS_VA = ZS_KA + N_KV_A * GROUP_A * HEAD_DIM
ZS_KB = ZS_VA + N_KV_A * GROUP_A * HEAD_DIM
ZS_VB = ZS_KB + B_W
ZS_W = ZS_VB + B_W
KT_KA = 0
KT_VA = KT_KA + KV_A
KT_KB = KT_VA + KV_A
KT_VB = KT_KB + B_W
KT_W = KT_VB + B_W


def _proj_sample_kernel(x_ref, g_ref, w_ref, cos_ref, sina_ref, sinb_ref, zs_ref, kvt_ref, gate_ref):
    nb = x_ref.shape[0]
    h = _rms(x_ref[...], g_ref[...]).astype(BF16)
    cos = cos_ref[...]
    sina = sina_ref[...]
    sinb = sinb_ref[...]
    lo = lax.broadcasted_iota(jnp.int32, (nb, LANES), 1) < HEAD_DIM

    def mm(off):
        z = jnp.dot(h, w_ref[:, off:off + 2 * LANES], preferred_element_type=F32)
        return z[:, :LANES], z[:, LANES:]

    def rope(z):
        return _rope(z, cos, sina, sinb)

    def put(off, val):
        zs_ref[:, off:off + LANES] = val

    for c in range(PAIRS_A // 2):
        z0, z1 = mm(c * 2 * LANES)
        put(ZS_QA + c * 2 * LANES, rope(z0) * SCALE)
        put(ZS_QA + c * 2 * LANES + LANES, rope(z1) * SCALE)
    for c in range(B_W // (2 * LANES)):
        z0, z1 = mm(OFF_QB + c * 2 * LANES)
        put(ZS_QB + c * 2 * LANES, rope(z0) * SCALE)
        put(ZS_QB + c * 2 * LANES + LANES, rope(z1) * SCALE)

    zk, zv = mm(OFF_KA)
    ka = rope(zk)
    kvt_ref[KT_KA:KT_KA + KV_A, :] = ka.T
    kvt_ref[KT_VA:KT_VA + KV_A, :] = zv.T
    reps = GROUP_A * HEAD_DIM // LANES
    for base, val in ((ZS_KA, ka), (ZS_VA, zv)):
        d0, d1 = _dup_halves(val, lo)
        for kv, dup in enumerate((d0, d1)):
            for c in range(reps):
                put(base + (kv * reps + c) * LANES, dup)

    for c in range(B_W // (2 * LANES)):
        k0, k1 = mm(OFF_KB + c * 2 * LANES)
        v0, v1 = mm(OFF_VB + c * 2 * LANES)
        k0 = rope(k0)
        k1 = rope(k1)
        for j, (kk, vv) in enumerate(((k0, v0), (k1, v1))):
            off = (2 * c + j) * LANES
            put(ZS_KB + off, kk)
            put(ZS_VB + off, vv)
            kvt_ref[KT_KB + off:KT_KB + off + LANES, :] = kk.T
            kvt_ref[KT_VB + off:KT_VB + off + LANES, :] = vv.T

    for c in range(2 * D_MODEL // (2 * LANES)):
        z0, z1 = mm(OFF_GATE + c * 2 * LANES)
        gate_ref[:, c * 2 * LANES:c * 2 * LANES + LANES] = _sigmoid(z0)
        gate_ref[:, c * 2 * LANES + LANES:(c + 1) * 2 * LANES] = _sigmoid(z1)


def _proj_sample(x, g1, w_in, tables):
    nb = x.shape[0]
    cos, sina, sinb = tables
    return pl.pallas_call(
        _proj_sample_kernel,
        out_shape=[
            jax.ShapeDtypeStruct((nb, ZS_W), F32),
            jax.ShapeDtypeStruct((KT_W, nb), F32),
            jax.ShapeDtypeStruct((nb, 2 * D_MODEL), F32),
        ],
        compiler_params=pltpu.CompilerParams(vmem_limit_bytes=VMEM_LIMIT),
        name="proj_sample",
    )(x, g1, w_in, cos, sina, sinb)


SEQS_PER_STEP = 2


def _sample_attn_kernel(zs_ref, kvt_ref, sink_ref, ca_ref, c1_ref, c2_ref, c3_ref,
                        sa_ref, s1_ref, s2_ref, s3_ref, oa_ref, ob_ref):
    for j in range(ca_ref.shape[0]):
        _sample_attn_one(j, zs_ref, kvt_ref, sink_ref, ca_ref, c1_ref, c2_ref, c3_ref,
                         sa_ref, s1_ref, s2_ref, s3_ref, oa_ref, ob_ref)


def _sample_attn_one(j, zs_ref, kvt_ref, sink_ref, ca_ref, c1_ref, c2_ref, c3_ref,
                     sa_ref, s1_ref, s2_ref, s3_ref, oa_ref, ob_ref):
    b = pl.program_id(0) * ca_ref.shape[0] + j
    nb = kvt_ref.shape[1]
    pick = lax.broadcasted_iota(jnp.int32, (1, nb), 1) == b

    def row(off, width):
        return zs_ref[pl.ds(b, 1), off:off + width]

    def new_col(off, width):
        blk = kvt_ref[off:off + width, :]
        return jnp.sum(jnp.where(pick, blk, 0.0), axis=1, keepdims=True)

    def shifted(tile, col):
        w = tile.shape[1]
        lane = lax.broadcasted_iota(jnp.int32, tile.shape, 1)
        return jnp.where(lane == w - 1, col, pltpu.roll(tile, w - 1, 1))

    def head_mask(nh):
        rows = max(nh, 8)
        hid = lax.broadcasted_iota(jnp.int32, (rows, nh * HEAD_DIM), 0)
        lh = lax.broadcasted_iota(jnp.int32, (rows, nh * HEAD_DIM), 1) // HEAD_DIM
        return hid == lh

    def attend(q_row, kt, vt, knew_row, vnew_row, hm, dil, sink_col):
        qm = jnp.where(hm, q_row, 0.0)
        s = jnp.dot(qm.astype(BF16), kt.astype(BF16), preferred_element_type=F32)
        if dil > 1:
            lane = lax.broadcasted_iota(jnp.int32, s.shape, 1)
            s = jnp.where(lane % dil == 0, s, NEG_INF)
        s_new = jnp.sum(qm * knew_row, axis=1, keepdims=True)
        m = jnp.maximum(jnp.max(s, axis=1, keepdims=True), s_new)
        if sink_col is not None:
            m = jnp.maximum(m, sink_col)
        e = jnp.exp(s - m)
        e_new = jnp.exp(s_new - m)
        l = jnp.sum(e, axis=1, keepdims=True) + e_new
        if sink_col is not None:
            l = l + jnp.exp(sink_col - m)
        num = lax.dot_general(e.astype(BF16), vt.astype(BF16), (((1,), (1,)), ((), ())),
                              preferred_element_type=F32)
        return num + e_new * vnew_row, m, l

    hm_a = head_mask(GROUP_A)
    wq = GROUP_A * HEAD_DIM
    for kv in range(N_KV_A):
        kt = ca_ref[j, 0, kv * HEAD_DIM:(kv + 1) * HEAD_DIM, :]
        vt = ca_ref[j, 1, kv * HEAD_DIM:(kv + 1) * HEAD_DIM, :]
        num, m, l = attend(row(ZS_QA + kv * wq, wq),
                           jnp.concatenate([kt] * GROUP_A, axis=0),
                           jnp.concatenate([vt] * GROUP_A, axis=0),
                           row(ZS_KA + kv * wq, wq), row(ZS_VA + kv * wq, wq),
                           hm_a, 1, sink_ref[kv * GROUP_A:(kv + 1) * GROUP_A, :])
        o = jnp.sum(jnp.where(hm_a, num / l, 0.0), axis=0, keepdims=True)
        oa_ref[pl.ds(b, 1), kv * wq:(kv + 1) * wq] = o
    sa_ref[j, 0] = shifted(ca_ref[j, 0], new_col(KT_KA, KV_A))
    sa_ref[j, 1] = shifted(ca_ref[j, 1], new_col(KT_VA, KV_A))

    hm_b = head_mask(HEADS_B)
    nums, ms, ls = [], [], []
    for g, (c_ref, s_ref, (_, dil)) in enumerate(zip((c1_ref, c2_ref, c3_ref), (s1_ref, s2_ref, s3_ref),
                                                      DILATED_PAIRS)):
        kt = c_ref[j, 0]
        vt = c_ref[j, 1]
        num, m, l = attend(row(ZS_QB + g * GRP_B, GRP_B), kt, vt,
                           row(ZS_KB + g * GRP_B, GRP_B), row(ZS_VB + g * GRP_B, GRP_B),
                           hm_b, dil, None)
        nums.append(num)
        ms.append(m)
        ls.append(l)
        s_ref[j, 0] = shifted(kt, new_col(KT_KB + g * GRP_B, GRP_B))
        s_ref[j, 1] = shifted(vt, new_col(KT_VB + g * GRP_B, GRP_B))
    mx = jnp.maximum(jnp.maximum(ms[0], ms[1]), ms[2])
    ws = [jnp.exp(m - mx) for m in ms]
    num = ws[0] * nums[0] + ws[1] * nums[1] + ws[2] * nums[2]
    den = ws[0] * ls[0] + ws[1] * ls[1] + ws[2] * ls[2]
    ob_ref[pl.ds(b, 1), :] = jnp.sum(jnp.where(hm_b, num / den, 0.0), axis=0, keepdims=True)


def _sample_attn(zs, kvt, sinks_col, caches):
    nb = zs.shape[0]
    full2 = lambda a: pl.BlockSpec(a.shape, lambda b: (0, 0))
    cspec = lambda a: pl.BlockSpec((SEQS_PER_STEP,) + a.shape[1:], lambda b: (b, 0, 0, 0))
    return pl.pallas_call(
        _sample_attn_kernel,
        grid=(nb // SEQS_PER_STEP,),
        in_specs=[full2(zs), full2(kvt), full2(sinks_col)] + [cspec(c) for c in caches],
        out_specs=[cspec(c) for c in caches] + [
            pl.BlockSpec((nb, Q_A), lambda b: (0, 0)),
            pl.BlockSpec((nb, GRP_B), lambda b: (0, 0)),
        ],
        out_shape=[jax.ShapeDtypeStruct(c.shape, F32) for c in caches] + [
            jax.ShapeDtypeStruct((nb, Q_A), F32),
            jax.ShapeDtypeStruct((nb, GRP_B), F32),
        ],
        compiler_params=pltpu.CompilerParams(
            dimension_semantics=("arbitrary",), vmem_limit_bytes=VMEM_LIMIT),
        name="sample_attn",
    )(zs, kvt, sinks_col, *caches)


def _rows_minor(c):
    _, bsz, rows, two, nh, hd = c.shape
    return jnp.transpose(c, (0, 1, 3, 4, 5, 2)).reshape(bsz, two, nh * hd, rows)


def _rows_major(c, nh):
    bsz, two, _, rows = c.shape
    return jnp.transpose(c.reshape(1, bsz, two, nh, HEAD_DIM, rows), (0, 1, 5, 2, 3, 4))


def kernel(x_prompt, x_sample, cache_a_kv, cache_b1_kv, cache_b2_kv, cache_b3_kv, norm1_g, w_in, sinks,
           w_pa, w_pb, w_o, norm2_g, w_gu, w_down, final_norm_g):
    bsz, seq, _ = x_prompt.shape
    nb, dec_seq, _ = x_sample.shape
    assert dec_seq == 1 and norm1_g.shape[0] == 1
    assert cache_a_kv.shape[2] == WINDOW_A
    assert tuple(c.shape[2] for c in (cache_b1_kv, cache_b2_kv, cache_b3_kv)) == tuple(
        w for w, _ in DILATED_PAIRS)

    w_in_b = w_in[0].astype(BF16)
    gf = final_norm_g.reshape(1, D_MODEL)

    tables_p = _rope_tables(jnp.arange(seq, dtype=jnp.int32))
    (qa, kad, vad, b1, b2, b3, gate, pa, pb1, pb2, pb3) = _proj_prompt(
        x_prompt, norm1_g, w_in_b, tables_p, TM_PROJ)
    oa, w_pa_b, w_pb_b, w_o_b, w_gu_b, w_dn_b = _attn_a(
        sinks[0], qa, kad, vad, (w_pa[0], w_pb[0], w_o[0], w_gu[0], w_down[0]))
    ob = _attn_b(b1, b2, b3)
    y_prompt = _finish(x_prompt, oa, ob, gate, w_pa_b, w_pb_b, w_o_b, norm2_g, w_gu_b, w_dn_b, gf, TM_FINISH)

    xs = x_sample.reshape(nb, D_MODEL)
    tables_s = _rope_tables(jnp.full((nb,), PAST_LEN, dtype=jnp.int32))
    zs, kvt, gate_s = _proj_sample(xs, norm1_g, w_in_b, tables_s)
    caches = [_rows_minor(c) for c in (cache_a_kv, cache_b1_kv, cache_b2_kv, cache_b3_kv)]
    sa, sb1, sb2, sb3, oa_s, ob_s = _sample_attn(zs, kvt, sinks[0].reshape(N_HEADS_A, 1), caches)
    oa_s = jnp.transpose(oa_s.reshape(nb, PAIRS_A, LANES), (1, 0, 2))[None].astype(BF16)
    y_sample = _finish(xs[None], oa_s, ob_s[None].astype(BF16), gate_s[None], w_pa_b, w_pb_b, w_o_b,
                       norm2_g, w_gu_b, w_dn_b, gf, nb)

    return (y_prompt, y_sample.reshape(nb, 1, D_MODEL),
            _rows_major(pa, N_KV_A), _rows_major(pb1, HEADS_B), _rows_major(pb2, HEADS_B),
            _rows_major(pb3, HEADS_B),
            _rows_major(sa, N_KV_A), _rows_major(sb1, HEADS_B), _rows_major(sb2, HEADS_B),
            _rows_major(sb3, HEADS_B))
```

```python
import jax
import jax.numpy as jnp
from jax import lax
from jax.experimental import pallas as pl
from jax.experimental.pallas import tpu as pltpu

F32 = jnp.float32
BF16 = jnp.bfloat16

D_MODEL = 1024
HEAD_DIM = 64
HALF = HEAD_DIM // 2
N_HEADS_A = 16
N_KV_A = 2
GROUP_A = N_HEADS_A // N_KV_A
WINDOW_A = 128
DILATED_PAIRS = ((128, 1), (512, 4), (2048, 16))
N_GROUPS_B = 3
HEADS_B = 4
SPAN = 128
PAST_LEN = 16384
ROPE_THETA = 10000.0
NORM_EPS = 1e-6
NEG_INF = -1e30
SCALE = HEAD_DIM ** -0.5

LANES = 128
PAIRS_A = N_HEADS_A // 2
PAIRS_B = HEADS_B // 2
Q_A = N_HEADS_A * HEAD_DIM
KV_A = N_KV_A * HEAD_DIM
GRP_B = HEADS_B * HEAD_DIM
B_W = N_GROUPS_B * GRP_B
OFF_KA = Q_A
OFF_VA = OFF_KA + KV_A
OFF_QB = OFF_VA + KV_A
OFF_KB = OFF_QB + B_W
OFF_VB = OFF_KB + B_W
OFF_GATE = OFF_VB + B_W
QBLK = 128
B_PARTS = 4

VMEM_LIMIT = 56 * 1024 * 1024
TM_PROJ = 512
TM_FINISH = 512


def _rms(x, g):
    return x * lax.rsqrt(jnp.mean(x * x, axis=-1, keepdims=True) + NORM_EPS) * g


def _rope(z, cos, sina, sinb):
    return z * cos + pltpu.roll(z, LANES - HALF, 1) * sina + pltpu.roll(z, HALF, 1) * sinb


def _sigmoid(z):
    return 1.0 / (1.0 + jnp.exp(-z))


def _dup_halves(z, lo):
    r = pltpu.roll(z, HEAD_DIM, 1)
    return jnp.where(lo, z, r), jnp.where(lo, r, z)


def _rope_tables(pos):
    inv = 1.0 / (ROPE_THETA ** (jnp.arange(HALF, dtype=F32) / HALF))
    ang = pos.astype(F32)[:, None] * inv[None, :]
    ang = jnp.concatenate([ang, ang, ang, ang], axis=-1)
    first = (jnp.arange(LANES) % HEAD_DIM) < HALF
    cos = jnp.cos(ang)
    sin = jnp.sin(ang)
    return cos, jnp.where(first, -sin, 0.0), jnp.where(first, 0.0, sin)


def _proj_prompt_kernel(x_ref, g_ref, w_ref, cos_ref, sina_ref, sinb_ref,
                        qa_ref, kad_ref, vad_ref, b1_ref, b2_ref, b3_ref, gate_ref,
                        pa_ref, pb1_ref, pb2_ref, pb3_ref, scr_ref, scr2_ref):
    tm = x_ref.shape[1]
    i = pl.program_id(1)
    last = pl.num_programs(1) - 1
    h = _rms(x_ref[0], g_ref[...]).astype(BF16)
    cos = cos_ref[...]
    sina = sina_ref[...]
    sinb = sinb_ref[...]
    lo = lax.broadcasted_iota(jnp.int32, (tm, LANES), 1) < HEAD_DIM

    wide = 8 * LANES
    chunks = {}

    def mm(ci):
        cols = slice(ci * wide, min((ci + 1) * wide, w_ref.shape[1]))
        chunks[ci] = jnp.dot(h, w_ref[:, cols], preferred_element_type=F32)

    def rope(z):
        return _rope(z, cos, sina, sinb)

    offs = ([c * 2 * LANES for c in range(PAIRS_A // 2)] + [OFF_KA]
            + [o + g * GRP_B for g in range(N_GROUPS_B) for o in (OFF_QB, OFF_KB, OFF_VB)]
            + [OFF_GATE + c * 2 * LANES for c in range(2 * D_MODEL // (2 * LANES))])
    mm(0)

    def take(off):
        nxt = offs.index(off) + 1
        for o in offs[nxt - 1:nxt + 1]:
            if o // wide not in chunks:
                mm(o // wide)
        ci, within = divmod(off, wide)
        z = chunks[ci]
        return z[:, within:within + LANES], z[:, within + LANES:within + 2 * LANES]

    for c in range(PAIRS_A // 2):
        z0, z1 = take(c * 2 * LANES)
        qa_ref[0, 2 * c] = (rope(z0) * SCALE).astype(BF16)
        qa_ref[0, 2 * c + 1] = (rope(z1) * SCALE).astype(BF16)

    zk, zv = take(OFF_KA)
    ka = rope(zk)
    k0, k1 = _dup_halves(ka, lo)
    v0, v1 = _dup_halves(zv, lo)
    kad_ref[0, 0] = k0.astype(BF16)
    kad_ref[0, 1] = k1.astype(BF16)
    for kv, vd in enumerate((v0, v1)):
        vad_ref[0, kv, 0] = jnp.where(lo, vd, 1.0).astype(BF16)
        vad_ref[0, kv, 1] = jnp.where(lo, 1.0, vd).astype(BF16)

    @pl.when(i == last)
    def _():
        pa_ref[0, 0] = ka[tm - WINDOW_A:, :].T
        pa_ref[0, 1] = zv[tm - WINDOW_A:, :].T

    b_refs = (b1_ref, b2_ref, b3_ref)
    for g, (win, dil) in enumerate(DILATED_PAIRS):
        q0, q1 = take(OFF_QB + g * GRP_B)
        k0, k1 = take(OFF_KB + g * GRP_B)
        v0, v1 = take(OFF_VB + g * GRP_B)
        q0 = rope(q0) * SCALE
        q1 = rope(q1) * SCALE
        k0 = rope(k0)
        k1 = rope(k1)
        parts = ((q0, q1), (k0, k1),
                 (jnp.where(lo, v0, 1.0), jnp.where(lo, v1, 1.0)),
                 (jnp.where(lo, 1.0, v0), jnp.where(lo, 1.0, v1)))
        for part in range(B_PARTS):
            for pr in range(PAIRS_B):
                val = parts[part][pr]
                if dil == 1:
                    b_refs[g][0, part, pr, 0] = val.astype(BF16)
                elif dil == 16:
                    slot = part * PAIRS_B + pr
                    scr_ref[slot] = val
                    q4 = tm // 4
                    for r4 in range(4):
                        scr2_ref[slot, r4 * q4:(r4 + 1) * q4, :] = scr_ref[slot, pl.ds(r4, q4, stride=4), :]
                    for r4 in range(4):
                        for r16 in range(4):
                            b_refs[g][0, part, pr, r4 + 4 * r16] = scr2_ref[
                                slot, pl.ds(r4 * q4 + r16, q4 // 4, stride=4), :].astype(BF16)
                else:
                    slot = part * PAIRS_B + pr
                    scr_ref[slot] = val
                    for r in range(dil):
                        b_refs[g][0, part, pr, r] = scr_ref[
                            slot, pl.ds(r, tm // dil, stride=dil), :].astype(BF16)
        kt = jnp.concatenate([k0, k1], axis=-1).T
        vt = jnp.concatenate([v0, v1], axis=-1).T
        if g == 2:
            pb3_ref[0, 0] = kt
            pb3_ref[0, 1] = vt
        else:
            st_ref = pb1_ref if g == 0 else pb2_ref
            rows = st_ref.shape[3]

            @pl.when(i == last)
            def _(kt=kt, vt=vt, st_ref=st_ref, rows=rows):
                st_ref[0, 0] = kt[:, tm - rows:]
                st_ref[0, 1] = vt[:, tm - rows:]

    for c in range(2 * D_MODEL // (2 * LANES)):
        z0, z1 = take(OFF_GATE + c * 2 * LANES)
        gate_ref[0, :, c * 2 * LANES:c * 2 * LANES + LANES] = _sigmoid(z0)
        gate_ref[0, :, c * 2 * LANES + LANES:(c + 1) * 2 * LANES] = _sigmoid(z1)


def _proj_prompt(x, g1, w_in, tables, tm):
    bsz, seq, _ = x.shape
    nt = seq // tm
    cos, sina, sinb = tables
    const2 = lambda b, i: (0, 0)
    tab_spec = pl.BlockSpec((tm, LANES), lambda b, i: (i, 0))
    out_shape = [
        jax.ShapeDtypeStruct((bsz, PAIRS_A, seq, LANES), BF16),
        jax.ShapeDtypeStruct((bsz, N_KV_A, seq, LANES), BF16),
        jax.ShapeDtypeStruct((bsz, N_KV_A, 2, seq, LANES), BF16),
    ]
    out_specs = [
        pl.BlockSpec((1, PAIRS_A, tm, LANES), lambda b, i: (b, 0, i, 0)),
        pl.BlockSpec((1, N_KV_A, tm, LANES), lambda b, i: (b, 0, i, 0)),
        pl.BlockSpec((1, N_KV_A, 2, tm, LANES), lambda b, i: (b, 0, 0, i, 0)),
    ]
    for _, dil in DILATED_PAIRS:
        out_shape.append(jax.ShapeDtypeStruct((bsz, B_PARTS, PAIRS_B, dil, seq // dil, LANES), BF16))
        out_specs.append(pl.BlockSpec((1, B_PARTS, PAIRS_B, dil, tm // dil, LANES),
                                      lambda b, i: (b, 0, 0, 0, i, 0)))
    out_shape.append(jax.ShapeDtypeStruct((bsz, seq, 2 * D_MODEL), F32))
    out_specs.append(pl.BlockSpec((1, tm, 2 * D_MODEL), lambda b, i: (b, i, 0)))
    out_shape.append(jax.ShapeDtypeStruct((bsz, 2, KV_A, min(WINDOW_A, seq)), F32))
    out_specs.append(pl.BlockSpec((1, 2, KV_A, min(WINDOW_A, seq)), lambda b, i: (b, 0, 0, 0)))
    for g, (win, _) in enumerate(DILATED_PAIRS):
        rows = min(win, seq)
        out_shape.append(jax.ShapeDtypeStruct((bsz, 2, GRP_B, rows), F32))
        if g == 2:
            out_specs.append(pl.BlockSpec((1, 2, GRP_B, tm), lambda b, i: (b, 0, 0, i)))
        else:
            out_specs.append(pl.BlockSpec((1, 2, GRP_B, rows), lambda b, i: (b, 0, 0, 0)))
    return pl.pallas_call(
        _proj_prompt_kernel,
        grid=(bsz, nt),
        in_specs=[
            pl.BlockSpec((1, tm, D_MODEL), lambda b, i: (b, i, 0)),
            pl.BlockSpec((1, D_MODEL), const2),
            pl.BlockSpec(w_in.shape, const2, pipeline_mode=pl.Buffered(1)),
            tab_spec, tab_spec, tab_spec,
        ],
        out_specs=out_specs,
        out_shape=out_shape,
        scratch_shapes=[pltpu.VMEM((B_PARTS * PAIRS_B, tm, LANES), F32),
                        pltpu.VMEM((B_PARTS * PAIRS_B, tm, LANES), F32)],
        compiler_params=pltpu.CompilerParams(
            dimension_semantics=("arbitrary", "arbitrary"), vmem_limit_bytes=VMEM_LIMIT),
        name="proj_prompt",
    )(x, g1, w_in, cos, sina, sinb)


def _band_biases():
    qi = lax.broadcasted_iota(jnp.int32, (QBLK, 2 * QBLK), 0)
    ki = lax.broadcasted_iota(jnp.int32, (QBLK, 2 * QBLK), 1)
    band = jnp.where((ki >= qi) & (ki <= qi + SPAN), 0.0, NEG_INF).astype(F32)
    first = jnp.where(ki <= qi, 0.0, NEG_INF).astype(F32)
    return first, band, first[:, :QBLK]


def _head_scores(q2, k, bias, lo, hh):
    qm = jnp.where(lo if hh == 0 else jnp.logical_not(lo), q2, jnp.zeros_like(q2))
    s = lax.dot_general(qm, k, (((1,), (1,)), ((), ())), preferred_element_type=F32)
    return s + bias


def _pair_pv(e0, e1, v_lo, v_hi, lo):
    pv0 = jnp.dot(e0, v_lo, preferred_element_type=F32)
    pv1 = jnp.dot(e1, v_hi, preferred_element_type=F32)
    num = jnp.where(lo, pv0, pv1)
    den = pltpu.roll(jnp.where(lo, pv1, pv0), HEAD_DIM, 1)
    return num, den


def _blk(j):
    return j * QBLK if isinstance(j, int) else pl.multiple_of(j * QBLK, QBLK)


def _win(j):
    return max(j - 1, 0) * QBLK if isinstance(j, int) else pl.multiple_of(jnp.maximum(j - 1, 0) * QBLK, QBLK)


def _attn_a_kernel(sink_ref, qa_ref, kd_ref, va_ref, *rest):
    nw = (len(rest) - 5) // 2
    oa_ref = rest[nw]
    s_scr, e_scr, m_scr, t_scr = rest[2 * nw + 1:]
    for src, dst in zip(rest[:nw], rest[nw + 1:2 * nw + 1]):
        dst[...] = src[...].astype(BF16)
    seq = qa_ref.shape[2]
    nblk = seq // QBLK
    first, band, _ = _band_biases()
    lo = lax.broadcasted_iota(jnp.int32, (QBLK, LANES), 1) < HEAD_DIM
    pairs_per_kv = PAIRS_A // N_KV_A

    def stage_qk(jb, slot, bias):
        for kv in range(N_KV_A):
            k = kd_ref[0, kv, pl.ds(_win(jb), 2 * QBLK), :]
            for pr in range(pairs_per_kv):
                pair = kv * pairs_per_kv + pr
                q2 = qa_ref[0, pair, pl.ds(_blk(jb), QBLK), :]
                for hh in range(2):
                    h = 2 * pair + hh
                    s = _head_scores(q2, k, bias, lo, hh)
                    s_scr[slot, h] = s
                    m = jnp.broadcast_to(jnp.max(s, axis=1, keepdims=True), (QBLK, LANES))
                    m_scr[slot, h] = jnp.maximum(m, sink_ref[h])

    def stage_sm(slot):
        for pair in range(PAIRS_A):
            ms = []
            for hh in range(2):
                h = 2 * pair + hh
                m = m_scr[slot, h]
                e_scr[slot, h] = jnp.exp(s_scr[slot, h] - jnp.tile(m, (1, 2))).astype(BF16)
                ms.append(m)
            sink_pair = jnp.where(lo, sink_ref[2 * pair], sink_ref[2 * pair + 1])
            t_scr[slot, pair] = jnp.exp(sink_pair - jnp.where(lo, ms[0], ms[1]))

    def stage_pv(jb, slot):
        for kv in range(N_KV_A):
            v_lo = va_ref[0, kv, 0, pl.ds(_win(jb), 2 * QBLK), :]
            v_hi = va_ref[0, kv, 1, pl.ds(_win(jb), 2 * QBLK), :]
            for pr in range(pairs_per_kv):
                pair = kv * pairs_per_kv + pr
                num, den = _pair_pv(e_scr[slot, 2 * pair], e_scr[slot, 2 * pair + 1], v_lo, v_hi, lo)
                den = den + t_scr[slot, pair]
                oa_ref[0, pair, pl.ds(_blk(jb), QBLK), :] = (num / den).astype(BF16)

    stage_qk(0, 0, first)
    stage_qk(1, 1, band)
    stage_sm(0)

    def two_blocks(t):
        stage_pv(t - 2, 0)
        stage_qk(t, 0, band)
        stage_sm(1)
        stage_pv(t - 1, 1)
        stage_qk(t + 1, 1, band)
        stage_sm(0)

    def body(i, carry):
        t = 2 + 4 * i
        two_blocks(t)
        two_blocks(t + 2)
        return carry

    nloop = (nblk - 2) // 4
    lax.fori_loop(0, nloop, body, 0)
    for t in range(2 + 4 * nloop, nblk, 2):
        two_blocks(t)
    stage_sm(1)
    stage_pv(nblk - 2, 0)
    stage_pv(nblk - 1, 1)


def _attn_a(sinks, qa, kad, vad, weights):
    bsz, _, seq, _ = qa.shape
    assert (seq // QBLK) % 2 == 0 and seq >= 2 * QBLK
    bf16_rows = 16
    assert all(w.shape[0] % (bsz * bf16_rows) == 0 for w in weights)
    wspecs = [pl.BlockSpec((w.shape[0] // bsz, w.shape[1]), lambda b: (b, 0)) for w in weights]
    return pl.pallas_call(
        _attn_a_kernel,
        grid=(bsz,),
        in_specs=[
            pl.BlockSpec(memory_space=pltpu.SMEM),
            pl.BlockSpec((1, PAIRS_A, seq, LANES), lambda b: (b, 0, 0, 0)),
            pl.BlockSpec((1, N_KV_A, seq, LANES), lambda b: (b, 0, 0, 0)),
            pl.BlockSpec((1, N_KV_A, 2, seq, LANES), lambda b: (b, 0, 0, 0, 0)),
        ] + wspecs,
        out_specs=[pl.BlockSpec((1, PAIRS_A, seq, LANES), lambda b: (b, 0, 0, 0))] + wspecs,
        out_shape=[jax.ShapeDtypeStruct((bsz, PAIRS_A, seq, LANES), BF16)] + [
            jax.ShapeDtypeStruct(w.shape, BF16) for w in weights],
        scratch_shapes=[pltpu.VMEM((2, N_HEADS_A, QBLK, 2 * QBLK), F32),
                        pltpu.VMEM((2, N_HEADS_A, QBLK, 2 * QBLK), BF16),
                        pltpu.VMEM((2, N_HEADS_A, QBLK, LANES), F32),
                        pltpu.VMEM((2, PAIRS_A, QBLK, LANES), F32)],
        compiler_params=pltpu.CompilerParams(
            dimension_semantics=("arbitrary",), vmem_limit_bytes=VMEM_LIMIT),
        name="attn_a",
    )(sinks, qa, kad, vad, *weights)


def _attn_b_kernel(b1_ref, b2_ref, b3_ref, ob_ref, o_scr, lse_scr, s12_scr, s3_scr, e12_scr, e3_scr, m_scr,
                   mp_scr, il_scr):
    seq = ob_ref.shape[1]
    nunit = seq // QBLK
    first, band, causal = _band_biases()
    lo = lax.broadcasted_iota(jnp.int32, (QBLK, LANES), 1) < HEAD_DIM
    refs = (b1_ref, b2_ref, b3_ref)
    dils = tuple(d for _, d in DILATED_PAIRS)
    nblks = tuple(nunit // d for d in dils)

    def coords(g, u):
        if nblks[g] == 1:
            return u, 0
        if nblks[g] == nunit:
            return 0, u
        return u // nblks[g], u % nblks[g]

    def stage_qk(u, slot):
        for g in range(N_GROUPS_B):
            r, jb = coords(g, u)
            for pr in range(PAIRS_B):
                q2 = refs[g][0, 0, pr, r, pl.ds(_blk(jb), QBLK), :]
                if nblks[g] == 1:
                    k = refs[g][0, 1, pr, r, :, :]
                    bias = causal
                else:
                    k = refs[g][0, 1, pr, r, pl.ds(_win(jb), 2 * QBLK), :]
                    if isinstance(jb, int):
                        bias = first if jb == 0 else band
                    elif nblks[g] == nunit:
                        bias = band
                    else:
                        bias = jnp.where(jb == 0, first, band)
                for hh in range(2):
                    s = _head_scores(q2, k, bias, lo, hh)
                    if nblks[g] == 1:
                        s3_scr[slot, 2 * pr + hh] = s
                    else:
                        s12_scr[slot, g, 2 * pr + hh] = s
                    m_scr[slot, g, 2 * pr + hh] = jnp.broadcast_to(
                        jnp.max(s, axis=1, keepdims=True), (QBLK, LANES))

    def stage_sm(slot):
        for g in range(N_GROUPS_B):
            for h in range(2 * PAIRS_B):
                m = m_scr[slot, g, h]
                if nblks[g] == 1:
                    e3_scr[slot, h] = jnp.exp(s3_scr[slot, h] - m).astype(BF16)
                else:
                    e12_scr[slot, g, h] = jnp.exp(s12_scr[slot, g, h] - jnp.tile(m, (1, 2))).astype(BF16)
            for pr in range(PAIRS_B):
                mp_scr[slot, g, pr] = jnp.where(lo, m_scr[slot, g, 2 * pr], m_scr[slot, g, 2 * pr + 1])

    def stage_pv(u, slot):
        for g in range(N_GROUPS_B):
            r, jb = coords(g, u)
            for pr in range(PAIRS_B):
                if nblks[g] == 1:
                    v_lo = refs[g][0, 2, pr, r, :, :]
                    v_hi = refs[g][0, 3, pr, r, :, :]
                    e0, e1 = e3_scr[slot, 2 * pr], e3_scr[slot, 2 * pr + 1]
                else:
                    v_lo = refs[g][0, 2, pr, r, pl.ds(_win(jb), 2 * QBLK), :]
                    v_hi = refs[g][0, 3, pr, r, pl.ds(_win(jb), 2 * QBLK), :]
                    e0, e1 = e12_scr[slot, g, 2 * pr], e12_scr[slot, g, 2 * pr + 1]
                num, den = _pair_pv(e0, e1, v_lo, v_hi, lo)
                if dils[g] == 16:
                    idx = pl.ds((r % 4) * (seq // 4) + r // 4, QBLK, stride=4)
                elif dils[g] == 1:
                    idx = pl.ds(_blk(jb) + r, QBLK)
                else:
                    idx = pl.ds(_blk(jb) * dils[g] + r, QBLK, stride=dils[g])
                o_scr[g, pr, idx, :] = num / den
                lse_scr[g, pr, idx, :] = mp_scr[slot, g, pr] + jnp.log(den)

    stage_qk(0, 0)
    stage_qk(1, 1)
    stage_sm(0)

    def two_units(u):
        stage_qk(u, 0)
        stage_sm(1)
        stage_pv(u - 2, 0)
        stage_qk(u + 1, 1)
        stage_sm(0)
        stage_pv(u - 1, 1)

    for u in range(2, nunit, 2):
        two_units(u)
    stage_sm(1)
    stage_pv(nunit - 2, 0)
    stage_pv(nunit - 1, 1)

    mrows = 2 * QBLK

    def merge(c, carry):
        rows = pl.ds(pl.multiple_of(c * mrows, mrows), mrows)
        for pr in range(PAIRS_B):
            quarter = mrows // 4
            for c4 in range(4):
                src = pl.ds(c4 * (seq // 4) + pl.multiple_of(c * quarter, quarter), quarter)
                il_scr[0, pr, pl.ds(c4, quarter, stride=4), :] = lse_scr[2, pr, src, :]
                il_scr[1, pr, pl.ds(c4, quarter, stride=4), :] = o_scr[2, pr, src, :]
            ls = [lse_scr[0, pr, rows, :], lse_scr[1, pr, rows, :], il_scr[0, pr]]
            mx = jnp.maximum(jnp.maximum(ls[0], ls[1]), ls[2])
            ws = [jnp.exp(l - mx) for l in ls]
            num = ws[0] * o_scr[0, pr, rows, :] + ws[1] * o_scr[1, pr, rows, :] + ws[2] * il_scr[1, pr]
            den = ws[0] + ws[1] + ws[2]
            ob_ref[0, rows, pr * LANES:(pr + 1) * LANES] = (num / den).astype(BF16)
        return carry

    lax.fori_loop(0, seq // mrows, merge, 0)


def _attn_b(b1, b2, b3):
    bsz = b1.shape[0]
    seq = b1.shape[4]
    nunit = seq // QBLK
    assert nunit % 2 == 0 and nunit >= 4 and nunit == DILATED_PAIRS[-1][1]
    assert b1.shape[3:5] == (1, seq) and b3.shape[4] == QBLK
    specs = [pl.BlockSpec((1,) + b.shape[1:], lambda b_: (b_, 0, 0, 0, 0, 0)) for b in (b1, b2, b3)]
    heads = 2 * PAIRS_B
    return pl.pallas_call(
        _attn_b_kernel,
        grid=(bsz,),
        in_specs=specs,
        out_specs=pl.BlockSpec((1, seq, GRP_B), lambda b_: (b_, 0, 0)),
        out_shape=jax.ShapeDtypeStruct((bsz, seq, GRP_B), BF16),
        scratch_shapes=[pltpu.VMEM((N_GROUPS_B, PAIRS_B, seq, LANES), F32),
                        pltpu.VMEM((N_GROUPS_B, PAIRS_B, seq, LANES), F32),
                        pltpu.VMEM((2, 2, heads, QBLK, 2 * QBLK), F32),
                        pltpu.VMEM((2, heads, QBLK, QBLK), F32),
                        pltpu.VMEM((2, 2, heads, QBLK, 2 * QBLK), BF16),
                        pltpu.VMEM((2, heads, QBLK, QBLK), BF16),
                        pltpu.VMEM((2, N_GROUPS_B, heads, QBLK, LANES), F32),
                        pltpu.VMEM((2, N_GROUPS_B, PAIRS_B, QBLK, LANES), F32),
                        pltpu.VMEM((2, PAIRS_B, 2 * QBLK, LANES), F32)],
        compiler_params=pltpu.CompilerParams(
            dimension_semantics=("arbitrary",), vmem_limit_bytes=VMEM_LIMIT),
        name="attn_b",
    )(b1, b2, b3)


def _finish_kernel(x_ref, oa_ref, ob_ref, gate_ref, wpa_ref, wpb_ref, wo_ref, g2_ref,
                   wgu_ref, wdn_ref, gf_ref, y_ref):
    d_ff = wdn_ref.shape[0]
    oa = jnp.concatenate([oa_ref[0, p] for p in range(PAIRS_A)], axis=-1)
    ma = jnp.dot(oa, wpa_ref[...], preferred_element_type=F32)
    mb = jnp.dot(ob_ref[0], wpb_ref[...], preferred_element_type=F32)
    m = gate_ref[0, :, :D_MODEL] * ma + gate_ref[0, :, D_MODEL:] * mb
    x1 = x_ref[0] + jnp.dot(m.astype(BF16), wo_ref[...], preferred_element_type=F32)
    xn = _rms(x1, g2_ref[...]).astype(BF16)
    gu = jnp.dot(xn, wgu_ref[...], preferred_element_type=F32)
    gg = gu[:, :d_ff]
    act = (gg * _sigmoid(gg) * gu[:, d_ff:]).astype(BF16)
    y_ref[0] = _rms(x1 + jnp.dot(act, wdn_ref[...], preferred_element_type=F32), gf_ref[...])


def _finish(x, oa, ob, gate, w_pa, w_pb, w_o, g2, w_gu, w_down, gf, tm):
    bsz, seq, _ = x.shape
    const2 = lambda b, i: (0, 0)
    wspec = lambda w: pl.BlockSpec(w.shape, const2, pipeline_mode=pl.Buffered(1))
    return pl.pallas_call(
        _finish_kernel,
        grid=(bsz, seq // tm),
        in_specs=[
            pl.BlockSpec((1, tm, D_MODEL), lambda b, i: (b, i, 0)),
            pl.BlockSpec((1, PAIRS_A, tm, LANES), lambda b, i: (b, 0, i, 0)),
            pl.BlockSpec((1, tm, GRP_B), lambda b, i: (b, i, 0)),
            pl.BlockSpec((1, tm, 2 * D_MODEL), lambda b, i: (b, i, 0)),
            wspec(w_pa), wspec(w_pb), wspec(w_o),
            pl.BlockSpec((1, D_MODEL), const2),
            wspec(w_gu), wspec(w_down),
            pl.BlockSpec((1, D_MODEL), const2),
        ],
        out_specs=pl.BlockSpec((1, tm, D_MODEL), lambda b, i: (b, i, 0)),
        out_shape=jax.ShapeDtypeStruct((bsz, seq, D_MODEL), F32),
        compiler_params=pltpu.CompilerParams(
            dimension_semantics=("arbitrary", "arbitrary"), vmem_limit_bytes=VMEM_LIMIT),
        name="finish",
    )(x, oa, ob, gate, w_pa, w_pb, w_o, g2, w_gu, w_down, gf)


ZS_QA = 0
ZS_QB = ZS_QA + Q_A
ZS_KA = ZS_QB + B_W
ZS_VA = ZS_KA + N_KV_A * GROUP_A * HEAD_DIM
ZS_KB = ZS_VA + N_KV_A * GROUP_A * HEAD_DIM
ZS_VB = ZS_KB + B_W
ZS_W = ZS_VB + B_W
KT_KA = 0
KT_VA = KT_KA + KV_A
KT_KB = KT_VA + KV_A
KT_VB = KT_KB + B_W
KT_W = KT_VB + B_W


def _proj_sample_kernel(x_ref, g_ref, w_ref, cos_ref, sina_ref, sinb_ref, zs_ref, kvt_ref, gate_ref):
    nb = x_ref.shape[0]
    h = _rms(x_ref[...], g_ref[...]).astype(BF16)
    cos = cos_ref[...]
    sina = sina_ref[...]
    sinb = sinb_ref[...]
    lo = lax.broadcasted_iota(jnp.int32, (nb, LANES), 1) < HEAD_DIM

    def mm(off):
        z = jnp.dot(h, w_ref[:, off:off + 2 * LANES], preferred_element_type=F32)
        return z[:, :LANES], z[:, LANES:]

    def rope(z):
        return _rope(z, cos, sina, sinb)

    def put(off, val):
        zs_ref[:, off:off + LANES] = val

    for c in range(PAIRS_A // 2):
        z0, z1 = mm(c * 2 * LANES)
        put(ZS_QA + c * 2 * LANES, rope(z0) * SCALE)
        put(ZS_QA + c * 2 * LANES + LANES, rope(z1) * SCALE)
    for c in range(B_W // (2 * LANES)):
        z0, z1 = mm(OFF_QB + c * 2 * LANES)
        put(ZS_QB + c * 2 * LANES, rope(z0) * SCALE)
        put(ZS_QB + c * 2 * LANES + LANES, rope(z1) * SCALE)

    zk, zv = mm(OFF_KA)
    ka = rope(zk)
    kvt_ref[KT_KA:KT_KA + KV_A, :] = ka.T
    kvt_ref[KT_VA:KT_VA + KV_A, :] = zv.T
    reps = GROUP_A * HEAD_DIM // LANES
    for base, val in ((ZS_KA, ka), (ZS_VA, zv)):
        d0, d1 = _dup_halves(val, lo)
        for kv, dup in enumerate((d0, d1)):
            for c in range(reps):
                put(base + (kv * reps + c) * LANES, dup)

    for c in range(B_W // (2 * LANES)):
        k0, k1 = mm(OFF_KB + c * 2 * LANES)
        v0, v1 = mm(OFF_VB + c * 2 * LANES)
        k0 = rope(k0)
        k1 = rope(k1)
        for j, (kk, vv) in enumerate(((k0, v0), (k1, v1))):
            off = (2 * c + j) * LANES
            put(ZS_KB + off, kk)
            put(ZS_VB + off, vv)
            kvt_ref[KT_KB + off:KT_KB + off + LANES, :] = kk.T
            kvt_ref[KT_VB + off:KT_VB + off + LANES, :] = vv.T

    for c in range(2 * D_MODEL // (2 * LANES)):
        z0, z1 = mm(OFF_GATE + c * 2 * LANES)
        gate_ref[:, c * 2 * LANES:c * 2 * LANES + LANES] = _sigmoid(z0)
        gate_ref[:, c * 2 * LANES + LANES:(c + 1) * 2 * LANES] = _sigmoid(z1)


def _proj_sample(x, g1, w_in, tables):
    nb = x.shape[0]
    cos, sina, sinb = tables
    return pl.pallas_call(
        _proj_sample_kernel,
        out_shape=[
            jax.ShapeDtypeStruct((nb, ZS_W), F32),
            jax.ShapeDtypeStruct((KT_W, nb), F32),
            jax.ShapeDtypeStruct((nb, 2 * D_MODEL), F32),
        ],
        compiler_params=pltpu.CompilerParams(vmem_limit_bytes=VMEM_LIMIT),
        name="proj_sample",
    )(x, g1, w_in, cos, sina, sinb)


SEQS_PER_STEP = 2


def _sample_attn_kernel(zs_ref, kvt_ref, sink_ref, ca_ref, c1_ref, c2_ref, c3_ref,
                        sa_ref, s1_ref, s2_ref, s3_ref, oa_ref, ob_ref):
    for j in range(ca_ref.shape[0]):
        _sample_attn_one(j, zs_ref, kvt_ref, sink_ref, ca_ref, c1_ref, c2_ref, c3_ref,
                         sa_ref, s1_ref, s2_ref, s3_ref, oa_ref, ob_ref)


def _sample_attn_one(j, zs_ref, kvt_ref, sink_ref, ca_ref, c1_ref, c2_ref, c3_ref,
                     sa_ref, s1_ref, s2_ref, s3_ref, oa_ref, ob_ref):
    b = pl.program_id(0) * ca_ref.shape[0] + j
    nb = kvt_ref.shape[1]
    pick = lax.broadcasted_iota(jnp.int32, (1, nb), 1) == b

    def row(off, width):
        return zs_ref[pl.ds(b, 1), off:off + width]

    def new_col(off, width):
        blk = kvt_ref[off:off + width, :]
        return jnp.sum(jnp.where(pick, blk, 0.0), axis=1, keepdims=True)

    def shifted(tile, col):
        w = tile.shape[1]
        lane = lax.broadcasted_iota(jnp.int32, tile.shape, 1)
        return jnp.where(lane == w - 1, col, pltpu.roll(tile, w - 1, 1))

    def head_mask(nh):
        rows = max(nh, 8)
        hid = lax.broadcasted_iota(jnp.int32, (rows, nh * HEAD_DIM), 0)
        lh = lax.broadcasted_iota(jnp.int32, (rows, nh * HEAD_DIM), 1) // HEAD_DIM
        return hid == lh

    def attend(q_row, kt, vt, knew_row, vnew_row, hm, dil, sink_col):
        qm = jnp.where(hm, q_row, 0.0)
        s = jnp.dot(qm.astype(BF16), kt.astype(BF16), preferred_element_type=F32)
        if dil > 1:
            lane = lax.broadcasted_iota(jnp.int32, s.shape, 1)
            s = jnp.where(lane % dil == 0, s, NEG_INF)
        s_new = jnp.sum(qm * knew_row, axis=1, keepdims=True)
        m = jnp.maximum(jnp.max(s, axis=1, keepdims=True), s_new)
        if sink_col is not None:
            m = jnp.maximum(m, sink_col)
        e = jnp.exp(s - m)
        e_new = jnp.exp(s_new - m)
        l = jnp.sum(e, axis=1, keepdims=True) + e_new
        if sink_col is not None:
            l = l + jnp.exp(sink_col - m)
        num = lax.dot_general(e.astype(BF16), vt.astype(BF16), (((1,), (1,)), ((), ())),
                              preferred_element_type=F32)
        return num + e_new * vnew_row, m, l

    hm_a = head_mask(GROUP_A)
    wq = GROUP_A * HEAD_DIM
    for kv in range(N_KV_A):
        kt = ca_ref[j, 0, kv * HEAD_DIM:(kv + 1) * HEAD_DIM, :]
        vt = ca_ref[j, 1, kv * HEAD_DIM:(kv + 1) * HEAD_DIM, :]
        num, m, l = attend(row(ZS_QA + kv * wq, wq),
                           jnp.concatenate([kt] * GROUP_A, axis=0),
                           jnp.concatenate([vt] * GROUP_A, axis=0),
                           row(ZS_KA + kv * wq, wq), row(ZS_VA + kv * wq, wq),
                           hm_a, 1, sink_ref[kv * GROUP_A:(kv + 1) * GROUP_A, :])
        o = jnp.sum(jnp.where(hm_a, num / l, 0.0), axis=0, keepdims=True)
        oa_ref[pl.ds(b, 1), kv * wq:(kv + 1) * wq] = o
    sa_ref[j, 0] = shifted(ca_ref[j, 0], new_col(KT_KA, KV_A))
    sa_ref[j, 1] = shifted(ca_ref[j, 1], new_col(KT_VA, KV_A))

    hm_b = head_mask(HEADS_B)
    nums, ms, ls = [], [], []
    for g, (c_ref, s_ref, (_, dil)) in enumerate(zip((c1_ref, c2_ref, c3_ref), (s1_ref, s2_ref, s3_ref),
                                                      DILATED_PAIRS)):
        kt = c_ref[j, 0]
        vt = c_ref[j, 1]
        num, m, l = attend(row(ZS_QB + g * GRP_B, GRP_B), kt, vt,
                           row(ZS_KB + g * GRP_B, GRP_B), row(ZS_VB + g * GRP_B, GRP_B),
                           hm_b, dil, None)
        nums.append(num)
        ms.append(m)
        ls.append(l)
        s_ref[j, 0] = shifted(kt, new_col(KT_KB + g * GRP_B, GRP_B))
        s_ref[j, 1] = shifted(vt, new_col(KT_VB + g * GRP_B, GRP_B))
    mx = jnp.maximum(jnp.maximum(ms[0], ms[1]), ms[2])
    ws = [jnp.exp(m - mx) for m in ms]
    num = ws[0] * nums[0] + ws[1] * nums[1] + ws[2] * nums[2]
    den = ws[0] * ls[0] + ws[1] * ls[1] + ws[2] * ls[2]
    ob_ref[pl.ds(b, 1), :] = jnp.sum(jnp.where(hm_b, num / den, 0.0), axis=0, keepdims=True)


def _sample_attn(zs, kvt, sinks_col, caches):
    nb = zs.shape[0]
    full2 = lambda a: pl.BlockSpec(a.shape, lambda b: (0, 0))
    cspec = lambda a: pl.BlockSpec((SEQS_PER_STEP,) + a.shape[1:], lambda b: (b, 0, 0, 0))
    return pl.pallas_call(
        _sample_attn_kernel,
        grid=(nb // SEQS_PER_STEP,),
        in_specs=[full2(zs), full2(kvt), full2(sinks_col)] + [cspec(c) for c in caches],
        out_specs=[cspec(c) for c in caches] + [
            pl.BlockSpec((nb, Q_A), lambda b: (0, 0)),
            pl.BlockSpec((nb, GRP_B), lambda b: (0, 0)),
        ],
        out_shape=[jax.ShapeDtypeStruct(c.shape, F32) for c in caches] + [
            jax.ShapeDtypeStruct((nb, Q_A), F32),
            jax.ShapeDtypeStruct((nb, GRP_B), F32),
        ],
        compiler_params=pltpu.CompilerParams(
            dimension_semantics=("arbitrary",), vmem_limit_bytes=VMEM_LIMIT),
        name="sample_attn",
    )(zs, kvt, sinks_col, *caches)


def _rows_minor(c):
    _, bsz, rows, two, nh, hd = c.shape
    return jnp.transpose(c, (0, 1, 3, 4, 5, 2)).reshape(bsz, two, nh * hd, rows)


def _rows_major(c, nh):
    bsz, two, _, rows = c.shape
    return jnp.transpose(c.reshape(1, bsz, two, nh, HEAD_DIM, rows), (0, 1, 5, 2, 3, 4))


def kernel(x_prompt, x_sample, cache_a_kv, cache_b1_kv, cache_b2_kv, cache_b3_kv, norm1_g, w_in, sinks,
           w_pa, w_pb, w_o, norm2_g, w_gu, w_down, final_norm_g):
    bsz, seq, _ = x_prompt.shape
    nb, dec_seq, _ = x_sample.shape
    assert dec_seq == 1 and norm1_g.shape[0] == 1
    assert cache_a_kv.shape[2] == WINDOW_A
    assert tuple(c.shape[2] for c in (cache_b1_kv, cache_b2_kv, cache_b3_kv)) == tuple(
        w for w, _ in DILATED_PAIRS)

    w_in_b = w_in[0].astype(BF16)
    gf = final_norm_g.reshape(1, D_MODEL)

    tables_p = _rope_tables(jnp.arange(seq, dtype=jnp.int32))
    (qa, kad, vad, b1, b2, b3, gate, pa, pb1, pb2, pb3) = _proj_prompt(
        x_prompt, norm1_g, w_in_b, tables_p, TM_PROJ)
    oa, w_pa_b, w_pb_b, w_o_b, w_gu_b, w_dn_b = _attn_a(
        sinks[0], qa, kad, vad, (w_pa[0], w_pb[0], w_o[0], w_gu[0], w_down[0]))
    ob = _attn_b(b1, b2, b3)
    y_prompt = _finish(x_prompt, oa, ob, gate, w_pa_b, w_pb_b, w_o_b, norm2_g, w_gu_b, w_dn_b, gf, TM_FINISH)

    xs = x_sample.reshape(nb, D_MODEL)
    tables_s = _rope_tables(jnp.full((nb,), PAST_LEN, dtype=jnp.int32))
    zs, kvt, gate_s = _proj_sample(xs, norm1_g, w_in_b, tables_s)
    caches = [_rows_minor(c) for c in (cache_a_kv, cache_b1_kv, cache_b2_kv, cache_b3_kv)]
    sa, sb1, sb2, sb3, oa_s, ob_s = _sample_attn(zs, kvt, sinks[0].reshape(N_HEADS_A, 1), caches)
    oa_s = jnp.transpose(oa_s.reshape(nb, PAIRS_A, LANES), (1, 0, 2))[None].astype(BF16)
    y_sample = _finish(xs[None], oa_s, ob_s[None].astype(BF16), gate_s[None], w_pa_b, w_pb_b, w_o_b,
                       norm2_g, w_gu_b, w_dn_b, gf, nb)

    return (y_prompt, y_sample.reshape(nb, 1, D_MODEL),
            _rows_major(pa, N_KV_A), _rows_major(pb1, HEADS_B), _rows_major(pb2, HEADS_B),
            _rows_major(pb3, HEADS_B),
            _rows_major(sa, N_KV_A), _rows_major(sb1, HEADS_B), _rows_major(sb2, HEADS_B),
            _rows_major(sb3, HEADS_B))
```

```python
import jax
import jax.numpy as jnp
from jax import lax
from jax.experimental import pallas as pl
from jax.experimental.pallas import tpu as pltpu

F32 = jnp.float32
BF16 = jnp.bfloat16

D_MODEL = 1024
HEAD_DIM = 64
HALF = HEAD_DIM // 2
N_HEADS_A = 16
N_KV_A = 2
GROUP_A = N_HEADS_A // N_KV_A
WINDOW_A = 128
DILATED_PAIRS = ((128, 1), (512, 4), (2048, 16))
N_GROUPS_B = 3
HEADS_B = 4
SPAN = 128
PAST_LEN = 16384
ROPE_THETA = 10000.0
NORM_EPS = 1e-6
NEG_INF = -1e30
SCALE = HEAD_DIM ** -0.5

LANES = 128
PAIRS_A = N_HEADS_A // 2
PAIRS_B = HEADS_B // 2
Q_A = N_HEADS_A * HEAD_DIM
KV_A = N_KV_A * HEAD_DIM
GRP_B = HEADS_B * HEAD_DIM
B_W = N_GROUPS_B * GRP_B
OFF_KA = Q_A
OFF_VA = OFF_KA + KV_A
OFF_QB = OFF_VA + KV_A
OFF_KB = OFF_QB + B_W
OFF_VB = OFF_KB + B_W
OFF_GATE = OFF_VB + B_W
QBLK = 128
B_PARTS = 4

VMEM_LIMIT = 56 * 1024 * 1024
TM_PROJ = 512
TM_FINISH = 512


def _rms(x, g):
    return x * lax.rsqrt(jnp.mean(x * x, axis=-1, keepdims=True) + NORM_EPS) * g


def _rope(z, cos, sina, sinb):
    return z * cos + pltpu.roll(z, LANES - HALF, 1) * sina + pltpu.roll(z, HALF, 1) * sinb


def _sigmoid(z):
    return 1.0 / (1.0 + jnp.exp(-z))


def _dup_halves(z, lo):
    r = pltpu.roll(z, HEAD_DIM, 1)
    return jnp.where(lo, z, r), jnp.where(lo, r, z)


def _rope_tables(pos):
    inv = 1.0 / (ROPE_THETA ** (jnp.arange(HALF, dtype=F32) / HALF))
    ang = pos.astype(F32)[:, None] * inv[None, :]
    ang = jnp.concatenate([ang, ang, ang, ang], axis=-1)
    first = (jnp.arange(LANES) % HEAD_DIM) < HALF
    cos = jnp.cos(ang)
    sin = jnp.sin(ang)
    return cos, jnp.where(first, -sin, 0.0), jnp.where(first, 0.0, sin)


def _proj_prompt_kernel(x_ref, g_ref, w_ref, cos_ref, sina_ref, sinb_ref,
                        qa_ref, kad_ref, vad_ref, b1_ref, b2_ref, b3_ref, gate_ref,
                        pa_ref, pb1_ref, pb2_ref, pb3_ref, scr_ref, scr2_ref):
    tm = x_ref.shape[1]
    i = pl.program_id(1)
    last = pl.num_programs(1) - 1
    h = _rms(x_ref[0], g_ref[...]).astype(BF16)
    cos = cos_ref[...]
    sina = sina_ref[...]
    sinb = sinb_ref[...]
    lo = lax.broadcasted_iota(jnp.int32, (tm, LANES), 1) < HEAD_DIM

    wide = 8 * LANES
    chunks = {}

    def mm(ci):
        cols = slice(ci * wide, min((ci + 1) * wide, w_ref.shape[1]))
        chunks[ci] = jnp.dot(h, w_ref[:, cols], preferred_element_type=F32)

    def rope(z):
        return _rope(z, cos, sina, sinb)

    offs = ([c * 2 * LANES for c in range(PAIRS_A // 2)] + [OFF_KA]
            + [o + g * GRP_B for g in range(N_GROUPS_B) for o in (OFF_QB, OFF_KB, OFF_VB)]
            + [OFF_GATE + c * 2 * LANES for c in range(2 * D_MODEL // (2 * LANES))])
    mm(0)

    def take(off):
        nxt = offs.index(off) + 1
        for o in offs[nxt - 1:nxt + 1]:
            if o // wide not in chunks:
                mm(o // wide)
        ci, within = divmod(off, wide)
        z = chunks[ci]
        return z[:, within:within + LANES], z[:, within + LANES:within + 2 * LANES]

    for c in range(PAIRS_A // 2):
        z0, z1 = take(c * 2 * LANES)
        qa_ref[0, 2 * c] = (rope(z0) * SCALE).astype(BF16)
        qa_ref[0, 2 * c + 1] = (rope(z1) * SCALE).astype(BF16)

    zk, zv = take(OFF_KA)
    ka = rope(zk)
    k0, k1 = _dup_halves(ka, lo)
    v0, v1 = _dup_halves(zv, lo)
    kad_ref[0, 0] = k0.astype(BF16)
    kad_ref[0, 1] = k1.astype(BF16)
    for kv, vd in enumerate((v0, v1)):
        vad_ref[0, kv, 0] = jnp.where(lo, vd, 1.0).astype(BF16)
        vad_ref[0, kv, 1] = jnp.where(lo, 1.0, vd).astype(BF16)

    @pl.when(i == last)
    def _():
        pa_ref[0, 0] = ka[tm - WINDOW_A:, :].T
        pa_ref[0, 1] = zv[tm - WINDOW_A:, :].T

    b_refs = (b1_ref, b2_ref, b3_ref)
    for g, (win, dil) in enumerate(DILATED_PAIRS):
        q0, q1 = take(OFF_QB + g * GRP_B)
        k0, k1 = take(OFF_KB + g * GRP_B)
        v0, v1 = take(OFF_VB + g * GRP_B)
        q0 = rope(q0) * SCALE
        q1 = rope(q1) * SCALE
        k0 = rope(k0)
        k1 = rope(k1)
        parts = ((q0, q1), (k0, k1),
                 (jnp.where(lo, v0, 1.0), jnp.where(lo, v1, 1.0)),
                 (jnp.where(lo, 1.0, v0), jnp.where(lo, 1.0, v1)))
        for part in range(B_PARTS):
            for pr in range(PAIRS_B):
                val = parts[part][pr]
                if dil == 1:
                    b_refs[g][0, part, pr, 0] = val.astype(BF16)
                elif dil == 16:
                    slot = part * PAIRS_B + pr
                    scr_ref[slot] = val
                    q4 = tm // 4
                    for r4 in range(4):
                        scr2_ref[slot, r4 * q4:(r4 + 1) * q4, :] = scr_ref[slot, pl.ds(r4, q4, stride=4), :]
                    for r4 in range(4):
                        for r16 in range(4):
                            b_refs[g][0, part, pr, r4 + 4 * r16] = scr2_ref[
                                slot, pl.ds(r4 * q4 + r16, q4 // 4, stride=4), :].astype(BF16)
                else:
                    slot = part * PAIRS_B + pr
                    scr_ref[slot] = val
                    for r in range(dil):
                        b_refs[g][0, part, pr, r] = scr_ref[
                            slot, pl.ds(r, tm // dil, stride=dil), :].astype(BF16)
        kt = jnp.concatenate([k0, k1], axis=-1).T
        vt = jnp.concatenate([v0, v1], axis=-1).T
        if g == 2:
            pb3_ref[0, 0] = kt
            pb3_ref[0, 1] = vt
        else:
            st_ref = pb1_ref if g == 0 else pb2_ref
            rows = st_ref.shape[3]

            @pl.when(i == last)
            def _(kt=kt, vt=vt, st_ref=st_ref, rows=rows):
                st_ref[0, 0] = kt[:, tm - rows:]
                st_ref[0, 1] = vt[:, tm - rows:]

    for c in range(2 * D_MODEL // (2 * LANES)):
        z0, z1 = take(OFF_GATE + c * 2 * LANES)
        gate_ref[0, :, c * 2 * LANES:c * 2 * LANES + LANES] = _sigmoid(z0)
        gate_ref[0, :, c * 2 * LANES + LANES:(c + 1) * 2 * LANES] = _sigmoid(z1)


def _proj_prompt(x, g1, w_in, tables, tm):
    bsz, seq, _ = x.shape
    nt = seq // tm
    cos, sina, sinb = tables
    const2 = lambda b, i: (0, 0)
    tab_spec = pl.BlockSpec((tm, LANES), lambda b, i: (i, 0))
    out_shape = [
        jax.ShapeDtypeStruct((bsz, PAIRS_A, seq, LANES), BF16),
        jax.ShapeDtypeStruct((bsz, N_KV_A, seq, LANES), BF16),
        jax.ShapeDtypeStruct((bsz, N_KV_A, 2, seq, LANES), BF16),
    ]
    out_specs = [
        pl.BlockSpec((1, PAIRS_A, tm, LANES), lambda b, i: (b, 0, i, 0)),
        pl.BlockSpec((1, N_KV_A, tm, LANES), lambda b, i: (b, 0, i, 0)),
        pl.BlockSpec((1, N_KV_A, 2, tm, LANES), lambda b, i: (b, 0, 0, i, 0)),
    ]
    for _, dil in DILATED_PAIRS:
        out_shape.append(jax.ShapeDtypeStruct((bsz, B_PARTS, PAIRS_B, dil, seq // dil, LANES), BF16))
        out_specs.append(pl.BlockSpec((1, B_PARTS, PAIRS_B, dil, tm // dil, LANES),
                                      lambda b, i: (b, 0, 0, 0, i, 0)))
    out_shape.append(jax.ShapeDtypeStruct((bsz, seq, 2 * D_MODEL), F32))
    out_specs.append(pl.BlockSpec((1, tm, 2 * D_MODEL), lambda b, i: (b, i, 0)))
    out_shape.append(jax.ShapeDtypeStruct((bsz, 2, KV_A, min(WINDOW_A, seq)), F32))
    out_specs.append(pl.BlockSpec((1, 2, KV_A, min(WINDOW_A, seq)), lambda b, i: (b, 0, 0, 0)))
    for g, (win, _) in enumerate(DILATED_PAIRS):
        rows = min(win, seq)
        out_shape.append(jax.ShapeDtypeStruct((bsz, 2, GRP_B, rows), F32))
        if g == 2:
            out_specs.append(pl.BlockSpec((1, 2, GRP_B, tm), lambda b, i: (b, 0, 0, i)))
        else:
            out_specs.append(pl.BlockSpec((1, 2, GRP_B, rows), lambda b, i: (b, 0, 0, 0)))
    return pl.pallas_call(
        _proj_prompt_kernel,
        grid=(bsz, nt),
        in_specs=[
            pl.BlockSpec((1, tm, D_MODEL), lambda b, i: (b, i, 0)),
            pl.BlockSpec((1, D_MODEL), const2),
            pl.BlockSpec(w_in.shape, const2, pipeline_mode=pl.Buffered(1)),
            tab_spec, tab_spec, tab_spec,
        ],
        out_specs=out_specs,
        out_shape=out_shape,
        scratch_shapes=[pltpu.VMEM((B_PARTS * PAIRS_B, tm, LANES), F32),
                        pltpu.VMEM((B_PARTS * PAIRS_B, tm, LANES), F32)],
        compiler_params=pltpu.CompilerParams(
            dimension_semantics=("arbitrary", "arbitrary"), vmem_limit_bytes=VMEM_LIMIT),
        name="proj_prompt",
    )(x, g1, w_in, cos, sina, sinb)


def _band_biases():
    qi = lax.broadcasted_iota(jnp.int32, (QBLK, 2 * QBLK), 0)
    ki = lax.broadcasted_iota(jnp.int32, (QBLK, 2 * QBLK), 1)
    band = jnp.where((ki >= qi) & (ki <= qi + SPAN), 0.0, NEG_INF).astype(F32)
    first = jnp.where(ki <= qi, 0.0, NEG_INF).astype(F32)
    return first, band, first[:, :QBLK]


def _head_scores(q2, k, bias, lo, hh):
    qm = jnp.where(lo if hh == 0 else jnp.logical_not(lo), q2, jnp.zeros_like(q2))
    s = lax.dot_general(qm, k, (((1,), (1,)), ((), ())), preferred_element_type=F32)
    return s + bias


def _pair_pv(e0, e1, v_lo, v_hi, lo):
    pv0 = jnp.dot(e0, v_lo, preferred_element_type=F32)
    pv1 = jnp.dot(e1, v_hi, preferred_element_type=F32)
    num = jnp.where(lo, pv0, pv1)
    den = pltpu.roll(jnp.where(lo, pv1, pv0), HEAD_DIM, 1)
    return num, den


def _blk(j):
    return j * QBLK if isinstance(j, int) else pl.multiple_of(j * QBLK, QBLK)


def _win(j):
    return max(j - 1, 0) * QBLK if isinstance(j, int) else pl.multiple_of(jnp.maximum(j - 1, 0) * QBLK, QBLK)


def _attn_a_kernel(sink_ref, qa_ref, kd_ref, va_ref, *rest):
    nw = (len(rest) - 3) // 2
    oa_ref = rest[nw]
    s_scr, m_scr = rest[2 * nw + 1:]
    for src, dst in zip(rest[:nw], rest[nw + 1:2 * nw + 1]):
        dst[...] = src[...].astype(BF16)
    seq = qa_ref.shape[2]
    nblk = seq // QBLK
    first, band, _ = _band_biases()
    lo = lax.broadcasted_iota(jnp.int32, (QBLK, LANES), 1) < HEAD_DIM
    pairs_per_kv = PAIRS_A // N_KV_A

    def stage_qk(jb, slot, bias):
        for kv in range(N_KV_A):
            k = kd_ref[0, kv, pl.ds(_win(jb), 2 * QBLK), :]
            for pr in range(pairs_per_kv):
                pair = kv * pairs_per_kv + pr
                q2 = qa_ref[0, pair, pl.ds(_blk(jb), QBLK), :]
                for hh in range(2):
                    h = 2 * pair + hh
                    s = _head_scores(q2, k, bias, lo, hh)
                    s_scr[slot, h] = s
                    m = jnp.broadcast_to(jnp.max(s, axis=1, keepdims=True), (QBLK, LANES))
                    m_scr[slot, h] = jnp.maximum(m, sink_ref[h])

    def stage_spv(jb, slot):
        for kv in range(N_KV_A):
            v_lo = va_ref[0, kv, 0, pl.ds(_win(jb), 2 * QBLK), :]
            v_hi = va_ref[0, kv, 1, pl.ds(_win(jb), 2 * QBLK), :]
            for pr in range(pairs_per_kv):
                pair = kv * pairs_per_kv + pr
                es, ms = [], []
                for hh in range(2):
                    h = 2 * pair + hh
                    m = m_scr[slot, h]
                    es.append(jnp.exp(s_scr[slot, h] - jnp.tile(m, (1, 2))).astype(BF16))
                    ms.append(m)
                sink_pair = jnp.where(lo, sink_ref[2 * pair], sink_ref[2 * pair + 1])
                num, den = _pair_pv(es[0], es[1], v_lo, v_hi, lo)
                den = den + jnp.exp(sink_pair - jnp.where(lo, ms[0], ms[1]))
                oa_ref[0, pair, pl.ds(_blk(jb), QBLK), :] = (num / den).astype(BF16)

    stage_qk(0, 0, first)
    stage_qk(1, 1, band)
    stage_spv(0, 0)

    def two_blocks(t):
        stage_spv(t - 1, 1)
        stage_qk(t, 0, band)
        stage_spv(t, 0)
        stage_qk(t + 1, 1, band)

    def body(i, carry):
        t = 2 + 4 * i
        two_blocks(t)
        two_blocks(t + 2)
        return carry

    nloop = (nblk - 2) // 4
    lax.fori_loop(0, nloop, body, 0)
    for t in range(2 + 4 * nloop, nblk, 2):
        two_blocks(t)
    stage_spv(nblk - 1, 1)


def _attn_a(sinks, qa, kad, vad, weights):
    bsz, _, seq, _ = qa.shape
    assert (seq // QBLK) % 2 == 0 and seq >= 2 * QBLK
    bf16_rows = 16
    assert all(w.shape[0] % (bsz * bf16_rows) == 0 for w in weights)
    wspecs = [pl.BlockSpec((w.shape[0] // bsz, w.shape[1]), lambda b: (b, 0)) for w in weights]
    return pl.pallas_call(
        _attn_a_kernel,
        grid=(bsz,),
        in_specs=[
            pl.BlockSpec(memory_space=pltpu.SMEM),
            pl.BlockSpec((1, PAIRS_A, seq, LANES), lambda b: (b, 0, 0, 0)),
            pl.BlockSpec((1, N_KV_A, seq, LANES), lambda b: (b, 0, 0, 0)),
            pl.BlockSpec((1, N_KV_A, 2, seq, LANES), lambda b: (b, 0, 0, 0, 0)),
        ] + wspecs,
        out_specs=[pl.BlockSpec((1, PAIRS_A, seq, LANES), lambda b: (b, 0, 0, 0))] + wspecs,
        out_shape=[jax.ShapeDtypeStruct((bsz, PAIRS_A, seq, LANES), BF16)] + [
            jax.ShapeDtypeStruct(w.shape, BF16) for w in weights],
        scratch_shapes=[pltpu.VMEM((2, N_HEADS_A, QBLK, 2 * QBLK), F32),
                        pltpu.VMEM((2, N_HEADS_A, QBLK, LANES), F32)],
        compiler_params=pltpu.CompilerParams(
            dimension_semantics=("arbitrary",), vmem_limit_bytes=VMEM_LIMIT),
        name="attn_a",
    )(sinks, qa, kad, vad, *weights)


def _attn_b_kernel(b1_ref, b2_ref, b3_ref, ob_ref, o_scr, lse_scr, s12_scr, s3_scr, e12_scr, e3_scr, m_scr,
                   mp_scr, il_scr):
    seq = ob_ref.shape[1]
    nunit = seq // QBLK
    first, band, causal = _band_biases()
    lo = lax.broadcasted_iota(jnp.int32, (QBLK, LANES), 1) < HEAD_DIM
    refs = (b1_ref, b2_ref, b3_ref)
    dils = tuple(d for _, d in DILATED_PAIRS)
    nblks = tuple(nunit // d for d in dils)

    def coords(g, u):
        if nblks[g] == 1:
            return u, 0
        if nblks[g] == nunit:
            return 0, u
        return u // nblks[g], u % nblks[g]

    def stage_qk(u, slot):
        for g in range(N_GROUPS_B):
            r, jb = coords(g, u)
            for pr in range(PAIRS_B):
                q2 = refs[g][0, 0, pr, r, pl.ds(_blk(jb), QBLK), :]
                if nblks[g] == 1:
                    k = refs[g][0, 1, pr, r, :, :]
                    bias = causal
                else:
                    k = refs[g][0, 1, pr, r, pl.ds(_win(jb), 2 * QBLK), :]
                    if isinstance(jb, int):
                        bias = first if jb == 0 else band
                    elif nblks[g] == nunit:
                        bias = band
                    else:
                        bias = jnp.where(jb == 0, first, band)
                for hh in range(2):
                    s = _head_scores(q2, k, bias, lo, hh)
                    if nblks[g] == 1:
                        s3_scr[slot, 2 * pr + hh] = s
                    else:
                        s12_scr[slot, g, 2 * pr + hh] = s
                    m_scr[slot, g, 2 * pr + hh] = jnp.broadcast_to(
                        jnp.max(s, axis=1, keepdims=True), (QBLK, LANES))

    def stage_sm(slot):
        for g in range(N_GROUPS_B):
            for h in range(2 * PAIRS_B):
                m = m_scr[slot, g, h]
                if nblks[g] == 1:
                    e3_scr[slot, h] = jnp.exp(s3_scr[slot, h] - m).astype(BF16)
                else:
                    e12_scr[slot, g, h] = jnp.exp(s12_scr[slot, g, h] - jnp.tile(m, (1, 2))).astype(BF16)
            for pr in range(PAIRS_B):
                mp_scr[slot, g, pr] = jnp.where(lo, m_scr[slot, g, 2 * pr], m_scr[slot, g, 2 * pr + 1])

    def stage_pv(u, slot):
        for g in range(N_GROUPS_B):
            r, jb = coords(g, u)
            for pr in range(PAIRS_B):
                if nblks[g] == 1:
                    v_lo = refs[g][0, 2, pr, r, :, :]
                    v_hi = refs[g][0, 3, pr, r, :, :]
                    e0, e1 = e3_scr[slot, 2 * pr], e3_scr[slot, 2 * pr + 1]
                else:
                    v_lo = refs[g][0, 2, pr, r, pl.ds(_win(jb), 2 * QBLK), :]
                    v_hi = refs[g][0, 3, pr, r, pl.ds(_win(jb), 2 * QBLK), :]
                    e0, e1 = e12_scr[slot, g, 2 * pr], e12_scr[slot, g, 2 * pr + 1]
                num, den = _pair_pv(e0, e1, v_lo, v_hi, lo)
                if dils[g] == 16:
                    idx = pl.ds((r % 4) * (seq // 4) + r // 4, QBLK, stride=4)
                elif dils[g] == 1:
                    idx = pl.ds(_blk(jb) + r, QBLK)
                else:
                    idx = pl.ds(_blk(jb) * dils[g] + r, QBLK, stride=dils[g])
                o_scr[g, pr, idx, :] = num / den
                lse_scr[g, pr, idx, :] = mp_scr[slot, g, pr] + jnp.log(den)

    stage_qk(0, 0)
    stage_qk(1, 1)
    stage_sm(0)

    def two_units(u):
        stage_qk(u, 0)
        stage_sm(1)
        stage_pv(u - 2, 0)
        stage_qk(u + 1, 1)
        stage_sm(0)
        stage_pv(u - 1, 1)

    for u in range(2, nunit, 2):
        two_units(u)
    stage_sm(1)
    stage_pv(nunit - 2, 0)
    stage_pv(nunit - 1, 1)

    mrows = 2 * QBLK

    def merge(c, carry):
        rows = pl.ds(pl.multiple_of(c * mrows, mrows), mrows)
        for pr in range(PAIRS_B):
            quarter = mrows // 4
            for c4 in range(4):
                src = pl.ds(c4 * (seq // 4) + pl.multiple_of(c * quarter, quarter), quarter)
                il_scr[0, pr, pl.ds(c4, quarter, stride=4), :] = lse_scr[2, pr, src, :]
                il_scr[1, pr, pl.ds(c4, quarter, stride=4), :] = o_scr[2, pr, src, :]
            ls = [lse_scr[0, pr, rows, :], lse_scr[1, pr, rows, :], il_scr[0, pr]]
            mx = jnp.maximum(jnp.maximum(ls[0], ls[1]), ls[2])
            ws = [jnp.exp(l - mx) for l in ls]
            num = ws[0] * o_scr[0, pr, rows, :] + ws[1] * o_scr[1, pr, rows, :] + ws[2] * il_scr[1, pr]
            den = ws[0] + ws[1] + ws[2]
            ob_ref[0, rows, pr * LANES:(pr + 1) * LANES] = (num / den).astype(BF16)
        return carry

    lax.fori_loop(0, seq // mrows, merge, 0)


def _attn_b(b1, b2, b3):
    bsz = b1.shape[0]
    seq = b1.shape[4]
    nunit = seq // QBLK
    assert nunit % 2 == 0 and nunit >= 4 and nunit == DILATED_PAIRS[-1][1]
    assert b1.shape[3:5] == (1, seq) and b3.shape[4] == QBLK
    specs = [pl.BlockSpec((1,) + b.shape[1:], lambda b_: (b_, 0, 0, 0, 0, 0)) for b in (b1, b2, b3)]
    heads = 2 * PAIRS_B
    return pl.pallas_call(
        _attn_b_kernel,
        grid=(bsz,),
        in_specs=specs,
        out_specs=pl.BlockSpec((1, seq, GRP_B), lambda b_: (b_, 0, 0)),
        out_shape=jax.ShapeDtypeStruct((bsz, seq, GRP_B), BF16),
        scratch_shapes=[pltpu.VMEM((N_GROUPS_B, PAIRS_B, seq, LANES), F32),
                        pltpu.VMEM((N_GROUPS_B, PAIRS_B, seq, LANES), F32),
                        pltpu.VMEM((2, 2, heads, QBLK, 2 * QBLK), F32),
                        pltpu.VMEM((2, heads, QBLK, QBLK), F32),
                        pltpu.VMEM((2, 2, heads, QBLK, 2 * QBLK), BF16),
                        pltpu.VMEM((2, heads, QBLK, QBLK), BF16),
                        pltpu.VMEM((2, N_GROUPS_B, heads, QBLK, LANES), F32),
                        pltpu.VMEM((2, N_GROUPS_B, PAIRS_B, QBLK, LANES), F32),
                        pltpu.VMEM((2, PAIRS_B, 2 * QBLK, LANES), F32)],
        compiler_params=pltpu.CompilerParams(
            dimension_semantics=("arbitrary",), vmem_limit_bytes=VMEM_LIMIT),
        name="attn_b",
    )(b1, b2, b3)


def _finish_kernel(x_ref, oa_ref, ob_ref, gate_ref, wpa_ref, wpb_ref, wo_ref, g2_ref,
                   wgu_ref, wdn_ref, gf_ref, y_ref):
    d_ff = wdn_ref.shape[0]
    oa = jnp.concatenate([oa_ref[0, p] for p in range(PAIRS_A)], axis=-1)
    ma = jnp.dot(oa, wpa_ref[...], preferred_element_type=F32)
    mb = jnp.dot(ob_ref[0], wpb_ref[...], preferred_element_type=F32)
    m = gate_ref[0, :, :D_MODEL] * ma + gate_ref[0, :, D_MODEL:] * mb
    x1 = x_ref[0] + jnp.dot(m.astype(BF16), wo_ref[...], preferred_element_type=F32)
    xn = _rms(x1, g2_ref[...]).astype(BF16)
    gu = jnp.dot(xn, wgu_ref[...], preferred_element_type=F32)
    gg = gu[:, :d_ff]
    act = (gg * _sigmoid(gg) * gu[:, d_ff:]).astype(BF16)
    y_ref[0] = _rms(x1 + jnp.dot(act, wdn_ref[...], preferred_element_type=F32), gf_ref[...])


def _finish(x, oa, ob, gate, w_pa, w_pb, w_o, g2, w_gu, w_down, gf, tm):
    bsz, seq, _ = x.shape
    const2 = lambda b, i: (0, 0)
    wspec = lambda w: pl.BlockSpec(w.shape, const2, pipeline_mode=pl.Buffered(1))
    return pl.pallas_call(
        _finish_kernel,
        grid=(bsz, seq // tm),
        in_specs=[
            pl.BlockSpec((1, tm, D_MODEL), lambda b, i: (b, i, 0)),
            pl.BlockSpec((1, PAIRS_A, tm, LANES), lambda b, i: (b, 0, i, 0)),
            pl.BlockSpec((1, tm, GRP_B), lambda b, i: (b, i, 0)),
            pl.BlockSpec((1, tm, 2 * D_MODEL), lambda b, i: (b, i, 0)),
            wspec(w_pa), wspec(w_pb), wspec(w_o),
            pl.BlockSpec((1, D_MODEL), const2),
            wspec(w_gu), wspec(w_down),
            pl.BlockSpec((1, D_MODEL), const2),
        ],
        out_specs=pl.BlockSpec((1, tm, D_MODEL), lambda b, i: (b, i, 0)),
        out_shape=jax.ShapeDtypeStruct((bsz, seq, D_MODEL), F32),
        compiler_params=pltpu.CompilerParams(
            dimension_semantics=("arbitrary", "arbitrary"), vmem_limit_bytes=VMEM_LIMIT),
        name="finish",
    )(x, oa, ob, gate, w_pa, w_pb, w_o, g2, w_gu, w_down, gf)


ZS_QA = 0
ZS_QB = ZS_QA + Q_A
ZS_KA = ZS_QB + B_W
ZS_VA = ZS_KA + N_KV_A * GROUP_A * HEAD_DIM
ZS_KB = ZS_VA + N_KV_A * GROUP_A * HEAD_DIM
ZS_VB = ZS_KB + B_W
ZS_W = ZS_VB + B_W
KT_KA = 0
KT_VA = KT_KA + KV_A
KT_KB = KT_VA + KV_A
KT_VB = KT_KB + B_W
KT_W = KT_VB + B_W


def _proj_sample_kernel(x_ref, g_ref, w_ref, cos_ref, sina_ref, sinb_ref, zs_ref, kvt_ref, gate_ref):
    nb = x_ref.shape[0]
    h = _rms(x_ref[...], g_ref[...]).astype(BF16)
    cos = cos_ref[...]
    sina = sina_ref[...]
    sinb = sinb_ref[...]
    lo = lax.broadcasted_iota(jnp.int32, (nb, LANES), 1) < HEAD_DIM

    def mm(off):
        z = jnp.dot(h, w_ref[:, off:off + 2 * LANES], preferred_element_type=F32)
        return z[:, :LANES], z[:, LANES:]

    def rope(z):
        return _rope(z, cos, sina, sinb)

    def put(off, val):
        zs_ref[:, off:off + LANES] = val

    for c in range(PAIRS_A // 2):
        z0, z1 = mm(c * 2 * LANES)
        put(ZS_QA + c * 2 * LANES, rope(z0) * SCALE)
        put(ZS_QA + c * 2 * LANES + LANES, rope(z1) * SCALE)
    for c in range(B_W // (2 * LANES)):
        z0, z1 = mm(OFF_QB + c * 2 * LANES)
        put(ZS_QB + c * 2 * LANES, rope(z0) * SCALE)
        put(ZS_QB + c * 2 * LANES + LANES, rope(z1) * SCALE)

    zk, zv = mm(OFF_KA)
    ka = rope(zk)
    kvt_ref[KT_KA:KT_KA + KV_A, :] = ka.T
    kvt_ref[KT_VA:KT_VA + KV_A, :] = zv.T
    reps = GROUP_A * HEAD_DIM // LANES
    for base, val in ((ZS_KA, ka), (ZS_VA, zv)):
        d0, d1 = _dup_halves(val, lo)
        for kv, dup in enumerate((d0, d1)):
            for c in range(reps):
                put(base + (kv * reps + c) * LANES, dup)

    for c in range(B_W // (2 * LANES)):
        k0, k1 = mm(OFF_KB + c * 2 * LANES)
        v0, v1 = mm(OFF_VB + c * 2 * LANES)
        k0 = rope(k0)
        k1 = rope(k1)
        for j, (kk, vv) in enumerate(((k0, v0), (k1, v1))):
            off = (2 * c + j) * LANES
            put(ZS_KB + off, kk)
            put(ZS_VB + off, vv)
            kvt_ref[KT_KB + off:KT_KB + off + LANES, :] = kk.T
            kvt_ref[KT_VB + off:KT_VB + off + LANES, :] = vv.T

    for c in range(2 * D_MODEL // (2 * LANES)):
        z0, z1 = mm(OFF_GATE + c * 2 * LANES)
        gate_ref[:, c * 2 * LANES:c * 2 * LANES + LANES] = _sigmoid(z0)
        gate_ref[:, c * 2 * LANES + LANES:(c + 1) * 2 * LANES] = _sigmoid(z1)


def _proj_sample(x, g1, w_in, tables):
    nb = x.shape[0]
    cos, sina, sinb = tables
    return pl.pallas_call(
        _proj_sample_kernel,
        out_shape=[
            jax.ShapeDtypeStruct((nb, ZS_W), F32),
            jax.ShapeDtypeStruct((KT_W, nb), F32),
            jax.ShapeDtypeStruct((nb, 2 * D_MODEL), F32),
        ],
        compiler_params=pltpu.CompilerParams(vmem_limit_bytes=VMEM_LIMIT),
        name="proj_sample",
    )(x, g1, w_in, cos, sina, sinb)


SEQS_PER_STEP = 2


def _sample_attn_kernel(zs_ref, kvt_ref, sink_ref, ca_ref, c1_ref, c2_ref, c3_ref,
                        sa_ref, s1_ref, s2_ref, s3_ref, oa_ref, ob_ref):
    for j in range(ca_ref.shape[0]):
        _sample_attn_one(j, zs_ref, kvt_ref, sink_ref, ca_ref, c1_ref, c2_ref, c3_ref,
                         sa_ref, s1_ref, s2_ref, s3_ref, oa_ref, ob_ref)


def _sample_attn_one(j, zs_ref, kvt_ref, sink_ref, ca_ref, c1_ref, c2_ref, c3_ref,
                     sa_ref, s1_ref, s2_ref, s3_ref, oa_ref, ob_ref):
    b = pl.program_id(0) * ca_ref.shape[0] + j
    nb = kvt_ref.shape[1]
    pick = lax.broadcasted_iota(jnp.int32, (1, nb), 1) == b

    def row(off, width):
        return zs_ref[pl.ds(b, 1), off:off + width]

    def new_col(off, width):
        blk = kvt_ref[off:off + width, :]
        return jnp.sum(jnp.where(pick, blk, 0.0), axis=1, keepdims=True)

    def shifted(tile, col):
        w = tile.shape[1]
        lane = lax.broadcasted_iota(jnp.int32, tile.shape, 1)
        return jnp.where(lane == w - 1, col, pltpu.roll(tile, w - 1, 1))

    def head_mask(nh):
        rows = max(nh, 8)
        hid = lax.broadcasted_iota(jnp.int32, (rows, nh * HEAD_DIM), 0)
        lh = lax.broadcasted_iota(jnp.int32, (rows, nh * HEAD_DIM), 1) // HEAD_DIM
        return hid == lh

    def attend(q_row, kt, vt, knew_row, vnew_row, hm, dil, sink_col):
        qm = jnp.where(hm, q_row, 0.0)
        s = jnp.dot(qm.astype(BF16), kt.astype(BF16), preferred_element_type=F32)
        if dil > 1:
            lane = lax.broadcasted_iota(jnp.int32, s.shape, 1)
            s = jnp.where(lane % dil == 0, s, NEG_INF)
        s_new = jnp.sum(qm * knew_row, axis=1, keepdims=True)
        m = jnp.maximum(jnp.max(s, axis=1, keepdims=True), s_new)
        if sink_col is not None:
            m = jnp.maximum(m, sink_col)
        e = jnp.exp(s - m)
        e_new = jnp.exp(s_new - m)
        l = jnp.sum(e, axis=1, keepdims=True) + e_new
        if sink_col is not None:
            l = l + jnp.exp(sink_col - m)
        num = lax.dot_general(e.astype(BF16), vt.astype(BF16), (((1,), (1,)), ((), ())),
                              preferred_element_type=F32)
        return num + e_new * vnew_row, m, l

    hm_a = head_mask(GROUP_A)
    wq = GROUP_A * HEAD_DIM
    for kv in range(N_KV_A):
        kt = ca_ref[j, 0, kv * HEAD_DIM:(kv + 1) * HEAD_DIM, :]
        vt = ca_ref[j, 1, kv * HEAD_DIM:(kv + 1) * HEAD_DIM, :]
        num, m, l = attend(row(ZS_QA + kv * wq, wq),
                           jnp.concatenate([kt] * GROUP_A, axis=0),
                           jnp.concatenate([vt] * GROUP_A, axis=0),
                           row(ZS_KA + kv * wq, wq), row(ZS_VA + kv * wq, wq),
                           hm_a, 1, sink_ref[kv * GROUP_A:(kv + 1) * GROUP_A, :])
        o = jnp.sum(jnp.where(hm_a, num / l, 0.0), axis=0, keepdims=True)
        oa_ref[pl.ds(b, 1), kv * wq:(kv + 1) * wq] = o
    sa_ref[j, 0] = shifted(ca_ref[j, 0], new_col(KT_KA, KV_A))
    sa_ref[j, 1] = shifted(ca_ref[j, 1], new_col(KT_VA, KV_A))

    hm_b = head_mask(HEADS_B)
    nums, ms, ls = [], [], []
    for g, (c_ref, s_ref, (_, dil)) in enumerate(zip((c1_ref, c2_ref, c3_ref), (s1_ref, s2_ref, s3_ref),
                                                      DILATED_PAIRS)):
        kt = c_ref[j, 0]
        vt = c_ref[j, 1]
        num, m, l = attend(row(ZS_QB + g * GRP_B, GRP_B), kt, vt,
                           row(ZS_KB + g * GRP_B, GRP_B), row(ZS_VB + g * GRP_B, GRP_B),
                           hm_b, dil, None)
        nums.append(num)
        ms.append(m)
        ls.append(l)
        s_ref[j, 0] = shifted(kt, new_col(KT_KB + g * GRP_B, GRP_B))
        s_ref[j, 1] = shifted(vt, new_col(KT_VB + g * GRP_B, GRP_B))
    mx = jnp.maximum(jnp.maximum(ms[0], ms[1]), ms[2])
    ws = [jnp.exp(m - mx) for m in ms]
    num = ws[0] * nums[0] + ws[1] * nums[1] + ws[2] * nums[2]
    den = ws[0] * ls[0] + ws[1] * ls[1] + ws[2] * ls[2]
    ob_ref[pl.ds(b, 1), :] = jnp.sum(jnp.where(hm_b, num / den, 0.0), axis=0, keepdims=True)


def _sample_attn(zs, kvt, sinks_col, caches):
    nb = zs.shape[0]
    full2 = lambda a: pl.BlockSpec(a.shape, lambda b: (0, 0))
    cspec = lambda a: pl.BlockSpec((SEQS_PER_STEP,) + a.shape[1:], lambda b: (b, 0, 0, 0))
    return pl.pallas_call(
        _sample_attn_kernel,
        grid=(nb // SEQS_PER_STEP,),
        in_specs=[full2(zs), full2(kvt), full2(sinks_col)] + [cspec(c) for c in caches],
        out_specs=[cspec(c) for c in caches] + [
            pl.BlockSpec((nb, Q_A), lambda b: (0, 0)),
            pl.BlockSpec((nb, GRP_B), lambda b: (0, 0)),
        ],
        out_shape=[jax.ShapeDtypeStruct(c.shape, F32) for c in caches] + [
            jax.ShapeDtypeStruct((nb, Q_A), F32),
            jax.ShapeDtypeStruct((nb, GRP_B), F32),
        ],
        compiler_params=pltpu.CompilerParams(
            dimension_semantics=("arbitrary",), vmem_limit_bytes=VMEM_LIMIT),
        name="sample_attn",
    )(zs, kvt, sinks_col, *caches)


def _rows_minor(c):
    _, bsz, rows, two, nh, hd = c.shape
    return jnp.transpose(c, (0, 1, 3, 4, 5, 2)).reshape(bsz, two, nh * hd, rows)


def _rows_major(c, nh):
    bsz, two, _, rows = c.shape
    return jnp.transpose(c.reshape(1, bsz, two, nh, HEAD_DIM, rows), (0, 1, 5, 2, 3, 4))


def kernel(x_prompt, x_sample, cache_a_kv, cache_b1_kv, cache_b2_kv, cache_b3_kv, norm1_g, w_in, sinks,
           w_pa, w_pb, w_o, norm2_g, w_gu, w_down, final_norm_g):
    bsz, seq, _ = x_prompt.shape
    nb, dec_seq, _ = x_sample.shape
    assert dec_seq == 1 and norm1_g.shape[0] == 1
    assert cache_a_kv.shape[2] == WINDOW_A
    assert tuple(c.shape[2] for c in (cache_b1_kv, cache_b2_kv, cache_b3_kv)) == tuple(
        w for w, _ in DILATED_PAIRS)

    w_in_b = w_in[0].astype(BF16)
    gf = final_norm_g.reshape(1, D_MODEL)

    tables_p = _rope_tables(jnp.arange(seq, dtype=jnp.int32))
    (qa, kad, vad, b1, b2, b3, gate, pa, pb1, pb2, pb3) = _proj_prompt(
        x_prompt, norm1_g, w_in_b, tables_p, TM_PROJ)
    oa, w_pa_b, w_pb_b, w_o_b, w_gu_b, w_dn_b = _attn_a(
        sinks[0], qa, kad, vad, (w_pa[0], w_pb[0], w_o[0], w_gu[0], w_down[0]))
    ob = _attn_b(b1, b2, b3)
    y_prompt = _finish(x_prompt, oa, ob, gate, w_pa_b, w_pb_b, w_o_b, norm2_g, w_gu_b, w_dn_b, gf, TM_FINISH)

    xs = x_sample.reshape(nb, D_MODEL)
    tables_s = _rope_tables(jnp.full((nb,), PAST_LEN, dtype=jnp.int32))
    zs, kvt, gate_s = _proj_sample(xs, norm1_g, w_in_b, tables_s)
    caches = [_rows_minor(c) for c in (cache_a_kv, cache_b1_kv, cache_b2_kv, cache_b3_kv)]
    sa, sb1, sb2, sb3, oa_s, ob_s = _sample_attn(zs, kvt, sinks[0].reshape(N_HEADS_A, 1), caches)
    oa_s = jnp.transpose(oa_s.reshape(nb, PAIRS_A, LANES), (1, 0, 2))[None].astype(BF16)
    y_sample = _finish(xs[None], oa_s, ob_s[None].astype(BF16), gate_s[None], w_pa_b, w_pb_b, w_o_b,
                       norm2_g, w_gu_b, w_dn_b, gf, nb)

    return (y_prompt, y_sample.reshape(nb, 1, D_MODEL),
            _rows_major(pa, N_KV_A), _rows_major(pb1, HEADS_B), _rows_major(pb2, HEADS_B),
            _rows_major(pb3, HEADS_B),
            _rows_major(sa, N_KV_A), _rows_major(sb1, HEADS_B), _rows_major(sb2, HEADS_B),
            _rows_major(sb3, HEADS_B))
```

```python
import jax
import jax.numpy as jnp
from jax import lax
from jax.experimental import pallas as pl
from jax.experimental.pallas import tpu as pltpu

F32 = jnp.float32
BF16 = jnp.bfloat16

D_MODEL = 1024
HEAD_DIM = 64
HALF = HEAD_DIM // 2
N_HEADS_A = 16
N_KV_A = 2
GROUP_A = N_HEADS_A // N_KV_A
WINDOW_A = 128
DILATED_PAIRS = ((128, 1), (512, 4), (2048, 16))
N_GROUPS_B = 3
HEADS_B = 4
SPAN = 128
PAST_LEN = 16384
ROPE_THETA = 10000.0
NORM_EPS = 1e-6
NEG_INF = -1e30
SCALE = HEAD_DIM ** -0.5

LANES = 128
PAIRS_A = N_HEADS_A // 2
PAIRS_B = HEADS_B // 2
Q_A = N_HEADS_A * HEAD_DIM
KV_A = N_KV_A * HEAD_DIM
GRP_B = HEADS_B * HEAD_DIM
B_W = N_GROUPS_B * GRP_B
OFF_KA = Q_A
OFF_VA = OFF_KA + KV_A
OFF_QB = OFF_VA + KV_A
OFF_KB = OFF_QB + B_W
OFF_VB = OFF_KB + B_W
OFF_GATE = OFF_VB + B_W
QBLK = 128
B_PARTS = 4

VMEM_LIMIT = 56 * 1024 * 1024
TM_PROJ = 512
TM_FINISH = 512


def _rms(x, g):
    return x * lax.rsqrt(jnp.mean(x * x, axis=-1, keepdims=True) + NORM_EPS) * g


def _rope(z, cos, sina, sinb):
    return z * cos + pltpu.roll(z, LANES - HALF, 1) * sina + pltpu.roll(z, HALF, 1) * sinb


def _sigmoid(z):
    return 1.0 / (1.0 + jnp.exp(-z))


def _dup_halves(z, lo):
    r = pltpu.roll(z, HEAD_DIM, 1)
    return jnp.where(lo, z, r), jnp.where(lo, r, z)


def _rope_tables(pos):
    inv = 1.0 / (ROPE_THETA ** (jnp.arange(HALF, dtype=F32) / HALF))
    ang = pos.astype(F32)[:, None] * inv[None, :]
    ang = jnp.concatenate([ang, ang, ang, ang], axis=-1)
    first = (jnp.arange(LANES) % HEAD_DIM) < HALF
    cos = jnp.cos(ang)
    sin = jnp.sin(ang)
    return cos, jnp.where(first, -sin, 0.0), jnp.where(first, 0.0, sin)


def _proj_prompt_kernel(x_ref, g_ref, w_ref, cos_ref, sina_ref, sinb_ref,
                        qa_ref, kad_ref, vad_ref, b1_ref, b2_ref, b3_ref, gate_ref,
                        pa_ref, pb1_ref, pb2_ref, pb3_ref, scr_ref, scr2_ref):
    tm = x_ref.shape[1]
    i = pl.program_id(1)
    last = pl.num_programs(1) - 1
    h = _rms(x_ref[0], g_ref[...]).astype(BF16)
    cos = cos_ref[...]
    sina = sina_ref[...]
    sinb = sinb_ref[...]
    lo = lax.broadcasted_iota(jnp.int32, (tm, LANES), 1) < HEAD_DIM

    wide = 8 * LANES
    chunks = {}

    def mm(ci):
        cols = slice(ci * wide, min((ci + 1) * wide, w_ref.shape[1]))
        chunks[ci] = jnp.dot(h, w_ref[:, cols], preferred_element_type=F32)

    def rope(z):
        return _rope(z, cos, sina, sinb)

    offs = ([c * 2 * LANES for c in range(PAIRS_A // 2)] + [OFF_KA]
            + [o + g * GRP_B for g in range(N_GROUPS_B) for o in (OFF_QB, OFF_KB, OFF_VB)]
            + [OFF_GATE + c * 2 * LANES for c in range(2 * D_MODEL // (2 * LANES))])
    mm(0)

    def take(off):
        nxt = offs.index(off) + 1
        for o in offs[nxt - 1:nxt + 1]:
            if o // wide not in chunks:
                mm(o // wide)
        ci, within = divmod(off, wide)
        z = chunks[ci]
        return z[:, within:within + LANES], z[:, within + LANES:within + 2 * LANES]

    for c in range(PAIRS_A // 2):
        z0, z1 = take(c * 2 * LANES)
        qa_ref[0, 2 * c] = (rope(z0) * SCALE).astype(BF16)
        qa_ref[0, 2 * c + 1] = (rope(z1) * SCALE).astype(BF16)

    zk, zv = take(OFF_KA)
    ka = rope(zk)
    k0, k1 = _dup_halves(ka, lo)
    v0, v1 = _dup_halves(zv, lo)
    kad_ref[0, 0] = k0.astype(BF16)
    kad_ref[0, 1] = k1.astype(BF16)
    for kv, vd in enumerate((v0, v1)):
        vad_ref[0, kv, 0] = jnp.where(lo, vd, 1.0).astype(BF16)
        vad_ref[0, kv, 1] = jnp.where(lo, 1.0, vd).astype(BF16)

    @pl.when(i == last)
    def _():
        pa_ref[0, 0] = ka[tm - WINDOW_A:, :].T
        pa_ref[0, 1] = zv[tm - WINDOW_A:, :].T

    b_refs = (b1_ref, b2_ref, b3_ref)
    for g, (win, dil) in enumerate(DILATED_PAIRS):
        q0, q1 = take(OFF_QB + g * GRP_B)
        k0, k1 = take(OFF_KB + g * GRP_B)
        v0, v1 = take(OFF_VB + g * GRP_B)
        q0 = rope(q0) * SCALE
        q1 = rope(q1) * SCALE
        k0 = rope(k0)
        k1 = rope(k1)
        parts = ((q0, q1), (k0, k1),
                 (jnp.where(lo, v0, 1.0), jnp.where(lo, v1, 1.0)),
                 (jnp.where(lo, 1.0, v0), jnp.where(lo, 1.0, v1)))
        for part in range(B_PARTS):
            for pr in range(PAIRS_B):
                val = parts[part][pr]
                if dil == 1:
                    b_refs[g][0, part, pr, 0] = val.astype(BF16)
                elif dil == 16:
                    slot = part * PAIRS_B + pr
                    scr_ref[slot] = val
                    q4 = tm // 4
                    for r4 in range(4):
                        scr2_ref[slot, r4 * q4:(r4 + 1) * q4, :] = scr_ref[slot, pl.ds(r4, q4, stride=4), :]
                    for r4 in range(4):
                        for r16 in range(4):
                            b_refs[g][0, part, pr, r4 + 4 * r16] = scr2_ref[
                                slot, pl.ds(r4 * q4 + r16, q4 // 4, stride=4), :].astype(BF16)
                else:
                    slot = part * PAIRS_B + pr
                    scr_ref[slot] = val
                    for r in range(dil):
                        b_refs[g][0, part, pr, r] = scr_ref[
                            slot, pl.ds(r, tm // dil, stride=dil), :].astype(BF16)
        kt = jnp.concatenate([k0, k1], axis=-1).T
        vt = jnp.concatenate([v0, v1], axis=-1).T
        if g == 2:
            pb3_ref[0, 0] = kt
            pb3_ref[0, 1] = vt
        else:
            st_ref = pb1_ref if g == 0 else pb2_ref
            rows = st_ref.shape[3]

            @pl.when(i == last)
            def _(kt=kt, vt=vt, st_ref=st_ref, rows=rows):
                st_ref[0, 0] = kt[:, tm - rows:]
                st_ref[0, 1] = vt[:, tm - rows:]

    for c in range(2 * D_MODEL // (2 * LANES)):
        z0, z1 = take(OFF_GATE + c * 2 * LANES)
        gate_ref[0, :, c * 2 * LANES:c * 2 * LANES + LANES] = _sigmoid(z0)
        gate_ref[0, :, c * 2 * LANES + LANES:(c + 1) * 2 * LANES] = _sigmoid(z1)


def _proj_prompt(x, g1, w_in, tables, tm):
    bsz, seq, _ = x.shape
    nt = seq // tm
    cos, sina, sinb = tables
    const2 = lambda b, i: (0, 0)
    tab_spec = pl.BlockSpec((tm, LANES), lambda b, i: (i, 0))
    out_shape = [
        jax.ShapeDtypeStruct((bsz, PAIRS_A, seq, LANES), BF16),
        jax.ShapeDtypeStruct((bsz, N_KV_A, seq, LANES), BF16),
        jax.ShapeDtypeStruct((bsz, N_KV_A, 2, seq, LANES), BF16),
    ]
    out_specs = [
        pl.BlockSpec((1, PAIRS_A, tm, LANES), lambda b, i: (b, 0, i, 0)),
        pl.BlockSpec((1, N_KV_A, tm, LANES), lambda b, i: (b, 0, i, 0)),
        pl.BlockSpec((1, N_KV_A, 2, tm, LANES), lambda b, i: (b, 0, 0, i, 0)),
    ]
    for _, dil in DILATED_PAIRS:
        out_shape.append(jax.ShapeDtypeStruct((bsz, B_PARTS, PAIRS_B, dil, seq // dil, LANES), BF16))
        out_specs.append(pl.BlockSpec((1, B_PARTS, PAIRS_B, dil, tm // dil, LANES),
                                      lambda b, i: (b, 0, 0, 0, i, 0)))
    out_shape.append(jax.ShapeDtypeStruct((bsz, seq, 2 * D_MODEL), F32))
    out_specs.append(pl.BlockSpec((1, tm, 2 * D_MODEL), lambda b, i: (b, i, 0)))
    out_shape.append(jax.ShapeDtypeStruct((bsz, 2, KV_A, min(WINDOW_A, seq)), F32))
    out_specs.append(pl.BlockSpec((1, 2, KV_A, min(WINDOW_A, seq)), lambda b, i: (b, 0, 0, 0)))
    for g, (win, _) in enumerate(DILATED_PAIRS):
        rows = min(win, seq)
        out_shape.append(jax.ShapeDtypeStruct((bsz, 2, GRP_B, rows), F32))
        if g == 2:
            out_specs.append(pl.BlockSpec((1, 2, GRP_B, tm), lambda b, i: (b, 0, 0, i)))
        else:
            out_specs.append(pl.BlockSpec((1, 2, GRP_B, rows), lambda b, i: (b, 0, 0, 0)))
    return pl.pallas_call(
        _proj_prompt_kernel,
        grid=(bsz, nt),
        in_specs=[
            pl.BlockSpec((1, tm, D_MODEL), lambda b, i: (b, i, 0)),
            pl.BlockSpec((1, D_MODEL), const2),
            pl.BlockSpec(w_in.shape, const2, pipeline_mode=pl.Buffered(1)),
            tab_spec, tab_spec, tab_spec,
        ],
        out_specs=out_specs,
        out_shape=out_shape,
        scratch_shapes=[pltpu.VMEM((B_PARTS * PAIRS_B, tm, LANES), F32),
                        pltpu.VMEM((B_PARTS * PAIRS_B, tm, LANES), F32)],
        compiler_params=pltpu.CompilerParams(
            dimension_semantics=("arbitrary", "arbitrary"), vmem_limit_bytes=VMEM_LIMIT),
        name="proj_prompt",
    )(x, g1, w_in, cos, sina, sinb)


def _band_biases():
    qi = lax.broadcasted_iota(jnp.int32, (QBLK, 2 * QBLK), 0)
    ki = lax.broadcasted_iota(jnp.int32, (QBLK, 2 * QBLK), 1)
    band = jnp.where((ki >= qi) & (ki <= qi + SPAN), 0.0, NEG_INF).astype(F32)
    first = jnp.where(ki <= qi, 0.0, NEG_INF).astype(F32)
    return first, band, first[:, :QBLK]


def _head_scores(q2, k, bias, lo, hh):
    qm = jnp.where(lo if hh == 0 else jnp.logical_not(lo), q2, jnp.zeros_like(q2))
    s = lax.dot_general(qm, k, (((1,), (1,)), ((), ())), preferred_element_type=F32)
    return s + bias


def _pair_pv(e0, e1, v_lo, v_hi, lo):
    pv0 = jnp.dot(e0, v_lo, preferred_element_type=F32)
    pv1 = jnp.dot(e1, v_hi, preferred_element_type=F32)
    num = jnp.where(lo, pv0, pv1)
    den = pltpu.roll(jnp.where(lo, pv1, pv0), HEAD_DIM, 1)
    return num, den


def _blk(j):
    return j * QBLK if isinstance(j, int) else pl.multiple_of(j * QBLK, QBLK)


def _win(j):
    return max(j - 1, 0) * QBLK if isinstance(j, int) else pl.multiple_of(jnp.maximum(j - 1, 0) * QBLK, QBLK)


def _attn_a_kernel(sink_ref, qa_ref, kd_ref, va_ref, *rest):
    nw = (len(rest) - 3) // 2
    oa_ref = rest[nw]
    s_scr, m_scr = rest[2 * nw + 1:]
    for src, dst in zip(rest[:nw], rest[nw + 1:2 * nw + 1]):
        dst[...] = src[...].astype(BF16)
    seq = qa_ref.shape[2]
    nblk = seq // QBLK
    first, band, _ = _band_biases()
    lo = lax.broadcasted_iota(jnp.int32, (QBLK, LANES), 1) < HEAD_DIM
    pairs_per_kv = PAIRS_A // N_KV_A

    def stage_qk(jb, slot, bias):
        for kv in range(N_KV_A):
            k = kd_ref[0, kv, pl.ds(_win(jb), 2 * QBLK), :]
            for pr in range(pairs_per_kv):
                pair = kv * pairs_per_kv + pr
                q2 = qa_ref[0, pair, pl.ds(_blk(jb), QBLK), :]
                for hh in range(2):
                    h = 2 * pair + hh
                    s = _head_scores(q2, k, bias, lo, hh)
                    s_scr[slot, h] = s
                    m = jnp.broadcast_to(jnp.max(s, axis=1, keepdims=True), (QBLK, LANES))
                    m_scr[slot, h] = jnp.maximum(m, sink_ref[h])

    def stage_spv(jb, slot):
        for kv in range(N_KV_A):
            v_lo = va_ref[0, kv, 0, pl.ds(_win(jb), 2 * QBLK), :]
            v_hi = va_ref[0, kv, 1, pl.ds(_win(jb), 2 * QBLK), :]
            for pr in range(pairs_per_kv):
                pair = kv * pairs_per_kv + pr
                es, ms = [], []
                for hh in range(2):
                    h = 2 * pair + hh
                    m = m_scr[slot, h]
                    es.append(jnp.exp(s_scr[slot, h] - jnp.tile(m, (1, 2))).astype(BF16))
                    ms.append(m)
                sink_pair = jnp.where(lo, sink_ref[2 * pair], sink_ref[2 * pair + 1])
                num, den = _pair_pv(es[0], es[1], v_lo, v_hi, lo)
                den = den + jnp.exp(sink_pair - jnp.where(lo, ms[0], ms[1]))
                oa_ref[0, pair, pl.ds(_blk(jb), QBLK), :] = (num / den).astype(BF16)

    stage_qk(0, 0, first)
    stage_qk(1, 1, band)
    stage_spv(0, 0)

    def two_blocks(t):
        stage_qk(t, 0, band)
        stage_spv(t - 1, 1)
        stage_qk(t + 1, 1, band)
        stage_spv(t, 0)

    def body(i, carry):
        t = 2 + 4 * i
        two_blocks(t)
        two_blocks(t + 2)
        return carry

    nloop = (nblk - 2) // 4
    lax.fori_loop(0, nloop, body, 0)
    for t in range(2 + 4 * nloop, nblk, 2):
        two_blocks(t)
    stage_spv(nblk - 1, 1)


def _attn_a(sinks, qa, kad, vad, weights):
    bsz, _, seq, _ = qa.shape
    assert (seq // QBLK) % 2 == 0 and seq >= 2 * QBLK
    bf16_rows = 16
    assert all(w.shape[0] % (bsz * bf16_rows) == 0 for w in weights)
    wspecs = [pl.BlockSpec((w.shape[0] // bsz, w.shape[1]), lambda b: (b, 0)) for w in weights]
    return pl.pallas_call(
        _attn_a_kernel,
        grid=(bsz,),
        in_specs=[
            pl.BlockSpec(memory_space=pltpu.SMEM),
            pl.BlockSpec((1, PAIRS_A, seq, LANES), lambda b: (b, 0, 0, 0)),
            pl.BlockSpec((1, N_KV_A, seq, LANES), lambda b: (b, 0, 0, 0)),
            pl.BlockSpec((1, N_KV_A, 2, seq, LANES), lambda b: (b, 0, 0, 0, 0)),
        ] + wspecs,
        out_specs=[pl.BlockSpec((1, PAIRS_A, seq, LANES), lambda b: (b, 0, 0, 0))] + wspecs,
        out_shape=[jax.ShapeDtypeStruct((bsz, PAIRS_A, seq, LANES), BF16)] + [
            jax.ShapeDtypeStruct(w.shape, BF16) for w in weights],
        scratch_shapes=[pltpu.VMEM((2, N_HEADS_A, QBLK, 2 * QBLK), F32),
                        pltpu.VMEM((2, N_HEADS_A, QBLK, LANES), F32)],
        compiler_params=pltpu.CompilerParams(
            dimension_semantics=("arbitrary",), vmem_limit_bytes=VMEM_LIMIT),
        name="attn_a",
    )(sinks, qa, kad, vad, *weights)


def _attn_b_kernel(b1_ref, b2_ref, b3_ref, ob_ref, o_scr, lse_scr, s12_scr, s3_scr, e12_scr, e3_scr, m_scr,
                   mp_scr, il_scr):
    seq = ob_ref.shape[1]
    nunit = seq // QBLK
    first, band, causal = _band_biases()
    lo = lax.broadcasted_iota(jnp.int32, (QBLK, LANES), 1) < HEAD_DIM
    refs = (b1_ref, b2_ref, b3_ref)
    dils = tuple(d for _, d in DILATED_PAIRS)
    nblks = tuple(nunit // d for d in dils)

    def coords(g, u):
        if nblks[g] == 1:
            return u, 0
        if nblks[g] == nunit:
            return 0, u
        return u // nblks[g], u % nblks[g]

    def stage_qk(u, slot):
        for g in range(N_GROUPS_B):
            r, jb = coords(g, u)
            for pr in range(PAIRS_B):
                q2 = refs[g][0, 0, pr, r, pl.ds(_blk(jb), QBLK), :]
                if nblks[g] == 1:
                    k = refs[g][0, 1, pr, r, :, :]
                    bias = causal
                else:
                    k = refs[g][0, 1, pr, r, pl.ds(_win(jb), 2 * QBLK), :]
                    if isinstance(jb, int):
                        bias = first if jb == 0 else band
                    elif nblks[g] == nunit:
                        bias = band
                    else:
                        bias = jnp.where(jb == 0, first, band)
                for hh in range(2):
                    s = _head_scores(q2, k, bias, lo, hh)
                    if nblks[g] == 1:
                        s3_scr[slot, 2 * pr + hh] = s
                    else:
                        s12_scr[slot, g, 2 * pr + hh] = s
                    m_scr[slot, g, 2 * pr + hh] = jnp.broadcast_to(
                        jnp.max(s, axis=1, keepdims=True), (QBLK, LANES))

    def stage_sm(slot):
        for g in range(N_GROUPS_B):
            for h in range(2 * PAIRS_B):
                m = m_scr[slot, g, h]
                if nblks[g] == 1:
                    e3_scr[slot, h] = jnp.exp(s3_scr[slot, h] - m).astype(BF16)
                else:
                    e12_scr[slot, g, h] = jnp.exp(s12_scr[slot, g, h] - jnp.tile(m, (1, 2))).astype(BF16)
            for pr in range(PAIRS_B):
                mp_scr[slot, g, pr] = jnp.where(lo, m_scr[slot, g, 2 * pr], m_scr[slot, g, 2 * pr + 1])

    def stage_pv(u, slot):
        for g in range(N_GROUPS_B):
            r, jb = coords(g, u)
            for pr in range(PAIRS_B):
                if nblks[g] == 1:
                    v_lo = refs[g][0, 2, pr, r, :, :]
                    v_hi = refs[g][0, 3, pr, r, :, :]
                    e0, e1 = e3_scr[slot, 2 * pr], e3_scr[slot, 2 * pr + 1]
                else:
                    v_lo = refs[g][0, 2, pr, r, pl.ds(_win(jb), 2 * QBLK), :]
                    v_hi = refs[g][0, 3, pr, r, pl.ds(_win(jb), 2 * QBLK), :]
                    e0, e1 = e12_scr[slot, g, 2 * pr], e12_scr[slot, g, 2 * pr + 1]
                num, den = _pair_pv(e0, e1, v_lo, v_hi, lo)
                if dils[g] == 16:
                    idx = pl.ds((r % 4) * (seq // 4) + r // 4, QBLK, stride=4)
                elif dils[g] == 1:
                    idx = pl.ds(_blk(jb) + r, QBLK)
                else:
                    idx = pl.ds(_blk(jb) * dils[g] + r, QBLK, stride=dils[g])
                o_scr[g, pr, idx, :] = num / den
                lse_scr[g, pr, idx, :] = mp_scr[slot, g, pr] + jnp.log(den)

    stage_qk(0, 0)
    stage_qk(1, 1)
    stage_sm(0)

    def two_units(u):
        stage_qk(u, 0)
        stage_sm(1)
        stage_pv(u - 2, 0)
        stage_qk(u + 1, 1)
        stage_sm(0)
        stage_pv(u - 1, 1)

    for u in range(2, nunit, 2):
        two_units(u)
    stage_sm(1)
    stage_pv(nunit - 2, 0)
    stage_pv(nunit - 1, 1)

    mrows = 2 * QBLK

    def merge(c, carry):
        rows = pl.ds(pl.multiple_of(c * mrows, mrows), mrows)
        for pr in range(PAIRS_B):
            quarter = mrows // 4
            for c4 in range(4):
                src = pl.ds(c4 * (seq // 4) + pl.multiple_of(c * quarter, quarter), quarter)
                il_scr[0, pr, pl.ds(c4, quarter, stride=4), :] = lse_scr[2, pr, src, :]
                il_scr[1, pr, pl.ds(c4, quarter, stride=4), :] = o_scr[2, pr, src, :]
            ls = [lse_scr[0, pr, rows, :], lse_scr[1, pr, rows, :], il_scr[0, pr]]
            mx = jnp.maximum(jnp.maximum(ls[0], ls[1]), ls[2])
            ws = [jnp.exp(l - mx) for l in ls]
            num = ws[0] * o_scr[0, pr, rows, :] + ws[1] * o_scr[1, pr, rows, :] + ws[2] * il_scr[1, pr]
            den = ws[0] + ws[1] + ws[2]
            ob_ref[0, rows, pr * LANES:(pr + 1) * LANES] = (num / den).astype(BF16)
        return carry

    lax.fori_loop(0, seq // mrows, merge, 0)


def _attn_b(b1, b2, b3):
    bsz = b1.shape[0]
    seq = b1.shape[4]
    nunit = seq // QBLK
    assert nunit % 2 == 0 and nunit >= 4 and nunit == DILATED_PAIRS[-1][1]
    assert b1.shape[3:5] == (1, seq) and b3.shape[4] == QBLK
    specs = [pl.BlockSpec((1,) + b.shape[1:], lambda b_: (b_, 0, 0, 0, 0, 0)) for b in (b1, b2, b3)]
    heads = 2 * PAIRS_B
    return pl.pallas_call(
        _attn_b_kernel,
        grid=(bsz,),
        in_specs=specs,
        out_specs=pl.BlockSpec((1, seq, GRP_B), lambda b_: (b_, 0, 0)),
        out_shape=jax.ShapeDtypeStruct((bsz, seq, GRP_B), BF16),
        scratch_shapes=[pltpu.VMEM((N_GROUPS_B, PAIRS_B, seq, LANES), F32),
                        pltpu.VMEM((N_GROUPS_B, PAIRS_B, seq, LANES), F32),
                        pltpu.VMEM((2, 2, heads, QBLK, 2 * QBLK), F32),
                        pltpu.VMEM((2, heads, QBLK, QBLK), F32),
                        pltpu.VMEM((2, 2, heads, QBLK, 2 * QBLK), BF16),
                        pltpu.VMEM((2, heads, QBLK, QBLK), BF16),
                        pltpu.VMEM((2, N_GROUPS_B, heads, QBLK, LANES), F32),
                        pltpu.VMEM((2, N_GROUPS_B, PAIRS_B, QBLK, LANES), F32),
                        pltpu.VMEM((2, PAIRS_B, 2 * QBLK, LANES), F32)],
        compiler_params=pltpu.CompilerParams(
            dimension_semantics=("arbitrary",), vmem_limit_bytes=VMEM_LIMIT),
        name="attn_b",
    )(b1, b2, b3)


def _finish_kernel(x_ref, oa_ref, ob_ref, gate_ref, wpa_ref, wpb_ref, wo_ref, g2_ref,
                   wgu_ref, wdn_ref, gf_ref, y_ref):
    d_ff = wdn_ref.shape[0]
    oa = jnp.concatenate([oa_ref[0, p] for p in range(PAIRS_A)], axis=-1)
    ma = jnp.dot(oa, wpa_ref[...], preferred_element_type=F32)
    mb = jnp.dot(ob_ref[0], wpb_ref[...], preferred_element_type=F32)
    m = gate_ref[0, :, :D_MODEL] * ma + gate_ref[0, :, D_MODEL:] * mb
    x1 = x_ref[0] + jnp.dot(m.astype(BF16), wo_ref[...], preferred_element_type=F32)
    xn = _rms(x1, g2_ref[...]).astype(BF16)
    gu = jnp.dot(xn, wgu_ref[...], preferred_element_type=F32)
    gg = gu[:, :d_ff]
    act = (gg * _sigmoid(gg) * gu[:, d_ff:]).astype(BF16)
    y_ref[0] = _rms(x1 + jnp.dot(act, wdn_ref[...], preferred_element_type=F32), gf_ref[...])


def _finish(x, oa, ob, gate, w_pa, w_pb, w_o, g2, w_gu, w_down, gf, tm):
    bsz, seq, _ = x.shape
    const2 = lambda b, i: (0, 0)
    wspec = lambda w: pl.BlockSpec(w.shape, const2, pipeline_mode=pl.Buffered(1))
    return pl.pallas_call(
        _finish_kernel,
        grid=(bsz, seq // tm),
        in_specs=[
            pl.BlockSpec((1, tm, D_MODEL), lambda b, i: (b, i, 0)),
            pl.BlockSpec((1, PAIRS_A, tm, LANES), lambda b, i: (b, 0, i, 0)),
            pl.BlockSpec((1, tm, GRP_B), lambda b, i: (b, i, 0)),
            pl.BlockSpec((1, tm, 2 * D_MODEL), lambda b, i: (b, i, 0)),
            wspec(w_pa), wspec(w_pb), wspec(w_o),
            pl.BlockSpec((1, D_MODEL), const2),
            wspec(w_gu), wspec(w_down),
            pl.BlockSpec((1, D_MODEL), const2),
        ],
        out_specs=pl.BlockSpec((1, tm, D_MODEL), lambda b, i: (b, i, 0)),
        out_shape=jax.ShapeDtypeStruct((bsz, seq, D_MODEL), F32),
        compiler_params=pltpu.CompilerParams(
            dimension_semantics=("arbitrary", "arbitrary"), vmem_limit_bytes=VMEM_LIMIT),
        name="finish",
    )(x, oa, ob, gate, w_pa, w_pb, w_o, g2, w_gu, w_down, gf)


ZS_QA = 0
ZS_QB = ZS_QA + Q_A
ZS_KA = ZS_QB + B_W
ZS_VA = ZS_KA + N_KV_A * GROUP_A * HEAD_DIM
ZS_KB = ZS_VA + N_KV_A * GROUP_A * HEAD_DIM
ZS_VB = ZS_KB + B_W
ZS_W = ZS_VB + B_W
KT_KA = 0
KT_VA = KT_KA + KV_A
KT_KB = KT_VA + KV_A
KT_VB = KT_KB + B_W
KT_W = KT_VB + B_W


def _proj_sample_kernel(x_ref, g_ref, w_ref, cos_ref, sina_ref, sinb_ref, zs_ref, kvt_ref, gate_ref):
    nb = x_ref.shape[0]
    h = _rms(x_ref[...], g_ref[...]).astype(BF16)
    cos = cos_ref[...]
    sina = sina_ref[...]
    sinb = sinb_ref[...]
    lo = lax.broadcasted_iota(jnp.int32, (nb, LANES), 1) < HEAD_DIM

    def mm(off):
        z = jnp.dot(h, w_ref[:, off:off + 2 * LANES], preferred_element_type=F32)
        return z[:, :LANES], z[:, LANES:]

    def rope(z):
        return _rope(z, cos, sina, sinb)

    def put(off, val):
        zs_ref[:, off:off + LANES] = val

    for c in range(PAIRS_A // 2):
        z0, z1 = mm(c * 2 * LANES)
        put(ZS_QA + c * 2 * LANES, rope(z0) * SCALE)
        put(ZS_QA + c * 2 * LANES + LANES, rope(z1) * SCALE)
    for c in range(B_W // (2 * LANES)):
        z0, z1 = mm(OFF_QB + c * 2 * LANES)
        put(ZS_QB + c * 2 * LANES, rope(z0) * SCALE)
        put(ZS_QB + c * 2 * LANES + LANES, rope(z1) * SCALE)

    zk, zv = mm(OFF_KA)
    ka = rope(zk)
    kvt_ref[KT_KA:KT_KA + KV_A, :] = ka.T
    kvt_ref[KT_VA:KT_VA + KV_A, :] = zv.T
    reps = GROUP_A * HEAD_DIM // LANES
    for base, val in ((ZS_KA, ka), (ZS_VA, zv)):
        d0, d1 = _dup_halves(val, lo)
        for kv, dup in enumerate((d0, d1)):
            for c in range(reps):
                put(base + (kv * reps + c) * LANES, dup)

    for c in range(B_W // (2 * LANES)):
        k0, k1 = mm(OFF_KB + c * 2 * LANES)
        v0, v1 = mm(OFF_VB + c * 2 * LANES)
        k0 = rope(k0)
        k1 = rope(k1)
        for j, (kk, vv) in enumerate(((k0, v0), (k1, v1))):
            off = (2 * c + j) * LANES
            put(ZS_KB + off, kk)
            put(ZS_VB + off, vv)
            kvt_ref[KT_KB + off:KT_KB + off + LANES, :] = kk.T
            kvt_ref[KT_VB + off:KT_VB + off + LANES, :] = vv.T

    for c in range(2 * D_MODEL // (2 * LANES)):
        z0, z1 = mm(OFF_GATE + c * 2 * LANES)
        gate_ref[:, c * 2 * LANES:c * 2 * LANES + LANES] = _sigmoid(z0)
        gate_ref[:, c * 2 * LANES + LANES:(c + 1) * 2 * LANES] = _sigmoid(z1)


def _proj_sample(x, g1, w_in, tables):
    nb = x.shape[0]
    cos, sina, sinb = tables
    return pl.pallas_call(
        _proj_sample_kernel,
        out_shape=[
            jax.ShapeDtypeStruct((nb, ZS_W), F32),
            jax.ShapeDtypeStruct((KT_W, nb), F32),
            jax.ShapeDtypeStruct((nb, 2 * D_MODEL), F32),
        ],
        compiler_params=pltpu.CompilerParams(vmem_limit_bytes=VMEM_LIMIT),
        name="proj_sample",
    )(x, g1, w_in, cos, sina, sinb)


SEQS_PER_STEP = 2


def _sample_attn_kernel(zs_ref, kvt_ref, sink_ref, ca_ref, c1_ref, c2_ref, c3_ref,
                        sa_ref, s1_ref, s2_ref, s3_ref, oa_ref, ob_ref):
    for j in range(ca_ref.shape[0]):
        _sample_attn_one(j, zs_ref, kvt_ref, sink_ref, ca_ref, c1_ref, c2_ref, c3_ref,
                         sa_ref, s1_ref, s2_ref, s3_ref, oa_ref, ob_ref)


def _sample_attn_one(j, zs_ref, kvt_ref, sink_ref, ca_ref, c1_ref, c2_ref, c3_ref,
                     sa_ref, s1_ref, s2_ref, s3_ref, oa_ref, ob_ref):
    b = pl.program_id(0) * ca_ref.shape[0] + j
    nb = kvt_ref.shape[1]
    pick = lax.broadcasted_iota(jnp.int32, (1, nb), 1) == b

    def row(off, width):
        return zs_ref[pl.ds(b, 1), off:off + width]

    def new_col(off, width):
        blk = kvt_ref[off:off + width, :]
        return jnp.sum(jnp.where(pick, blk, 0.0), axis=1, keepdims=True)

    def shifted(tile, col):
        w = tile.shape[1]
        lane = lax.broadcasted_iota(jnp.int32, tile.shape, 1)
        return jnp.where(lane == w - 1, col, pltpu.roll(tile, w - 1, 1))

    def head_mask(nh):
        rows = max(nh, 8)
        hid = lax.broadcasted_iota(jnp.int32, (rows, nh * HEAD_DIM), 0)
        lh = lax.broadcasted_iota(jnp.int32, (rows, nh * HEAD_DIM), 1) // HEAD_DIM
        return hid == lh

    def attend(q_row, kt, vt, knew_row, vnew_row, hm, dil, sink_col):
        qm = jnp.where(hm, q_row, 0.0)
        s = jnp.dot(qm.astype(BF16), kt.astype(BF16), preferred_element_type=F32)
        if dil > 1:
            lane = lax.broadcasted_iota(jnp.int32, s.shape, 1)
            s = jnp.where(lane % dil == 0, s, NEG_INF)
        s_new = jnp.sum(qm * knew_row, axis=1, keepdims=True)
        m = jnp.maximum(jnp.max(s, axis=1, keepdims=True), s_new)
        if sink_col is not None:
            m = jnp.maximum(m, sink_col)
        e = jnp.exp(s - m)
        e_new = jnp.exp(s_new - m)
        l = jnp.sum(e, axis=1, keepdims=True) + e_new
        if sink_col is not None:
            l = l + jnp.exp(sink_col - m)
        num = lax.dot_general(e.astype(BF16), vt.astype(BF16), (((1,), (1,)), ((), ())),
                              preferred_element_type=F32)
        return num + e_new * vnew_row, m, l

    hm_a = head_mask(GROUP_A)
    wq = GROUP_A * HEAD_DIM
    for kv in range(N_KV_A):
        kt = ca_ref[j, 0, kv * HEAD_DIM:(kv + 1) * HEAD_DIM, :]
        vt = ca_ref[j, 1, kv * HEAD_DIM:(kv + 1) * HEAD_DIM, :]
        num, m, l = attend(row(ZS_QA + kv * wq, wq),
                           jnp.concatenate([kt] * GROUP_A, axis=0),
                           jnp.concatenate([vt] * GROUP_A, axis=0),
                           row(ZS_KA + kv * wq, wq), row(ZS_VA + kv * wq, wq),
                           hm_a, 1, sink_ref[kv * GROUP_A:(kv + 1) * GROUP_A, :])
        o = jnp.sum(jnp.where(hm_a, num / l, 0.0), axis=0, keepdims=True)
        oa_ref[pl.ds(b, 1), kv * wq:(kv + 1) * wq] = o
    sa_ref[j, 0] = shifted(ca_ref[j, 0], new_col(KT_KA, KV_A))
    sa_ref[j, 1] = shifted(ca_ref[j, 1], new_col(KT_VA, KV_A))

    hm_b = head_mask(HEADS_B)
    nums, ms, ls = [], [], []
    for g, (c_ref, s_ref, (_, dil)) in enumerate(zip((c1_ref, c2_ref, c3_ref), (s1_ref, s2_ref, s3_ref),
                                                      DILATED_PAIRS)):
        kt = c_ref[j, 0]
        vt = c_ref[j, 1]
        num, m, l = attend(row(ZS_QB + g * GRP_B, GRP_B), kt, vt,
                           row(ZS_KB + g * GRP_B, GRP_B), row(ZS_VB + g * GRP_B, GRP_B),
                           hm_b, dil, None)
        nums.append(num)
        ms.append(m)
        ls.append(l)
        s_ref[j, 0] = shifted(kt, new_col(KT_KB + g * GRP_B, GRP_B))
        s_ref[j, 1] = shifted(vt, new_col(KT_VB + g * GRP_B, GRP_B))
    mx = jnp.maximum(jnp.maximum(ms[0], ms[1]), ms[2])
    ws = [jnp.exp(m - mx) for m in ms]
    num = ws[0] * nums[0] + ws[1] * nums[1] + ws[2] * nums[2]
    den = ws[0] * ls[0] + ws[1] * ls[1] + ws[2] * ls[2]
    ob_ref[pl.ds(b, 1), :] = jnp.sum(jnp.where(hm_b, num / den, 0.0), axis=0, keepdims=True)


def _sample_attn(zs, kvt, sinks_col, caches):
    nb = zs.shape[0]
    full2 = lambda a: pl.BlockSpec(a.shape, lambda b: (0, 0))
    cspec = lambda a: pl.BlockSpec((SEQS_PER_STEP,) + a.shape[1:], lambda b: (b, 0, 0, 0))
    return pl.pallas_call(
        _sample_attn_kernel,
        grid=(nb // SEQS_PER_STEP,),
        in_specs=[full2(zs), full2(kvt), full2(sinks_col)] + [cspec(c) for c in caches],
        out_specs=[cspec(c) for c in caches] + [
            pl.BlockSpec((nb, Q_A), lambda b: (0, 0)),
            pl.BlockSpec((nb, GRP_B), lambda b: (0, 0)),
        ],
        out_shape=[jax.ShapeDtypeStruct(c.shape, F32) for c in caches] + [
            jax.ShapeDtypeStruct((nb, Q_A), F32),
            jax.ShapeDtypeStruct((nb, GRP_B), F32),
        ],
        compiler_params=pltpu.CompilerParams(
            dimension_semantics=("arbitrary",), vmem_limit_bytes=VMEM_LIMIT),
        name="sample_attn",
    )(zs, kvt, sinks_col, *caches)


def _rows_minor(c):
    _, bsz, rows, two, nh, hd = c.shape
    return jnp.transpose(c, (0, 1, 3, 4, 5, 2)).reshape(bsz, two, nh * hd, rows)


def _rows_major(c, nh):
    bsz, two, _, rows = c.shape
    return jnp.transpose(c.reshape(1, bsz, two, nh, HEAD_DIM, rows), (0, 1, 5, 2, 3, 4))


def kernel(x_prompt, x_sample, cache_a_kv, cache_b1_kv, cache_b2_kv, cache_b3_kv, norm1_g, w_in, sinks,
           w_pa, w_pb, w_o, norm2_g, w_gu, w_down, final_norm_g):
    bsz, seq, _ = x_prompt.shape
    nb, dec_seq, _ = x_sample.shape
    assert dec_seq == 1 and norm1_g.shape[0] == 1
    assert cache_a_kv.shape[2] == WINDOW_A
    assert tuple(c.shape[2] for c in (cache_b1_kv, cache_b2_kv, cache_b3_kv)) == tuple(
        w for w, _ in DILATED_PAIRS)

    w_in_b = w_in[0].astype(BF16)
    gf = final_norm_g.reshape(1, D_MODEL)

    tables_p = _rope_tables(jnp.arange(seq, dtype=jnp.int32))
    (qa, kad, vad, b1, b2, b3, gate, pa, pb1, pb2, pb3) = _proj_prompt(
        x_prompt, norm1_g, w_in_b, tables_p, TM_PROJ)
    oa, w_pa_b, w_pb_b, w_o_b, w_gu_b, w_dn_b = _attn_a(
        sinks[0], qa, kad, vad, (w_pa[0], w_pb[0], w_o[0], w_gu[0], w_down[0]))
    ob = _attn_b(b1, b2, b3)
    y_prompt = _finish(x_prompt, oa, ob, gate, w_pa_b, w_pb_b, w_o_b, norm2_g, w_gu_b, w_dn_b, gf, TM_FINISH)

    xs = x_sample.reshape(nb, D_MODEL)
    tables_s = _rope_tables(jnp.full((nb,), PAST_LEN, dtype=jnp.int32))
    zs, kvt, gate_s = _proj_sample(xs, norm1_g, w_in_b, tables_s)
    caches = [_rows_minor(c) for c in (cache_a_kv, cache_b1_kv, cache_b2_kv, cache_b3_kv)]
    sa, sb1, sb2, sb3, oa_s, ob_s = _sample_attn(zs, kvt, sinks[0].reshape(N_HEADS_A, 1), caches)
    oa_s = jnp.transpose(oa_s.reshape(nb, PAIRS_A, LANES), (1, 0, 2))[None].astype(BF16)
    y_sample = _finish(xs[None], oa_s, ob_s[None].astype(BF16), gate_s[None], w_pa_b, w_pb_b, w_o_b,
                       norm2_g, w_gu_b, w_dn_b, gf, nb)

    return (y_prompt, y_sample.reshape(nb, 1, D_MODEL),
            _rows_major(pa, N_KV_A), _rows_major(pb1, HEADS_B), _rows_major(pb2, HEADS_B),
            _rows_major(pb3, HEADS_B),
            _rows_major(sa, N_KV_A), _rows_major(sb1, HEADS_B), _rows_major(sb2, HEADS_B),
            _rows_major(sb3, HEADS_B))
```
